```python
import math
import numpy as np
import jax
import jax.numpy as jnp
from jax import lax


D_MODEL = 1024
BATCH = 8
SEQ = 4096
DEPTH = 1

GRID_W = 64
CTX_LEN = 256
D_MIX = D_MODEL

GDN_HEADS = 4
GDN_DK = 128
GDN_DV = 128
GDN_CONV = 5
GDN_CHUNK = 64
GDN_QK_W = GDN_HEADS * GDN_DK
GDN_V_W = GDN_HEADS * GDN_DV
GDN_COLS = 2 * GDN_QK_W + 2 * GDN_V_W + 4 * GDN_HEADS

RWKV_HEADS = 8
RWKV_HD = 64
RWKV_W = RWKV_HEADS * RWKV_HD
RWKV_DECAY_LORA = 64
RWKV_ICLR_LORA = 64
RWKV_GATE_LORA = 128
RWKV_COLS = 3 * RWKV_W + 2 * RWKV_DECAY_LORA + 2 * RWKV_ICLR_LORA + RWKV_GATE_LORA
IN_COLS = GDN_COLS + RWKV_COLS

PEER_HEADS = 8
PEER_NKEYS = 128
PEER_EXPERTS = PEER_NKEYS * PEER_NKEYS
PEER_DKEY = 256
PEER_TOPK = 16
PEER_BLOCK = 128

NORM_EPS = 1e-6
L2_EPS = 1e-6
RWKV_GN_EPS = 64e-5
N_MOD = 6

kernel_name = 'hybrid_gdn_rwkv7_peer_dit_block'


def split_cols(t, widths):
    idx = [int(i) for i in np.cumsum(widths)[:-1]]
    return jnp.split(t, idx, axis=-1)


def rms_norm(x, gain):
    xf = x.astype(jnp.float32)
    y = xf * lax.rsqrt(jnp.mean(xf * xf, axis=-1, keepdims=True) + NORM_EPS)
    return (y * gain.astype(jnp.float32)).astype(x.dtype)


def l2_normalize(t):
    return t * lax.rsqrt(jnp.sum(t * t, axis=-1, keepdims=True) + L2_EPS)


def ada_mod(cond, w, b):
    m = (jax.nn.silu(cond) @ w + b)[:, None, :]
    return jnp.split(m, N_MOD, axis=-1)


def modulate(h, shift, scale):
    return h * (1 + scale) + shift


def seg_flip(t, n_ctx):
    return jnp.concatenate([jnp.flip(t[:, :n_ctx], axis=1), jnp.flip(t[:, n_ctx:], axis=1)], axis=1)


def dw_conv_centred(x, w):
    pad = (w.shape[0] - 1) // 2
    return lax.conv_general_dilated(x, w[:, None, :].astype(x.dtype), window_strides=(1,),
                                    padding=[(pad, pad)], dimension_numbers=('NWC', 'WIO', 'NWC'),
                                    feature_group_count=x.shape[-1])


def q_shift_grid(p):
    B, L, C = p.shape
    rows = L // GRID_W
    g = p.reshape(B, rows, GRID_W, C)
    q = C // 4
    left = jnp.pad(g[:, :, :-1, :q], ((0, 0), (0, 0), (1, 0), (0, 0)))
    right = jnp.pad(g[:, :, 1:, q:2 * q], ((0, 0), (0, 0), (0, 1), (0, 0)))
    up = jnp.pad(g[:, :-1, :, 2 * q:3 * q], ((0, 0), (1, 0), (0, 0), (0, 0)))
    down = jnp.pad(g[:, 1:, :, 3 * q:], ((0, 0), (0, 1), (0, 0), (0, 0)))
    return jnp.concatenate([left, right, up, down], axis=-1).reshape(B, L, C)


def bi_shift_seq(p):
    h = p.shape[-1] // 2
    prev = jnp.pad(p[:, :-1, :h], ((0, 0), (1, 0), (0, 0)))
    nxt = jnp.pad(p[:, 1:, h:], ((0, 0), (0, 1), (0, 0)))
    return jnp.concatenate([prev, nxt], axis=-1)


def gated_delta_chunked(q, k, v, g, beta):
    B, T, H, Dk = q.shape
    Dv = v.shape[-1]
    C = GDN_CHUNK
    N = T // C

    def to_chunks(t):
        t = t.reshape((B, N, C, H) + t.shape[3:])
        return jnp.moveaxis(t, (1, 3), (0, 2))

    qc, kc, vc = to_chunks(q), to_chunks(k), to_chunks(v)
    gc, bc = to_chunks(g), to_chunks(beta)
    G = jnp.cumsum(gc, axis=-1)
    incl = jnp.tril(jnp.ones((C, C), dtype=bool))
    strict = jnp.tril(jnp.ones((C, C), dtype=bool), -1)
    diff = G[..., :, None] - G[..., None, :]
    gamma = jnp.where(incl, jnp.exp(jnp.where(incl, diff, 0.0)), 0.0)
    kk = jnp.einsum('nbhid,nbhjd->nbhij', kc, kc)
    a_mat = jnp.where(strict, bc[..., :, None] * kk * gamma, 0.0) + jnp.eye(C, dtype=q.dtype)
    u = lax.linalg.triangular_solve(a_mat, bc[..., None] * vc, left_side=True, lower=True)
    w = lax.linalg.triangular_solve(a_mat, (bc * jnp.exp(G))[..., None] * kc, left_side=True, lower=True)
    qk = jnp.einsum('nbhid,nbhjd->nbhij', qc, kc) * gamma

    def step(S, inp):
        q_i, k_i, u_i, w_i, G_i, qk_i = inp
        v_new = u_i - jnp.einsum('bhcd,bhde->bhce', w_i, S)
        o = (jnp.einsum('bhcd,bhde->bhce', q_i * jnp.exp(G_i)[..., None], S)
             + jnp.einsum('bhij,bhje->bhie', qk_i, v_new))
        g_last = G_i[..., -1:]
        S = (S * jnp.exp(g_last)[..., None]
             + jnp.einsum('bhcd,bhce->bhde', k_i * jnp.exp(g_last - G_i)[..., None], v_new))
        return S, o

    S0 = jnp.zeros((B, H, Dk, Dv), q.dtype)
    _, o = lax.scan(step, S0, (qc, kc, u, w, G, qk))
    return jnp.moveaxis(o, (0, 2), (1, 3)).reshape(B, T, H, Dv)


def gdn_mixer(p, n_ctx, conv_w, a_log, dt_bias, norm_w):
    B, T, _ = p.shape
    f32 = jnp.float32
    qkv, z, alpha, beta_logit = split_cols(p, [2 * GDN_QK_W + GDN_V_W, GDN_V_W, 2 * GDN_HEADS, 2 * GDN_HEADS])
    qkv = jnp.concatenate([dw_conv_centred(qkv[:, :n_ctx], conv_w), dw_conv_centred(qkv[:, n_ctx:], conv_w)], axis=1)
    qkv = jax.nn.silu(qkv).astype(f32)
    q, k, v = split_cols(qkv, [GDN_QK_W, GDN_QK_W, GDN_V_W])
    q = l2_normalize(q.reshape(B, T, GDN_HEADS, GDN_DK)) * (GDN_DK ** -0.5)
    k = l2_normalize(k.reshape(B, T, GDN_HEADS, GDN_DK))
    v = v.reshape(B, T, GDN_HEADS, GDN_DV)
    alpha = alpha.astype(f32).reshape(B, T, 2, GDN_HEADS)
    g = -jnp.exp(a_log.astype(f32)) * jax.nn.softplus(alpha + dt_bias.astype(f32))
    beta = jax.nn.sigmoid(beta_logit.astype(f32).reshape(B, T, 2, GDN_HEADS))

    def both(t_fwd, t_bwd):
        return jnp.concatenate([t_fwd, seg_flip(t_bwd, n_ctx)], axis=0)

    o2 = gated_delta_chunked(both(q, q), both(k, k), both(v, v),
                             both(g[:, :, 0], g[:, :, 1]), both(beta[:, :, 0], beta[:, :, 1]))
    o = o2[:B] + seg_flip(o2[B:], n_ctx)
    o = (o * lax.rsqrt(jnp.mean(o * o, axis=-1, keepdims=True) + NORM_EPS) * norm_w.astype(f32)
         * jax.nn.silu(z.astype(f32).reshape(B, T, GDN_HEADS, GDN_DV)))
    return o.reshape(B, T, GDN_V_W).astype(p.dtype)


def rwkv7_scan(r, decay, k, v, kk, a):
    def step(S, inp):
        r_t, w_t, k_t, v_t, kk_t, a_t = inp
        sa = jnp.einsum('bhvk,bhk->bhv', S, -kk_t)
        S = (S * w_t[:, :, None, :] + sa[..., None] * (kk_t * a_t)[:, :, None, :]
             + v_t[..., None] * k_t[:, :, None, :])
        return S, jnp.einsum('bhvk,bhk->bhv', S, r_t)

    B2, T, H, N = r.shape
    S0 = jnp.zeros((B2, H, N, N), jnp.float32)
    xs = tuple(jnp.moveaxis(t, 1, 0) for t in (r, decay, k, v, kk, a))
    _, y = lax.scan(step, S0, xs)
    return jnp.moveaxis(y, 0, 1)


def rwkv7_mixer(p, n_ctx, mu, w0, w2, a0, a2, g2, k_k, k_a, r_k, gn_w, gn_b):
    B, T, _ = p.shape
    f32 = jnp.float32
    shifted = jnp.concatenate([bi_shift_seq(p[:, :n_ctx]), q_shift_grid(p[:, n_ctx:])], axis=1)
    p = (p + mu * (shifted - p)).astype(f32)
    r, k, v, xw, xa, xg = split_cols(p, [RWKV_W, RWKV_W, RWKV_W, 2 * RWKV_DECAY_LORA,
                                         2 * RWKV_ICLR_LORA, RWKV_GATE_LORA])
    xw = xw.reshape(B, T, 2, RWKV_DECAY_LORA)
    xa = xa.reshape(B, T, 2, RWKV_ICLR_LORA)
    w_pre = w0.astype(f32) + jnp.einsum('btdr,drc->btdc', jnp.tanh(xw), w2.astype(f32))
    log_w = -jnp.exp(-jax.nn.softplus(-w_pre) - 0.5)
    a = jax.nn.sigmoid(a0.astype(f32) + jnp.einsum('btdr,drc->btdc', xa, a2.astype(f32)))
    gate = jax.nn.sigmoid(xg) @ g2.astype(f32)
    kk = l2_normalize((k * k_k.astype(f32)).reshape(B, T, RWKV_HEADS, RWKV_HD))
    k_dir = k[:, :, None, :] * (1 + (a - 1) * k_a.astype(f32))

    def heads(t):
        return t.reshape(t.shape[:-1] + (RWKV_HEADS, RWKV_HD))

    def both(t_fwd, t_bwd):
        return jnp.concatenate([t_fwd, seg_flip(t_bwd, n_ctx)], axis=0)

    decay = jnp.exp(log_w)
    y2 = rwkv7_scan(heads(both(r, r)), heads(both(decay[:, :, 0], decay[:, :, 1])),
                    heads(both(k_dir[:, :, 0], k_dir[:, :, 1])), heads(both(v, v)),
                    both(kk, kk), heads(both(a[:, :, 0], a[:, :, 1])))
    y = y2[:B] + seg_flip(y2[B:], n_ctx)
    mean = jnp.mean(y, axis=-1, keepdims=True)
    var = jnp.mean(jnp.square(y - mean), axis=-1, keepdims=True)
    yn = ((y - mean) * lax.rsqrt(var + RWKV_GN_EPS)).reshape(B, T, RWKV_W) * gn_w.astype(f32) + gn_b.astype(f32)
    rk = (r[:, :, None, :] * k_dir * r_k.astype(f32)).reshape(B, T, 2, RWKV_HEADS, RWKV_HD)
    bonus = jnp.sum(jnp.sum(rk, axis=-1, keepdims=True), axis=2) * heads(v)
    out = (yn + bonus.reshape(B, T, RWKV_W)) * gate
    return out.astype(mu.dtype)


def peer_ffn(h, w_query, sub_keys, down, up):
    B, T, D = h.shape
    q = (h @ w_query).reshape(B, T, PEER_HEADS, 2, PEER_DKEY // 2)
    s = jnp.einsum('bthpd,hpkd->bthpk', q, sub_keys).astype(jnp.float32)
    s_top, i_top = lax.top_k(s, PEER_TOPK)
    cand_s = (s_top[..., 0, :, None] + s_top[..., 1, None, :]).reshape(B, T, PEER_HEADS, PEER_TOPK * PEER_TOPK)
    cand_i = (i_top[..., 0, :, None] * PEER_NKEYS + i_top[..., 1, None, :]).reshape(B, T, PEER_HEADS, PEER_TOPK * PEER_TOPK)
    best_s, pos = lax.top_k(cand_s, PEER_TOPK)
    idx = jnp.take_along_axis(cand_i, pos, axis=-1)
    gate = jax.nn.softmax(best_s, axis=-1)
    nb = (B * T) // PEER_BLOCK
    hb = h.reshape(nb, PEER_BLOCK, D)
    ib = idx.reshape(nb, PEER_BLOCK, PEER_HEADS * PEER_TOPK)
    gb = gate.reshape(nb, PEER_BLOCK, PEER_HEADS * PEER_TOPK).astype(h.dtype)

    def expert_block(args):
        h_blk, i_blk, g_blk = args
        act = jax.nn.gelu(jnp.einsum('td,ted->te', h_blk, down[i_blk]), approximate=False)
        return jnp.einsum('te,ted->td', act * g_blk, up[i_blk])

    y = lax.map(expert_block, (hb, ib, gb))
    return y.reshape(B, T, D)


def setup_inputs(seed: int = 0) -> dict:
    key = jax.random.key(seed)
    ks = jax.random.split(key, 30)
    f32 = jnp.float32

    def nrm(k, shape, s):
        return jax.random.normal(k, shape, f32) * s

    L = DEPTH
    dt = jnp.exp(jax.random.uniform(ks[10], (L, 2, GDN_HEADS), f32, math.log(1e-3), math.log(1e-1)))
    return {
        'x': nrm(ks[0], (BATCH, SEQ, D_MODEL), 1.0),
        'c': nrm(ks[1], (BATCH, D_MODEL), 1.0),
        'ctx': nrm(ks[2], (BATCH, CTX_LEN, D_MODEL), 1.0),
        'c_ctx': nrm(ks[3], (D_MODEL,), 1.0),
        'ada_w': nrm(ks[4], (L, D_MODEL, N_MOD * D_MODEL), 0.2 * D_MODEL ** -0.5),
        'ada_b': nrm(ks[5], (L, N_MOD * D_MODEL), 0.02),
        'norm1_g': 1.0 + nrm(ks[6], (L, D_MODEL), 0.02),
        'w_in': nrm(ks[7], (L, D_MODEL, IN_COLS), D_MODEL ** -0.5),
        'gdn_conv_w': nrm(ks[8], (L, GDN_CONV, 2 * GDN_QK_W + GDN_V_W), GDN_CONV ** -0.5),
        'gdn_a_log': jnp.log(jax.random.uniform(ks[9], (L, 2, GDN_HEADS), f32, 1.0, 16.0)),
        'gdn_dt_bias': dt + jnp.log(-jnp.expm1(-dt)),
        'gdn_norm_w': 1.0 + nrm(ks[11], (L, GDN_DV), 0.02),
        'rwkv_mu': jax.random.uniform(ks[12], (L, RWKV_COLS), f32, 0.0, 1.0),
        'rwkv_w0': jax.random.uniform(ks[13], (L, 2, RWKV_W), f32, -6.0, 0.0),
        'rwkv_w2': nrm(ks[14], (L, 2, RWKV_DECAY_LORA, RWKV_W), 0.5 * RWKV_DECAY_LORA ** -0.5),
        'rwkv_a0': nrm(ks[15], (L, 2, RWKV_W), 0.1),
        'rwkv_a2': nrm(ks[16], (L, 2, RWKV_ICLR_LORA, RWKV_W), 0.5 * RWKV_ICLR_LORA ** -0.5),
        'rwkv_g2': nrm(ks[17], (L, RWKV_GATE_LORA, RWKV_W), RWKV_GATE_LORA ** -0.5),
        'rwkv_k_k': 0.85 + nrm(ks[18], (L, RWKV_W), 0.02),
        'rwkv_k_a': 1.0 + nrm(ks[19], (L, RWKV_W), 0.02),
        'rwkv_r_k': nrm(ks[20], (L, RWKV_W), 0.1),
        'rwkv_gn_w': 1.0 + nrm(ks[21], (L, RWKV_W), 0.02),
        'rwkv_gn_b': nrm(ks[22], (L, RWKV_W), 0.02),
        'w_out': nrm(ks[23], (L, D_MIX, D_MODEL), D_MIX ** -0.5),
        'norm2_g': 1.0 + nrm(ks[24], (L, D_MODEL), 0.02),
        'peer_w_query': nrm(ks[25], (L, D_MODEL, PEER_HEADS * PEER_DKEY), D_MODEL ** -0.5),
        'peer_sub_keys': nrm(ks[26], (L, PEER_HEADS, 2, PEER_NKEYS, PEER_DKEY // 2), (PEER_DKEY // 2) ** -0.5),
        'peer_down': nrm(ks[27], (L, PEER_EXPERTS, D_MODEL), D_MODEL ** -0.5),
        'peer_up': nrm(ks[28], (L, PEER_EXPERTS, D_MODEL), 1.0),
        'final_norm_g': 1.0 + nrm(ks[29], (D_MODEL,), 0.02),
    }


def reference(x, c, ctx, c_ctx, ada_w, ada_b, norm1_g, w_in, gdn_conv_w, gdn_a_log, gdn_dt_bias,
              gdn_norm_w, rwkv_mu, rwkv_w0, rwkv_w2, rwkv_a0, rwkv_a2, rwkv_g2, rwkv_k_k, rwkv_k_a,
              rwkv_r_k, rwkv_gn_w, rwkv_gn_b, w_out, norm2_g, peer_w_query, peer_sub_keys,
              peer_down, peer_up, final_norm_g):
    n_ctx = ctx.shape[1]
    for layer in range(DEPTH):
        sh1, sc1, gt1, sh2, sc2, gt2 = ada_mod(c, ada_w[layer], ada_b[layer])
        csh1, csc1, cgt1, csh2, csc2, cgt2 = ada_mod(c_ctx[None, :], ada_w[layer], ada_b[layer])
        h = jnp.concatenate([modulate(rms_norm(ctx, norm1_g[layer]), csh1, csc1),
                             modulate(rms_norm(x, norm1_g[layer]), sh1, sc1)], axis=1)
        p = h @ w_in[layer]
        p_gdn, p_rwkv = split_cols(p, [GDN_COLS, RWKV_COLS])
        o_gdn = gdn_mixer(p_gdn, n_ctx, gdn_conv_w[layer], gdn_a_log[layer], gdn_dt_bias[layer],
                          gdn_norm_w[layer])
        o_rwkv = rwkv7_mixer(p_rwkv, n_ctx, rwkv_mu[layer], rwkv_w0[layer], rwkv_w2[layer],
                             rwkv_a0[layer], rwkv_a2[layer], rwkv_g2[layer], rwkv_k_k[layer],
                             rwkv_k_a[layer], rwkv_r_k[layer], rwkv_gn_w[layer], rwkv_gn_b[layer])
        o = jnp.concatenate([o_gdn, o_rwkv], axis=-1) @ w_out[layer]
        x = x + gt1 * o[:, n_ctx:]
        x = x + gt2 * peer_ffn(modulate(rms_norm(x, norm2_g[layer]), sh2, sc2), peer_w_query[layer],
                               peer_sub_keys[layer], peer_down[layer], peer_up[layer])
        if layer + 1 < DEPTH:
            ctx = ctx + cgt1 * o[:, :n_ctx]
            ctx = ctx + cgt2 * peer_ffn(modulate(rms_norm(ctx, norm2_g[layer]), csh2, csc2),
                                        peer_w_query[layer], peer_sub_keys[layer],
                                        peer_down[layer], peer_up[layer])
    return rms_norm(x, final_norm_g)
```

```python
import functools
import math

import jax
import jax.numpy as jnp
from jax import lax
from jax.experimental import pallas as pl
from jax.experimental.pallas import tpu as pltpu

F32 = jnp.float32
BF16 = jnp.bfloat16

GRID_W = 64
GDN_HEADS = 4
GDN_DK = 128
GDN_CONV = 5
CHUNK = 64
RWKV_HEADS = 8
RWKV_HD = 64
RWKV_W = RWKV_HEADS * RWKV_HD
LORA = 64
PEER_HEADS = 8
PEER_NKEYS = 128
PEER_TOPK = 16
N_MOD = 6

NORM_EPS = 1e-6
L2_EPS = 1e-6
RWKV_GN_EPS = 64e-5

LANES = 128
VMEM_LIMIT = 56 * 1024 * 1024
NEG_INF = float("-inf")


def _cparams(sem):
    return pltpu.CompilerParams(dimension_semantics=sem, vmem_limit_bytes=VMEM_LIMIT)


def _dot(a, b):
    return jnp.dot(a.astype(BF16), b.astype(BF16), preferred_element_type=F32)


def _dot_nt(a, b):
    return lax.dot_general(a.astype(BF16), b.astype(BF16), (((1,), (1,)), ((), ())),
                           preferred_element_type=F32)


def _dot_tn(a, b):
    return lax.dot_general(a.astype(BF16), b.astype(BF16), (((0,), (0,)), ((), ())),
                           preferred_element_type=F32)


def _dot_hi(a, b):
    return jnp.dot(a, b, preferred_element_type=F32, precision=lax.Precision.HIGHEST)


def _sigmoid(x):
    return 1.0 / (1.0 + jnp.exp(-x))


def _silu(x):
    return x * _sigmoid(x)


def _softplus(x):
    return jnp.maximum(x, 0.0) + jnp.log(1.0 + jnp.exp(-jnp.abs(x)))


def _unit_lower_inverse(x):
    c = x.shape[0]
    eye = (lax.broadcasted_iota(jnp.int32, (c, c), 0) ==
           lax.broadcasted_iota(jnp.int32, (c, c), 1)).astype(F32)
    p = eye + x
    xp = x
    for _ in range(int(math.log2(c)) - 1):
        xp = _dot_hi(xp, xp)
        p = p + _dot_hi(p, xp)
    return p


def _order_masks(c, rev):
    r = lax.broadcasted_iota(jnp.int32, (c, c), 0)
    s = lax.broadcasted_iota(jnp.int32, (c, c), 1)
    if rev:
        return s >= r, s > r
    return s <= r, s < r


def _ada_kernel(c_ref, w_ref, b_ref, o_ref):
    o_ref[...] = _dot_hi(_silu(c_ref[...]), w_ref[...]) + b_ref[...]


def ada_mod(cond, w, b):
    n, d = cond.shape
    cols = w.shape[1]
    tn = 1024
    return pl.pallas_call(
        _ada_kernel,
        grid=(cols // tn,),
        in_specs=[pl.BlockSpec((n, d), lambda j: (0, 0)),
                  pl.BlockSpec((d, tn), lambda j: (0, j)),
                  pl.BlockSpec((1, tn), lambda j: (0, j))],
        out_specs=pl.BlockSpec((n, tn), lambda j: (0, j)),
        out_shape=jax.ShapeDtypeStruct((n, cols), F32),
        compiler_params=_cparams(("arbitrary",)),
        name="ada_mod",
    )(cond, w, b.reshape(1, cols))


def _in_proj_kernel(x_ref, g_ref, sh_ref, sc_ref, wg_ref, wr_ref, pg_ref, pr_ref):
    x = x_ref[...]
    y = x * lax.rsqrt(jnp.mean(x * x, axis=-1, keepdims=True) + NORM_EPS) * g_ref[...]
    h = (y * (1.0 + sc_ref[...]) + sh_ref[...]).astype(BF16)
    pg_ref[...] = jnp.dot(h, wg_ref[...], preferred_element_type=F32)
    pr_ref[...] = jnp.dot(h, wr_ref[...], preferred_element_type=F32)


def in_proj(xin, norm_g, shift, scale, w_gdn, w_rwkv, n_ctx, tm):
    bsz, t, d = xin.shape
    cg, cr = w_gdn.shape[1], w_rwkv.shape[1]
    nctx_tiles = n_ctx // tm

    def mod_map(b, i):
        return (b, jnp.where(i < nctx_tiles, 0, 1), 0, 0)

    return pl.pallas_call(
        _in_proj_kernel,
        grid=(bsz, t // tm),
        in_specs=[pl.BlockSpec((None, tm, d), lambda b, i: (b, i, 0)),
                  pl.BlockSpec((1, d), lambda b, i: (0, 0)),
                  pl.BlockSpec((None, None, 1, d), mod_map),
                  pl.BlockSpec((None, None, 1, d), mod_map),
                  pl.BlockSpec((d, cg), lambda b, i: (0, 0)),
                  pl.BlockSpec((d, cr), lambda b, i: (0, 0))],
        out_specs=[pl.BlockSpec((None, tm, cg), lambda b, i: (b, i, 0)),
                   pl.BlockSpec((None, tm, cr), lambda b, i: (b, i, 0))],
        out_shape=[jax.ShapeDtypeStruct((bsz, t, cg), F32),
                   jax.ShapeDtypeStruct((bsz, t, cr), F32)],
        compiler_params=_cparams(("parallel", "parallel")),
        name="in_proj",
    )(xin, norm_g.reshape(1, d), shift, scale, w_gdn, w_rwkv)


def _gdn_prep_kernel(p_ref, w_ref, o_ref, *, n_ctx):
    j = pl.program_id(1)
    x = p_ref[...]
    t_len = x.shape[0]
    w = w_ref[...]
    t = lax.broadcasted_iota(jnp.int32, x.shape, 0)
    is_ctx = t < n_ctx
    lo = jnp.where(is_ctx, 0, n_ctx)
    hi = jnp.where(is_ctx, n_ctx, t_len)
    pad = (GDN_CONV - 1) // 2
    acc = x * w[pad:pad + 1, :]
    for s in range(-pad, pad + 1):
        if s == 0:
            continue
        xs = pltpu.roll(x, (-s) % t_len, 0)
        valid = (t + s >= lo) & (t + s < hi)
        acc = acc + jnp.where(valid, xs, 0.0) * w[s + pad:s + pad + 1, :]
    y = _silu(acc)
    inv = lax.rsqrt(jnp.sum(y * y, axis=-1, keepdims=True) + L2_EPS)
    fac = jnp.where(j < GDN_HEADS, inv * (GDN_DK ** -0.5), jnp.where(j < 2 * GDN_HEADS, inv, 1.0))
    o_ref[...] = y * fac


def gdn_prep(p_gdn, conv_w, n_ctx):
    bsz, t, _ = p_gdn.shape
    ncol = conv_w.shape[1]
    return pl.pallas_call(
        functools.partial(_gdn_prep_kernel, n_ctx=n_ctx),
        grid=(bsz, ncol // LANES),
        in_specs=[pl.BlockSpec((None, t, LANES), lambda b, j: (b, 0, j)),
                  pl.BlockSpec((GDN_CONV, LANES), lambda b, j: (0, j))],
        out_specs=pl.BlockSpec((None, t, LANES), lambda b, j: (b, 0, j)),
        out_shape=jax.ShapeDtypeStruct((bsz, t, ncol), F32),
        compiler_params=_cparams(("parallel", "parallel")),
        name="gdn_prep",
    )(p_gdn, conv_w)


def _gdn_dir(qkv, gates, prm, s_ref, d, rev):
    c = qkv.shape[0]
    nh, dk = GDN_HEADS, GDN_DK
    incl, strict = _order_masks(c, rev)
    a_log, dt_bias = prm[0:1, :], prm[1:2, :]
    g_all = -jnp.exp(a_log) * _softplus(gates + dt_bias)
    beta_all = _sigmoid(gates)
    big_g = _dot_hi(incl.astype(F32), g_all)
    big_g_t = big_g.T
    last = 0 if rev else c - 1
    outs = []
    for h in range(nh):
        col = d * nh + h
        q = qkv[:, h * dk:(h + 1) * dk]
        k = qkv[:, (nh + h) * dk:(nh + h + 1) * dk]
        v = qkv[:, (2 * nh + h) * dk:(2 * nh + h + 1) * dk]
        g_col = big_g[:, col:col + 1]
        g_row = big_g_t[col:col + 1, :]
        g_last = big_g[last:last + 1, col:col + 1]
        beta = beta_all[:, 2 * nh + col:2 * nh + col + 1]
        e_g = jnp.exp(g_col)
        gamma = jnp.where(incl, jnp.exp(jnp.where(incl, g_col - g_row, 0.0)), 0.0)
        kk = _dot_nt(k, k)
        a_mat = jnp.where(strict, beta * kk * gamma, 0.0)
        t_inv = _unit_lower_inverse(-a_mat)
        u = _dot_hi(t_inv, beta * v)
        w = _dot_hi(t_inv, (beta * e_g) * k)
        qk = _dot_nt(q, k) * gamma
        s = s_ref[d, h]
        v_new = u - _dot(w, s)
        outs.append(_dot(q * e_g, s) + _dot(qk, v_new))
        s_ref[d, h] = s * jnp.exp(g_last) + _dot_tn(k * jnp.exp(g_last - g_col), v_new)
    return jnp.concatenate(outs, axis=-1)


def _gdn_chunk_kernel(qf_ref, qb_ref, gf_ref, gb_ref, prm_ref, of_ref, ob_ref, s_ref):
    @pl.when(pl.program_id(1) == 0)
    def _():
        s_ref[...] = jnp.zeros_like(s_ref)

    prm = prm_ref[...]
    of_ref[...] = _gdn_dir(qf_ref[...], gf_ref[...], prm, s_ref, 0, False)
    ob_ref[...] = _gdn_dir(qb_ref[...], gb_ref[...], prm, s_ref, 1, True)


def _rev_chunk_map(n_ctx_chunks, n_chunks):
    def cb(s):
        return jnp.where(s < n_ctx_chunks, n_ctx_chunks - 1 - s, n_chunks + n_ctx_chunks - 1 - s)
    return cb


def gdn_chunk(qkv, p_gdn, prm, n_ctx):
    bsz, t, ncol = qkv.shape
    nv = GDN_HEADS * GDN_DK
    n_chunks = t // CHUNK
    cb = _rev_chunk_map(n_ctx // CHUNK, n_chunks)
    gate_tile = p_gdn.shape[2] // LANES - 1
    return pl.pallas_call(
        _gdn_chunk_kernel,
        grid=(bsz, n_chunks),
        in_specs=[pl.BlockSpec((None, CHUNK, ncol), lambda b, s: (b, s, 0)),
                  pl.BlockSpec((None, CHUNK, ncol), lambda b, s: (b, cb(s), 0)),
                  pl.BlockSpec((None, CHUNK, LANES), lambda b, s: (b, s, gate_tile)),
                  pl.BlockSpec((None, CHUNK, LANES), lambda b, s: (b, cb(s), gate_tile)),
                  pl.BlockSpec((8, LANES), lambda b, s: (0, 0))],
        out_specs=[pl.BlockSpec((None, CHUNK, nv), lambda b, s: (b, s, 0)),
                   pl.BlockSpec((None, CHUNK, nv), lambda b, s: (b, cb(s), 0))],
        out_shape=[jax.ShapeDtypeStruct((bsz, t, nv), F32)] * 2,
        scratch_shapes=[pltpu.VMEM((2, GDN_HEADS, GDN_DK, GDN_DK), F32)],
        compiler_params=_cparams(("parallel", "arbitrary")),
        name="gdn_chunk",
    )(qkv, qkv, p_gdn, p_gdn, prm)


def _rwkv_shift_kernel(p_ref, mu_ref, o_ref, *, n_ctx, n_cols):
    j = pl.program_id(1)
    x = p_ref[...]
    t_len = x.shape[0]
    t = lax.broadcasted_iota(jnp.int32, x.shape, 0)
    ch = lax.broadcasted_iota(jnp.int32, x.shape, 1) + j * LANES
    is_ctx = t < n_ctx
    col = (t - n_ctx) % GRID_W
    prev1 = pltpu.roll(x, 1, 0)
    next1 = pltpu.roll(x, t_len - 1, 0)
    up = pltpu.roll(x, GRID_W, 0)
    down = pltpu.roll(x, t_len - GRID_W, 0)
    quarter = n_cols // 4
    half = n_cols // 2
    ctx_sh = jnp.where(ch < half,
                       jnp.where(t >= 1, prev1, 0.0),
                       jnp.where(t < n_ctx - 1, next1, 0.0))
    lat_sh = jnp.where(ch < quarter, jnp.where(col >= 1, prev1, 0.0),
                       jnp.where(ch < 2 * quarter, jnp.where(col < GRID_W - 1, next1, 0.0),
                                 jnp.where(ch < 3 * quarter,
                                           jnp.where(t >= n_ctx + GRID_W, up, 0.0),
                                           jnp.where(t < t_len - GRID_W, down, 0.0))))
    sh = jnp.where(is_ctx, ctx_sh, lat_sh)
    o_ref[...] = x + mu_ref[...] * (sh - x)


def rwkv_shift(p_rwkv, mu, n_ctx):
    bsz, t, ncol = p_rwkv.shape
    return pl.pallas_call(
        functools.partial(_rwkv_shift_kernel, n_ctx=n_ctx, n_cols=ncol),
        grid=(bsz, ncol // LANES),
        in_specs=[pl.BlockSpec((None, t, LANES), lambda b, j: (b, 0, j)),
                  pl.BlockSpec((1, LANES), lambda b, j: (0, j))],
        out_specs=pl.BlockSpec((None, t, LANES), lambda b, j: (b, 0, j)),
        out_shape=jax.ShapeDtypeStruct((bsz, t, ncol), F32),
        compiler_params=_cparams(("parallel", "parallel")),
        name="rwkv_shift",
    )(p_rwkv, mu.reshape(1, ncol))


def _rwkv_gates(xs, d, w0_ref, w2_ref, a0_ref, a2_ref):
    w_ = RWKV_W
    xw = xs[:, 3 * w_ + d * LORA:3 * w_ + (d + 1) * LORA]
    xa = xs[:, 3 * w_ + 2 * LORA + d * LORA:3 * w_ + 2 * LORA + (d + 1) * LORA]
    w_pre = w0_ref[d:d + 1, :] + _dot(jnp.tanh(xw), w2_ref[d])
    log_w = -math.exp(-0.5) * _sigmoid(w_pre)
    a = _sigmoid(a0_ref[d:d + 1, :] + _dot(xa, a2_ref[d]))
    return log_w, a


def _rwkv_dir(xs, vec_ref, w0_ref, w2_ref, a0_ref, a2_ref, hsum_ref, p_ref, d, rev):
    c = xs.shape[0]
    w_ = RWKV_W
    n = RWKV_HD
    incl, strict = _order_masks(c, rev)
    r = xs[:, 0:w_]
    k = xs[:, w_:2 * w_]
    v = xs[:, 2 * w_:3 * w_]
    log_w, a = _rwkv_gates(xs, d, w0_ref, w2_ref, a0_ref, a2_ref)
    k_k, k_a = vec_ref[0:1, :], vec_ref[1:2, :]
    kk = k * k_k
    kk = kk * lax.rsqrt(_dot_hi(kk * kk, hsum_ref[...]) + L2_EPS)
    k_dir = k * (1.0 + (a - 1.0) * k_a)
    alpha = -(kk * a)
    big_g = _dot_hi(incl.astype(F32), log_w)
    last = 0 if rev else c - 1
    g_last = big_g[last:last + 1, :]
    e_pos = jnp.exp(big_g)
    e_neg = jnp.exp(-big_g)
    e_tail = jnp.exp(g_last - big_g)
    r_t = r * e_pos
    b_t = kk * jnp.exp(big_g - log_w)
    a_t = alpha * e_neg
    k_t = k_dir * e_neg
    a_h = alpha * e_tail
    k_h = k_dir * e_tail
    w_last = jnp.exp(g_last)
    outs = []
    for h in range(RWKV_HEADS):
        sl = slice(h * n, (h + 1) * n)
        p0 = p_ref[d, h]
        a_ba = jnp.where(strict, _dot_nt(b_t[:, sl], a_t[:, sl]), 0.0)
        a_bk = jnp.where(strict, _dot_nt(b_t[:, sl], k_t[:, sl]), 0.0)
        a_ra = jnp.where(incl, _dot_nt(r_t[:, sl], a_t[:, sl]), 0.0)
        a_rk = jnp.where(incl, _dot_nt(r_t[:, sl], k_t[:, sl]), 0.0)
        vh = v[:, sl]
        t_inv = _unit_lower_inverse(a_ba)
        u = _dot_hi(t_inv, _dot_nt(b_t[:, sl], p0) + _dot(a_bk, vh))
        outs.append(_dot_nt(r_t[:, sl], p0) + _dot(a_ra, u) + _dot(a_rk, vh))
        p_ref[d, h] = (p0 * w_last[:, sl] + _dot_tn(u, a_h[:, sl]) + _dot_tn(vh, k_h[:, sl]))
    return jnp.concatenate(outs, axis=-1)


def _rwkv_chunk_kernel(xf_ref, xb_ref, vec_ref, w0_ref, w2_ref, a0_ref, a2_ref, hsum_ref,
                       yf_ref, yb_ref, p_ref):
    @pl.when(pl.program_id(1) == 0)
    def _():
        p_ref[...] = jnp.zeros_like(p_ref)

    args = (vec_ref, w0_ref, w2_ref, a0_ref, a2_ref, hsum_ref, p_ref)
    yf_ref[...] = _rwkv_dir(xf_ref[...], *args, 0, False)
    yb_ref[...] = _rwkv_dir(xb_ref[...], *args, 1, True)


def _head_sum_matrix(width, group):
    i = jnp.arange(width)
    return (i[:, None] // group == i[None, :] // group).astype(F32)


def rwkv_chunk(xs, vecs, w0, w2, a0, a2, n_ctx):
    bsz, t, ncol = xs.shape
    n_chunks = t // CHUNK
    cb = _rev_chunk_map(n_ctx // CHUNK, n_chunks)
    hsum = _head_sum_matrix(RWKV_W, RWKV_HD)
    full = lambda *shape: pl.BlockSpec(shape, lambda b, s: (0,) * len(shape))
    return pl.pallas_call(
        _rwkv_chunk_kernel,
        grid=(bsz, n_chunks),
        in_specs=[pl.BlockSpec((None, CHUNK, ncol), lambda b, s: (b, s, 0)),
                  pl.BlockSpec((None, CHUNK, ncol), lambda b, s: (b, cb(s), 0)),
                  full(8, RWKV_W), full(2, RWKV_W), full(2, LORA, RWKV_W),
                  full(2, RWKV_W), full(2, LORA, RWKV_W), full(RWKV_W, RWKV_W)],
        out_specs=[pl.BlockSpec((None, CHUNK, RWKV_W), lambda b, s: (b, s, 0)),
                   pl.BlockSpec((None, CHUNK, RWKV_W), lambda b, s: (b, cb(s), 0))],
        out_shape=[jax.ShapeDtypeStruct((bsz, t, RWKV_W), F32)] * 2,
        scratch_shapes=[pltpu.VMEM((2, RWKV_HEADS, RWKV_HD, RWKV_HD), F32)],
        compiler_params=_cparams(("parallel", "arbitrary")),
        name="rwkv_chunk",
    )(xs, xs, vecs, w0, w2, a0, a2, hsum)


def _mix_out_kernel(x_ref, gt_ref, of_ref, ob_ref, z_ref, yf_ref, yb_ref, xs_ref,
                    gnw_ref, vec_ref, a0_ref, a2_ref, g2_ref, hsum_ref, wo_ref, o_ref):
    nh, dv = GDN_HEADS, GDN_DK
    w_ = RWKV_W
    o = of_ref[...] + ob_ref[...]
    z = z_ref[...]
    parts = []
    for h in range(nh):
        oh = o[:, h * dv:(h + 1) * dv]
        oh = oh * lax.rsqrt(jnp.mean(oh * oh, axis=-1, keepdims=True) + NORM_EPS)
        parts.append(oh * gnw_ref[...] * _silu(z[:, h * dv:(h + 1) * dv]))
    xs = xs_ref[...]
    r = xs[:, 0:w_]
    k = xs[:, w_:2 * w_]
    v = xs[:, 2 * w_:3 * w_]
    xg = xs[:, 3 * w_ + 4 * LORA:]
    k_a, r_k = vec_ref[1:2, :], vec_ref[2:3, :]
    gn_w, gn_b = vec_ref[3:4, :], vec_ref[4:5, :]
    hsum = hsum_ref[...]
    y = yf_ref[...] + yb_ref[...]
    mean = _dot_hi(y, hsum) * (1.0 / RWKV_HD)
    yc = y - mean
    var = _dot_hi(yc * yc, hsum) * (1.0 / RWKV_HD)
    yn = yc * lax.rsqrt(var + RWKV_GN_EPS) * gn_w + gn_b
    rk = jnp.zeros_like(r)
    for d in range(2):
        xa = xs[:, 3 * w_ + 2 * LORA + d * LORA:3 * w_ + 2 * LORA + (d + 1) * LORA]
        a = _sigmoid(a0_ref[d:d + 1, :] + _dot(xa, a2_ref[d]))
        rk = rk + r * (k * (1.0 + (a - 1.0) * k_a)) * r_k
    bonus = _dot_hi(rk, hsum) * v
    gate = _dot(_sigmoid(xg), g2_ref[...])
    parts.append((yn + bonus) * gate)
    mixed = jnp.concatenate(parts, axis=-1).astype(BF16)
    o_ref[...] = x_ref[...] + gt_ref[...] * jnp.dot(mixed, wo_ref[...], preferred_element_type=F32)


def mix_out(x, gt1, o_f, o_b, p_gdn, y_f, y_b, xs, gnw, vecs, a0, a2, g2, w_out, n_ctx, tm):
    bsz, seq, d = x.shape
    off = n_ctx // tm
    nv = GDN_HEADS * GDN_DK
    ncol = xs.shape[2]
    hsum = _head_sum_matrix(RWKV_W, RWKV_HD)
    lat = lambda width, cblk=0: pl.BlockSpec((None, tm, width), lambda b, i: (b, i + off, cblk))
    full = lambda *shape: pl.BlockSpec(shape, lambda b, i: (0,) * len(shape))
    return pl.pallas_call(
        _mix_out_kernel,
        grid=(bsz, seq // tm),
        in_specs=[pl.BlockSpec((None, tm, d), lambda b, i: (b, i, 0)),
                  pl.BlockSpec((None, 1, d), lambda b, i: (b, 0, 0)),
                  lat(nv), lat(nv), lat(nv, 3), lat(RWKV_W), lat(RWKV_W), lat(ncol),
                  full(1, GDN_DK), full(8, RWKV_W), full(2, RWKV_W), full(2, LORA, RWKV_W),
                  full(2 * LORA, RWKV_W), full(RWKV_W, RWKV_W), full(nv + RWKV_W, d)],
        out_specs=pl.BlockSpec((None, tm, d), lambda b, i: (b, i, 0)),
        out_shape=jax.ShapeDtypeStruct((bsz, seq, d), F32),
        compiler_params=_cparams(("parallel", "parallel")),
        name="mix_out",
    )(x, gt1, o_f, o_b, p_gdn, y_f, y_b, xs, gnw, vecs, a0, a2, g2, hsum, w_out)


def _extract_top(s, riota, n_rows, vals_ref, idx_ref, k):
    for i in range(k):
        m = jnp.max(s, axis=0, keepdims=True)
        am = jnp.min(jnp.where(s == m, riota, float(n_rows)), axis=0, keepdims=True)
        vals_ref[i:i + 1, :] = m
        idx_ref[i:i + 1, :] = am
        s = jnp.where(riota == am, NEG_INF, s)


def _peer_route_kernel(x_ref, g_ref, sh_ref, sc_ref, wq_ref, keys_ref,
                       h2_ref, idx_ref, gate_ref,
                       q_scr, tv_scr, ti_scr, cs_scr, ci_scr, bv_scr, bi_scr):
    h = pl.program_id(1)
    dq = 2 * LANES
    kk = PEER_TOPK

    @pl.when(h == 0)
    def _():
        x = x_ref[...]
        y = x * lax.rsqrt(jnp.mean(x * x, axis=-1, keepdims=True) + NORM_EPS) * g_ref[...]
        h2 = y * (1.0 + sc_ref[...]) + sh_ref[...]
        h2_ref[...] = h2
        hb = h2.astype(BF16)
        for hh in range(PEER_HEADS):
            q_scr[hh] = jnp.dot(hb, wq_ref[:, hh * dq:(hh + 1) * dq], preferred_element_type=F32)

    q = q_scr[h]
    tm = q.shape[0]
    riota = lax.broadcasted_iota(jnp.int32, (PEER_NKEYS, tm), 0).astype(F32)
    for p in range(2):
        s_t = _dot_nt(keys_ref[p], q[:, p * LANES:(p + 1) * LANES])
        _extract_top(s_t, riota, PEER_NKEYS, tv_scr.at[p], ti_scr.at[p], kk)
    s1 = tv_scr[1]
    i1 = ti_scr[1]
    for i in range(kk):
        cs_scr[i * kk:(i + 1) * kk, :] = tv_scr[0, i:i + 1, :] + s1
        ci_scr[i * kk:(i + 1) * kk, :] = ti_scr[0, i:i + 1, :] * float(PEER_NKEYS) + i1
    ciota = lax.broadcasted_iota(jnp.int32, (kk * kk, tm), 0).astype(F32)
    cand = cs_scr[...]
    cand_i = ci_scr[...]
    for i in range(kk):
        m = jnp.max(cand, axis=0, keepdims=True)
        pos = jnp.min(jnp.where(cand == m, ciota, float(kk * kk)), axis=0, keepdims=True)
        hit = ciota == pos
        bv_scr[i:i + 1, :] = m
        bi_scr[i:i + 1, :] = jnp.sum(jnp.where(hit, cand_i, 0.0), axis=0, keepdims=True)
        cand = jnp.where(hit, NEG_INF, cand)
    best = bv_scr[...]
    e = jnp.exp(best - best[0:1, :])
    gate_ref[...] = e / jnp.sum(e, axis=0, keepdims=True)
    idx_ref[...] = bi_scr[...].astype(jnp.int32)


def peer_route(x1, norm_g, sh2, sc2, wq, sub_keys, tm):
    bsz, seq, d = x1.shape
    ntok = bsz * seq
    tiles_per_b = seq // tm
    kk = PEER_TOPK
    nq = wq.shape[1]
    return pl.pallas_call(
        _peer_route_kernel,
        grid=(ntok // tm, PEER_HEADS),
        in_specs=[pl.BlockSpec((tm, d), lambda i, h: (i, 0)),
                  pl.BlockSpec((1, d), lambda i, h: (0, 0)),
                  pl.BlockSpec((None, 1, d), lambda i, h: (i // tiles_per_b, 0, 0)),
                  pl.BlockSpec((None, 1, d), lambda i, h: (i // tiles_per_b, 0, 0)),
                  pl.BlockSpec((d, nq), lambda i, h: (0, 0)),
                  pl.BlockSpec((None, 2, PEER_NKEYS, LANES), lambda i, h: (h, 0, 0, 0))],
        out_specs=[pl.BlockSpec((tm, d), lambda i, h: (i, 0)),
                   pl.BlockSpec((kk, tm), lambda i, h: (h, i)),
                   pl.BlockSpec((kk, tm), lambda i, h: (h, i))],
        out_shape=[jax.ShapeDtypeStruct((ntok, d), F32),
                   jax.ShapeDtypeStruct((PEER_HEADS * kk, ntok), jnp.int32),
                   jax.ShapeDtypeStruct((PEER_HEADS * kk, ntok), F32)],
        scratch_shapes=[pltpu.VMEM((PEER_HEADS, tm, 2 * LANES), F32),
                        pltpu.VMEM((2, kk, tm), F32), pltpu.VMEM((2, kk, tm), F32),
                        pltpu.VMEM((kk * kk, tm), F32), pltpu.VMEM((kk * kk, tm), F32),
                        pltpu.VMEM((kk, tm), F32), pltpu.VMEM((kk, tm), F32)],
        compiler_params=_cparams(("parallel", "arbitrary")),
        name="peer_route",
    )(x1.reshape(ntok, d), norm_g.reshape(1, d), sh2, sc2, wq, sub_keys)


def _peer_expert_kernel(idx_ref, h2_ref, gate_ref, x1_ref, gt_ref, fg_ref, tbl_ref,
                        o_ref, buf, sem, *, tt, d):
    ne = PEER_HEADS * PEER_TOPK

    def gather(t, slot):
        def issue(kb, carry):
            for u in range(8):
                k = kb * 8 + u
                e = idx_ref[t * ne + k]
                pltpu.make_async_copy(tbl_ref.at[pl.ds(e, 1), :],
                                      buf.at[slot, pl.ds(k, 1), :], sem.at[slot]).start()
            return carry
        lax.fori_loop(0, ne // 8, issue, 0)

    def wait(slot):
        pltpu.make_async_copy(tbl_ref.at[pl.ds(0, ne), :], buf.at[slot], sem.at[slot]).wait()

    gather(0, 0)
    eye = (lax.broadcasted_iota(jnp.int32, (ne, ne), 0) ==
           lax.broadcasted_iota(jnp.int32, (ne, ne), 1))

    def body(t, carry):
        slot = t % 2

        @pl.when(t + 1 < tt)
        def _():
            gather(t + 1, 1 - slot)

        wait(slot)
        rows = buf[slot]
        hrow = h2_ref[pl.ds(t, 1), :]
        pre = jnp.sum(rows[:, :d] * hrow, axis=1, keepdims=True)
        act = 0.5 * pre * (1.0 + lax.erf(pre * (2.0 ** -0.5)))
        gcol = jnp.sum(jnp.where(eye, gate_ref[pl.ds(t, 1), :], 0.0), axis=1, keepdims=True)
        y = jnp.sum((act * gcol) * rows[:, d:], axis=0, keepdims=True)
        xo = x1_ref[pl.ds(t, 1), :] + gt_ref[...] * y
        xo = xo * lax.rsqrt(jnp.mean(xo * xo, axis=-1, keepdims=True) + NORM_EPS) * fg_ref[...]
        o_ref[pl.ds(t, 1), :] = xo
        return carry

    lax.fori_loop(0, tt, body, 0)


def peer_expert(idx_flat, h2, gate_tok, x1, gt2, final_g, table, seq, tt):
    ntok, d = h2.shape
    ne = PEER_HEADS * PEER_TOPK
    tiles_per_b = seq // tt
    return pl.pallas_call(
        functools.partial(_peer_expert_kernel, tt=tt, d=d),
        grid=(ntok // tt,),
        in_specs=[pl.BlockSpec((tt * ne,), lambda i: (i,), memory_space=pltpu.SMEM),
                  pl.BlockSpec((tt, d), lambda i: (i, 0)),
                  pl.BlockSpec((tt, ne), lambda i: (i, 0)),
                  pl.BlockSpec((tt, d), lambda i: (i, 0)),
                  pl.BlockSpec((None, 1, d), lambda i: (i // tiles_per_b, 0, 0)),
                  pl.BlockSpec((1, d), lambda i: (0, 0)),
                  pl.BlockSpec(memory_space=pl.ANY)],
        out_specs=pl.BlockSpec((tt, d), lambda i: (i, 0)),
        out_shape=jax.ShapeDtypeStruct((ntok, d), F32),
        scratch_shapes=[pltpu.VMEM((2, ne, 2 * d), F32), pltpu.SemaphoreType.DMA((2,))],
        compiler_params=_cparams(("arbitrary",)),
        name="peer_expert",
    )(idx_flat, h2, gate_tok, x1, gt2, final_g.reshape(1, d), table)


def kernel(x, c, ctx, c_ctx, ada_w, ada_b, norm1_g, w_in, gdn_conv_w, gdn_a_log, gdn_dt_bias,
           gdn_norm_w, rwkv_mu, rwkv_w0, rwkv_w2, rwkv_a0, rwkv_a2, rwkv_g2, rwkv_k_k, rwkv_k_a,
           rwkv_r_k, rwkv_gn_w, rwkv_gn_b, w_out, norm2_g, peer_w_query, peer_sub_keys,
           peer_down, peer_up, final_norm_g):
    bsz, seq, d = x.shape
    n_ctx = ctx.shape[1]
    assert ada_w.shape[0] == 1, "single-layer block"
    assert n_ctx % 256 == 0 and seq % 256 == 0 and seq % GRID_W == 0
    nh = GDN_HEADS
    n_qkvz = 4 * nh * GDN_DK
    gdn_cols = n_qkvz + 4 * nh

    cond = jnp.concatenate([c, c_ctx[None, :], jnp.zeros((16 - bsz - 1, d), F32)], axis=0)
    mods = ada_mod(cond, ada_w[0], ada_b[0])
    m_lat = mods[:bsz].reshape(bsz, N_MOD, 1, d)
    m_ctx = jnp.broadcast_to(mods[bsz].reshape(1, N_MOD, 1, d), (bsz, N_MOD, 1, d))
    shift1 = jnp.stack([m_ctx[:, 0], m_lat[:, 0]], axis=1)
    scale1 = jnp.stack([m_ctx[:, 1], m_lat[:, 1]], axis=1)
    gt1, sh2, sc2, gt2 = m_lat[:, 2], m_lat[:, 3], m_lat[:, 4], m_lat[:, 5]

    wl = w_in[0]
    w_gdn = jnp.concatenate([wl[:, :gdn_cols], jnp.zeros((d, LANES - 4 * nh), F32)], axis=1)
    w_rwkv = wl[:, gdn_cols:]
    xin = jnp.concatenate([ctx, x], axis=1)
    p_gdn, p_rwkv = in_proj(xin, norm1_g[0], shift1, scale1, w_gdn.astype(BF16),
                            w_rwkv.astype(BF16), n_ctx, 256)

    qkv = gdn_prep(p_gdn, gdn_conv_w[0], n_ctx)
    prm = jnp.zeros((8, LANES), F32)
    prm = prm.at[0, :2 * nh].set(gdn_a_log[0].reshape(-1))
    prm = prm.at[1, :2 * nh].set(gdn_dt_bias[0].reshape(-1))
    o_f, o_b = gdn_chunk(qkv, p_gdn, prm, n_ctx)

    xs = rwkv_shift(p_rwkv, rwkv_mu[0], n_ctx)
    vecs = jnp.zeros((8, RWKV_W), F32)
    for i, vec in enumerate((rwkv_k_k, rwkv_k_a, rwkv_r_k, rwkv_gn_w, rwkv_gn_b)):
        vecs = vecs.at[i].set(vec[0])
    y_f, y_b = rwkv_chunk(xs, vecs, rwkv_w0[0], rwkv_w2[0], rwkv_a0[0], rwkv_a2[0], n_ctx)

    x1 = mix_out(x, gt1, o_f, o_b, p_gdn, y_f, y_b, xs, gdn_norm_w[0].reshape(1, GDN_DK), vecs,
                 rwkv_a0[0], rwkv_a2[0], rwkv_g2[0], w_out[0].astype(BF16), n_ctx, 256)

    h2, idx_t, gate_t = peer_route(x1, norm2_g[0], sh2, sc2, peer_w_query[0].astype(BF16),
                                   peer_sub_keys[0], 256)
    table = jnp.concatenate([peer_down[0], peer_up[0]], axis=1)
    out = peer_expert(idx_t.T.reshape(-1), h2, gate_t.T, x1.reshape(bsz * seq, d), gt2,
                      final_norm_g, table, seq, 128)
    return out.reshape(bsz, seq, d)
```

```python
import functools
import math

import jax
import jax.numpy as jnp
from jax import lax
from jax.experimental import pallas as pl
from jax.experimental.pallas import tpu as pltpu

F32 = jnp.float32
BF16 = jnp.bfloat16

GRID_W = 64
GDN_HEADS = 4
GDN_DK = 128
GDN_CONV = 5
CHUNK = 64
RWKV_HEADS = 8
RWKV_HD = 64
RWKV_W = RWKV_HEADS * RWKV_HD
LORA = 64
PEER_HEADS = 8
PEER_NKEYS = 128
PEER_TOPK = 16
N_MOD = 6

NORM_EPS = 1e-6
L2_EPS = 1e-6
RWKV_GN_EPS = 64e-5

LANES = 128
VMEM_LIMIT = 56 * 1024 * 1024
NEG_INF = float("-inf")


def _cparams(sem):
    return pltpu.CompilerParams(dimension_semantics=sem, vmem_limit_bytes=VMEM_LIMIT)


def _dot(a, b):
    return jnp.dot(a.astype(BF16), b.astype(BF16), preferred_element_type=F32)


def _dot_nt(a, b):
    return lax.dot_general(a.astype(BF16), b.astype(BF16), (((1,), (1,)), ((), ())),
                           preferred_element_type=F32)


def _dot_tn(a, b):
    return lax.dot_general(a.astype(BF16), b.astype(BF16), (((0,), (0,)), ((), ())),
                           preferred_element_type=F32)


def _dot_hi(a, b):
    return jnp.dot(a, b, preferred_element_type=F32, precision=lax.Precision.HIGHEST)


def _sigmoid(x):
    return 1.0 / (1.0 + jnp.exp(-x))


def _silu(x):
    return x * _sigmoid(x)


def _softplus(x):
    return jnp.maximum(x, 0.0) + jnp.log(1.0 + jnp.exp(-jnp.abs(x)))


def _unit_lower_inverse(x):
    c = x.shape[0]
    eye = (lax.broadcasted_iota(jnp.int32, (c, c), 0) ==
           lax.broadcasted_iota(jnp.int32, (c, c), 1)).astype(F32)
    p = eye + x
    xp = x
    for _ in range(int(math.log2(c)) - 1):
        xp = _dot(xp, xp)
        p = p + _dot(p, xp)
    return p


def _order_masks(c, rev):
    r = lax.broadcasted_iota(jnp.int32, (c, c), 0)
    s = lax.broadcasted_iota(jnp.int32, (c, c), 1)
    if rev:
        return s >= r, s > r
    return s <= r, s < r


def _ada_kernel(c_ref, w_ref, b_ref, o_ref):
    o_ref[...] = _dot_hi(_silu(c_ref[...]), w_ref[...]) + b_ref[...]


def ada_mod(cond, w, b):
    n, d = cond.shape
    cols = w.shape[1]
    tn = 1024
    return pl.pallas_call(
        _ada_kernel,
        grid=(cols // tn,),
        in_specs=[pl.BlockSpec((n, d), lambda j: (0, 0)),
                  pl.BlockSpec((d, tn), lambda j: (0, j)),
                  pl.BlockSpec((1, tn), lambda j: (0, j))],
        out_specs=pl.BlockSpec((n, tn), lambda j: (0, j)),
        out_shape=jax.ShapeDtypeStruct((n, cols), F32),
        compiler_params=_cparams(("arbitrary",)),
        name="ada_mod",
    )(cond, w, b.reshape(1, cols))


def _in_proj_kernel(x_ref, g_ref, sh_ref, sc_ref, wg_ref, wr_ref, pg_ref, pr_ref):
    x = x_ref[...]
    y = x * lax.rsqrt(jnp.mean(x * x, axis=-1, keepdims=True) + NORM_EPS) * g_ref[...]
    h = (y * (1.0 + sc_ref[...]) + sh_ref[...]).astype(BF16)
    pg_ref[...] = jnp.dot(h, wg_ref[...], preferred_element_type=F32)
    pr_ref[...] = jnp.dot(h, wr_ref[...], preferred_element_type=F32)


def in_proj(xin, norm_g, shift, scale, w_gdn, w_rwkv, n_ctx, tm):
    bsz, t, d = xin.shape
    cg, cr = w_gdn.shape[1], w_rwkv.shape[1]
    nctx_tiles = n_ctx // tm

    def mod_map(b, i):
        return (b, jnp.where(i < nctx_tiles, 0, 1), 0, 0)

    return pl.pallas_call(
        _in_proj_kernel,
        grid=(bsz, t // tm),
        in_specs=[pl.BlockSpec((None, tm, d), lambda b, i: (b, i, 0)),
                  pl.BlockSpec((1, d), lambda b, i: (0, 0)),
                  pl.BlockSpec((None, None, 1, d), mod_map),
                  pl.BlockSpec((None, None, 1, d), mod_map),
                  pl.BlockSpec((d, cg), lambda b, i: (0, 0)),
                  pl.BlockSpec((d, cr), lambda b, i: (0, 0))],
        out_specs=[pl.BlockSpec((None, tm, cg), lambda b, i: (b, i, 0)),
                   pl.BlockSpec((None, tm, cr), lambda b, i: (b, i, 0))],
        out_shape=[jax.ShapeDtypeStruct((bsz, t, cg), F32),
                   jax.ShapeDtypeStruct((bsz, t, cr), F32)],
        compiler_params=_cparams(("parallel", "parallel")),
        name="in_proj",
    )(xin, norm_g.reshape(1, d), shift, scale, w_gdn, w_rwkv)


def _gdn_prep_kernel(p_ref, w_ref, o_ref, *, n_ctx):
    j = pl.program_id(1)
    x = p_ref[...]
    t_len = x.shape[0]
    w = w_ref[...]
    t = lax.broadcasted_iota(jnp.int32, x.shape, 0)
    is_ctx = t < n_ctx
    lo = jnp.where(is_ctx, 0, n_ctx)
    hi = jnp.where(is_ctx, n_ctx, t_len)
    pad = (GDN_CONV - 1) // 2
    acc = x * w[pad:pad + 1, :]
    for s in range(-pad, pad + 1):
        if s == 0:
            continue
        xs = pltpu.roll(x, (-s) % t_len, 0)
        valid = (t + s >= lo) & (t + s < hi)
        acc = acc + jnp.where(valid, xs, 0.0) * w[s + pad:s + pad + 1, :]
    y = _silu(acc)
    inv = lax.rsqrt(jnp.sum(y * y, axis=-1, keepdims=True) + L2_EPS)
    fac = jnp.where(j < GDN_HEADS, inv * (GDN_DK ** -0.5), jnp.where(j < 2 * GDN_HEADS, inv, 1.0))
    o_ref[...] = y * fac


def gdn_prep(p_gdn, conv_w, n_ctx):
    bsz, t, _ = p_gdn.shape
    ncol = conv_w.shape[1]
    return pl.pallas_call(
        functools.partial(_gdn_prep_kernel, n_ctx=n_ctx),
        grid=(bsz, ncol // LANES),
        in_specs=[pl.BlockSpec((None, t, LANES), lambda b, j: (b, 0, j)),
                  pl.BlockSpec((GDN_CONV, LANES), lambda b, j: (0, j))],
        out_specs=pl.BlockSpec((None, t, LANES), lambda b, j: (b, 0, j)),
        out_shape=jax.ShapeDtypeStruct((bsz, t, ncol), F32),
        compiler_params=_cparams(("parallel", "parallel")),
        name="gdn_prep",
    )(p_gdn, conv_w)


def _gdn_dir(qkv, gates, prm, s_ref, d, rev):
    c = qkv.shape[0]
    nh, dk = GDN_HEADS, GDN_DK
    incl, strict = _order_masks(c, rev)
    a_log, dt_bias = prm[0:1, :], prm[1:2, :]
    g_all = -jnp.exp(a_log) * _softplus(gates + dt_bias)
    beta_all = _sigmoid(gates)
    big_g = _dot_hi(incl.astype(F32), g_all)
    big_g_t = big_g.T
    last = 0 if rev else c - 1
    outs = []
    for h in range(nh):
        col = d * nh + h
        q = qkv[:, h * dk:(h + 1) * dk]
        k = qkv[:, (nh + h) * dk:(nh + h + 1) * dk]
        v = qkv[:, (2 * nh + h) * dk:(2 * nh + h + 1) * dk]
        g_col = big_g[:, col:col + 1]
        g_row = big_g_t[col:col + 1, :]
        g_last = big_g[last:last + 1, col:col + 1]
        beta = beta_all[:, 2 * nh + col:2 * nh + col + 1]
        e_g = jnp.exp(g_col)
        gamma = jnp.where(incl, jnp.exp(jnp.where(incl, g_col - g_row, 0.0)), 0.0)
        kk = _dot_nt(k, k)
        a_mat = jnp.where(strict, beta * kk * gamma, 0.0)
        t_inv = _unit_lower_inverse(-a_mat)
        u = _dot(t_inv, beta * v)
        w = _dot(t_inv, (beta * e_g) * k)
        qk = _dot_nt(q, k) * gamma
        s = s_ref[d, h]
        v_new = u - _dot(w, s)
        outs.append(_dot(q * e_g, s) + _dot(qk, v_new))
        s_ref[d, h] = s * jnp.exp(g_last) + _dot_tn(k * jnp.exp(g_last - g_col), v_new)
    return jnp.concatenate(outs, axis=-1)


def _gdn_chunk_kernel(qf_ref, qb_ref, gf_ref, gb_ref, prm_ref, of_ref, ob_ref, s_ref):
    @pl.when(pl.program_id(1) == 0)
    def _():
        s_ref[...] = jnp.zeros_like(s_ref)

    prm = prm_ref[...]
    of_ref[...] = _gdn_dir(qf_ref[...], gf_ref[...], prm, s_ref, 0, False)
    ob_ref[...] = _gdn_dir(qb_ref[...], gb_ref[...], prm, s_ref, 1, True)


def _rev_chunk_map(n_ctx_chunks, n_chunks):
    def cb(s):
        return jnp.where(s < n_ctx_chunks, n_ctx_chunks - 1 - s, n_chunks + n_ctx_chunks - 1 - s)
    return cb


def gdn_chunk(qkv, p_gdn, prm, n_ctx):
    bsz, t, ncol = qkv.shape
    nv = GDN_HEADS * GDN_DK
    n_chunks = t // CHUNK
    cb = _rev_chunk_map(n_ctx // CHUNK, n_chunks)
    gate_tile = p_gdn.shape[2] // LANES - 1
    return pl.pallas_call(
        _gdn_chunk_kernel,
        grid=(bsz, n_chunks),
        in_specs=[pl.BlockSpec((None, CHUNK, ncol), lambda b, s: (b, s, 0)),
                  pl.BlockSpec((None, CHUNK, ncol), lambda b, s: (b, cb(s), 0)),
                  pl.BlockSpec((None, CHUNK, LANES), lambda b, s: (b, s, gate_tile)),
                  pl.BlockSpec((None, CHUNK, LANES), lambda b, s: (b, cb(s), gate_tile)),
                  pl.BlockSpec((8, LANES), lambda b, s: (0, 0))],
        out_specs=[pl.BlockSpec((None, CHUNK, nv), lambda b, s: (b, s, 0)),
                   pl.BlockSpec((None, CHUNK, nv), lambda b, s: (b, cb(s), 0))],
        out_shape=[jax.ShapeDtypeStruct((bsz, t, nv), F32)] * 2,
        scratch_shapes=[pltpu.VMEM((2, GDN_HEADS, GDN_DK, GDN_DK), F32)],
        compiler_params=_cparams(("parallel", "arbitrary")),
        name="gdn_chunk",
    )(qkv, qkv, p_gdn, p_gdn, prm)


def _rwkv_shift_kernel(p_ref, mu_ref, o_ref, *, n_ctx, n_cols):
    j = pl.program_id(1)
    x = p_ref[...]
    t_len = x.shape[0]
    t = lax.broadcasted_iota(jnp.int32, x.shape, 0)
    ch = lax.broadcasted_iota(jnp.int32, x.shape, 1) + j * LANES
    is_ctx = t < n_ctx
    col = (t - n_ctx) % GRID_W
    prev1 = pltpu.roll(x, 1, 0)
    next1 = pltpu.roll(x, t_len - 1, 0)
    up = pltpu.roll(x, GRID_W, 0)
    down = pltpu.roll(x, t_len - GRID_W, 0)
    quarter = n_cols // 4
    half = n_cols // 2
    ctx_sh = jnp.where(ch < half,
                       jnp.where(t >= 1, prev1, 0.0),
                       jnp.where(t < n_ctx - 1, next1, 0.0))
    lat_sh = jnp.where(ch < quarter, jnp.where(col >= 1, prev1, 0.0),
                       jnp.where(ch < 2 * quarter, jnp.where(col < GRID_W - 1, next1, 0.0),
                                 jnp.where(ch < 3 * quarter,
                                           jnp.where(t >= n_ctx + GRID_W, up, 0.0),
                                           jnp.where(t < t_len - GRID_W, down, 0.0))))
    sh = jnp.where(is_ctx, ctx_sh, lat_sh)
    o_ref[...] = x + mu_ref[...] * (sh - x)


def rwkv_shift(p_rwkv, mu, n_ctx):
    bsz, t, ncol = p_rwkv.shape
    return pl.pallas_call(
        functools.partial(_rwkv_shift_kernel, n_ctx=n_ctx, n_cols=ncol),
        grid=(bsz, ncol // LANES),
        in_specs=[pl.BlockSpec((None, t, LANES), lambda b, j: (b, 0, j)),
                  pl.BlockSpec((1, LANES), lambda b, j: (0, j))],
        out_specs=pl.BlockSpec((None, t, LANES), lambda b, j: (b, 0, j)),
        out_shape=jax.ShapeDtypeStruct((bsz, t, ncol), F32),
        compiler_params=_cparams(("parallel", "parallel")),
        name="rwkv_shift",
    )(p_rwkv, mu.reshape(1, ncol))


def _rwkv_gates(xs, d, w0_ref, w2_ref, a0_ref, a2_ref):
    w_ = RWKV_W
    xw = xs[:, 3 * w_ + d * LORA:3 * w_ + (d + 1) * LORA]
    xa = xs[:, 3 * w_ + 2 * LORA + d * LORA:3 * w_ + 2 * LORA + (d + 1) * LORA]
    w_pre = w0_ref[d:d + 1, :] + _dot(jnp.tanh(xw), w2_ref[d])
    log_w = -math.exp(-0.5) * _sigmoid(w_pre)
    a = _sigmoid(a0_ref[d:d + 1, :] + _dot(xa, a2_ref[d]))
    return log_w, a


def _rwkv_dir(xs, vec_ref, w0_ref, w2_ref, a0_ref, a2_ref, hsum_ref, p_ref, d, rev):
    c = xs.shape[0]
    w_ = RWKV_W
    n = RWKV_HD
    incl, strict = _order_masks(c, rev)
    r = xs[:, 0:w_]
    k = xs[:, w_:2 * w_]
    v = xs[:, 2 * w_:3 * w_]
    log_w, a = _rwkv_gates(xs, d, w0_ref, w2_ref, a0_ref, a2_ref)
    k_k, k_a = vec_ref[0:1, :], vec_ref[1:2, :]
    kk = k * k_k
    kk = kk * lax.rsqrt(_dot_hi(kk * kk, hsum_ref[...]) + L2_EPS)
    k_dir = k * (1.0 + (a - 1.0) * k_a)
    alpha = -(kk * a)
    big_g = _dot_hi(incl.astype(F32), log_w)
    last = 0 if rev else c - 1
    g_last = big_g[last:last + 1, :]
    e_pos = jnp.exp(big_g)
    e_neg = jnp.exp(-big_g)
    e_tail = jnp.exp(g_last - big_g)
    r_t = r * e_pos
    b_t = kk * jnp.exp(big_g - log_w)
    a_t = alpha * e_neg
    k_t = k_dir * e_neg
    a_h = alpha * e_tail
    k_h = k_dir * e_tail
    w_last = jnp.exp(g_last)
    outs = []
    for h in range(RWKV_HEADS):
        sl = slice(h * n, (h + 1) * n)
        p0 = p_ref[d, h]
        a_ba = jnp.where(strict, _dot_nt(b_t[:, sl], a_t[:, sl]), 0.0)
        a_bk = jnp.where(strict, _dot_nt(b_t[:, sl], k_t[:, sl]), 0.0)
        a_ra = jnp.where(incl, _dot_nt(r_t[:, sl], a_t[:, sl]), 0.0)
        a_rk = jnp.where(incl, _dot_nt(r_t[:, sl], k_t[:, sl]), 0.0)
        vh = v[:, sl]
        t_inv = _unit_lower_inverse(a_ba)
        u = _dot(t_inv, _dot_nt(b_t[:, sl], p0) + _dot(a_bk, vh))
        outs.append(_dot_nt(r_t[:, sl], p0) + _dot(a_ra, u) + _dot(a_rk, vh))
        p_ref[d, h] = (p0 * w_last[:, sl] + _dot_tn(u, a_h[:, sl]) + _dot_tn(vh, k_h[:, sl]))
    return jnp.concatenate(outs, axis=-1)


def _rwkv_chunk_kernel(xf_ref, xb_ref, vec_ref, w0_ref, w2_ref, a0_ref, a2_ref, hsum_ref,
                       yf_ref, yb_ref, p_ref):
    @pl.when(pl.program_id(1) == 0)
    def _():
        p_ref[...] = jnp.zeros_like(p_ref)

    args = (vec_ref, w0_ref, w2_ref, a0_ref, a2_ref, hsum_ref, p_ref)
    yf_ref[...] = _rwkv_dir(xf_ref[...], *args, 0, False)
    yb_ref[...] = _rwkv_dir(xb_ref[...], *args, 1, True)


def _head_sum_matrix(width, group):
    i = jnp.arange(width)
    return (i[:, None] // group == i[None, :] // group).astype(F32)


def rwkv_chunk(xs, vecs, w0, w2, a0, a2, n_ctx):
    bsz, t, ncol = xs.shape
    n_chunks = t // CHUNK
    cb = _rev_chunk_map(n_ctx // CHUNK, n_chunks)
    hsum = _head_sum_matrix(RWKV_W, RWKV_HD)
    full = lambda *shape: pl.BlockSpec(shape, lambda b, s: (0,) * len(shape))
    return pl.pallas_call(
        _rwkv_chunk_kernel,
        grid=(bsz, n_chunks),
        in_specs=[pl.BlockSpec((None, CHUNK, ncol), lambda b, s: (b, s, 0)),
                  pl.BlockSpec((None, CHUNK, ncol), lambda b, s: (b, cb(s), 0)),
                  full(8, RWKV_W), full(2, RWKV_W), full(2, LORA, RWKV_W),
                  full(2, RWKV_W), full(2, LORA, RWKV_W), full(RWKV_W, RWKV_W)],
        out_specs=[pl.BlockSpec((None, CHUNK, RWKV_W), lambda b, s: (b, s, 0)),
                   pl.BlockSpec((None, CHUNK, RWKV_W), lambda b, s: (b, cb(s), 0))],
        out_shape=[jax.ShapeDtypeStruct((bsz, t, RWKV_W), F32)] * 2,
        scratch_shapes=[pltpu.VMEM((2, RWKV_HEADS, RWKV_HD, RWKV_HD), F32)],
        compiler_params=_cparams(("parallel", "arbitrary")),
        name="rwkv_chunk",
    )(xs, xs, vecs, w0, w2, a0, a2, hsum)


def _mix_out_kernel(x_ref, gt_ref, of_ref, ob_ref, z_ref, yf_ref, yb_ref, xs_ref,
                    gnw_ref, vec_ref, a0_ref, a2_ref, g2_ref, hsum_ref, wo_ref, o_ref):
    nh, dv = GDN_HEADS, GDN_DK
    w_ = RWKV_W
    o = of_ref[...] + ob_ref[...]
    z = z_ref[...]
    parts = []
    for h in range(nh):
        oh = o[:, h * dv:(h + 1) * dv]
        oh = oh * lax.rsqrt(jnp.mean(oh * oh, axis=-1, keepdims=True) + NORM_EPS)
        parts.append(oh * gnw_ref[...] * _silu(z[:, h * dv:(h + 1) * dv]))
    xs = xs_ref[...]
    r = xs[:, 0:w_]
    k = xs[:, w_:2 * w_]
    v = xs[:, 2 * w_:3 * w_]
    xg = xs[:, 3 * w_ + 4 * LORA:]
    k_a, r_k = vec_ref[1:2, :], vec_ref[2:3, :]
    gn_w, gn_b = vec_ref[3:4, :], vec_ref[4:5, :]
    hsum = hsum_ref[...]
    y = yf_ref[...] + yb_ref[...]
    mean = _dot_hi(y, hsum) * (1.0 / RWKV_HD)
    yc = y - mean
    var = _dot_hi(yc * yc, hsum) * (1.0 / RWKV_HD)
    yn = yc * lax.rsqrt(var + RWKV_GN_EPS) * gn_w + gn_b
    rk = jnp.zeros_like(r)
    for d in range(2):
        xa = xs[:, 3 * w_ + 2 * LORA + d * LORA:3 * w_ + 2 * LORA + (d + 1) * LORA]
        a = _sigmoid(a0_ref[d:d + 1, :] + _dot(xa, a2_ref[d]))
        rk = rk + r * (k * (1.0 + (a - 1.0) * k_a)) * r_k
    bonus = _dot_hi(rk, hsum) * v
    gate = _dot(_sigmoid(xg), g2_ref[...])
    parts.append((yn + bonus) * gate)
    mixed = jnp.concatenate(parts, axis=-1).astype(BF16)
    o_ref[...] = x_ref[...] + gt_ref[...] * jnp.dot(mixed, wo_ref[...], preferred_element_type=F32)


def mix_out(x, gt1, o_f, o_b, p_gdn, y_f, y_b, xs, gnw, vecs, a0, a2, g2, w_out, n_ctx, tm):
    bsz, seq, d = x.shape
    off = n_ctx // tm
    nv = GDN_HEADS * GDN_DK
    ncol = xs.shape[2]
    hsum = _head_sum_matrix(RWKV_W, RWKV_HD)
    lat = lambda width, cblk=0: pl.BlockSpec((None, tm, width), lambda b, i: (b, i + off, cblk))
    full = lambda *shape: pl.BlockSpec(shape, lambda b, i: (0,) * len(shape))
    return pl.pallas_call(
        _mix_out_kernel,
        grid=(bsz, seq // tm),
        in_specs=[pl.BlockSpec((None, tm, d), lambda b, i: (b, i, 0)),
                  pl.BlockSpec((None, 1, d), lambda b, i: (b, 0, 0)),
                  lat(nv), lat(nv), lat(nv, 3), lat(RWKV_W), lat(RWKV_W), lat(ncol),
                  full(1, GDN_DK), full(8, RWKV_W), full(2, RWKV_W), full(2, LORA, RWKV_W),
                  full(2 * LORA, RWKV_W), full(RWKV_W, RWKV_W), full(nv + RWKV_W, d)],
        out_specs=pl.BlockSpec((None, tm, d), lambda b, i: (b, i, 0)),
        out_shape=jax.ShapeDtypeStruct((bsz, seq, d), F32),
        compiler_params=_cparams(("parallel", "parallel")),
        name="mix_out",
    )(x, gt1, o_f, o_b, p_gdn, y_f, y_b, xs, gnw, vecs, a0, a2, g2, hsum, w_out)


def _extract_top(s, riota, n_rows, vals_ref, idx_ref, k):
    for i in range(k):
        m = jnp.max(s, axis=0, keepdims=True)
        am = jnp.min(jnp.where(s == m, riota, float(n_rows)), axis=0, keepdims=True)
        vals_ref[i:i + 1, :] = m
        idx_ref[i:i + 1, :] = am
        s = jnp.where(riota == am, NEG_INF, s)


def _peer_route_kernel(x_ref, g_ref, sh_ref, sc_ref, wq_ref, keys_ref,
                       h2_ref, idx_ref, gate_ref,
                       q_scr, tv_scr, ti_scr, cs_scr, ci_scr, bv_scr, bi_scr):
    h = pl.program_id(1)
    dq = 2 * LANES
    kk = PEER_TOPK

    @pl.when(h == 0)
    def _():
        x = x_ref[...]
        y = x * lax.rsqrt(jnp.mean(x * x, axis=-1, keepdims=True) + NORM_EPS) * g_ref[...]
        h2 = y * (1.0 + sc_ref[...]) + sh_ref[...]
        h2_ref[...] = h2
        hb = h2.astype(BF16)
        for hh in range(PEER_HEADS):
            q_scr[hh] = jnp.dot(hb, wq_ref[:, hh * dq:(hh + 1) * dq], preferred_element_type=F32)

    q = q_scr[h]
    tm = q.shape[0]
    riota = lax.broadcasted_iota(jnp.int32, (PEER_NKEYS, tm), 0).astype(F32)
    for p in range(2):
        s_t = _dot_nt(keys_ref[p], q[:, p * LANES:(p + 1) * LANES])
        _extract_top(s_t, riota, PEER_NKEYS, tv_scr.at[p], ti_scr.at[p], kk)
    s1 = tv_scr[1]
    i1 = ti_scr[1]
    for i in range(kk):
        cs_scr[i * kk:(i + 1) * kk, :] = tv_scr[0, i:i + 1, :] + s1
        ci_scr[i * kk:(i + 1) * kk, :] = ti_scr[0, i:i + 1, :] * float(PEER_NKEYS) + i1
    ciota = lax.broadcasted_iota(jnp.int32, (kk * kk, tm), 0).astype(F32)
    cand = cs_scr[...]
    cand_i = ci_scr[...]
    for i in range(kk):
        m = jnp.max(cand, axis=0, keepdims=True)
        pos = jnp.min(jnp.where(cand == m, ciota, float(kk * kk)), axis=0, keepdims=True)
        hit = ciota == pos
        bv_scr[i:i + 1, :] = m
        bi_scr[i:i + 1, :] = jnp.sum(jnp.where(hit, cand_i, 0.0), axis=0, keepdims=True)
        cand = jnp.where(hit, NEG_INF, cand)
    best = bv_scr[...]
    e = jnp.exp(best - best[0:1, :])
    gate_ref[...] = e / jnp.sum(e, axis=0, keepdims=True)
    idx_ref[...] = bi_scr[...].astype(jnp.int32)


def peer_route(x1, norm_g, sh2, sc2, wq, sub_keys, tm):
    bsz, seq, d = x1.shape
    ntok = bsz * seq
    tiles_per_b = seq // tm
    kk = PEER_TOPK
    nq = wq.shape[1]
    return pl.pallas_call(
        _peer_route_kernel,
        grid=(ntok // tm, PEER_HEADS),
        in_specs=[pl.BlockSpec((tm, d), lambda i, h: (i, 0)),
                  pl.BlockSpec((1, d), lambda i, h: (0, 0)),
                  pl.BlockSpec((None, 1, d), lambda i, h: (i // tiles_per_b, 0, 0)),
                  pl.BlockSpec((None, 1, d), lambda i, h: (i // tiles_per_b, 0, 0)),
                  pl.BlockSpec((d, nq), lambda i, h: (0, 0)),
                  pl.BlockSpec((None, 2, PEER_NKEYS, LANES), lambda i, h: (h, 0, 0, 0))],
        out_specs=[pl.BlockSpec((tm, d), lambda i, h: (i, 0)),
                   pl.BlockSpec((kk, tm), lambda i, h: (h, i)),
                   pl.BlockSpec((kk, tm), lambda i, h: (h, i))],
        out_shape=[jax.ShapeDtypeStruct((ntok, d), F32),
                   jax.ShapeDtypeStruct((PEER_HEADS * kk, ntok), jnp.int32),
                   jax.ShapeDtypeStruct((PEER_HEADS * kk, ntok), F32)],
        scratch_shapes=[pltpu.VMEM((PEER_HEADS, tm, 2 * LANES), F32),
                        pltpu.VMEM((2, kk, tm), F32), pltpu.VMEM((2, kk, tm), F32),
                        pltpu.VMEM((kk * kk, tm), F32), pltpu.VMEM((kk * kk, tm), F32),
                        pltpu.VMEM((kk, tm), F32), pltpu.VMEM((kk, tm), F32)],
        compiler_params=_cparams(("parallel", "arbitrary")),
        name="peer_route",
    )(x1.reshape(ntok, d), norm_g.reshape(1, d), sh2, sc2, wq, sub_keys)


GROUP = 4
N_GSLOTS = 4
AHEAD = 2
N_SLOTS = GROUP * N_GSLOTS
ROW_TILE = 8


def _peer_expert_kernel(idx_ref, h2_ref, gate_ref, x1_ref, gt_ref, fg_ref, tbl_ref,
                        o_ref, *scratch, tt, d):
    bufs, tiles, sem = scratch[:N_SLOTS], scratch[N_SLOTS:2 * N_SLOTS], scratch[2 * N_SLOTS]
    ne = PEER_HEADS * PEER_TOPK
    step = pl.program_id(0)
    last_step = pl.num_programs(0) - 1

    def issue_token(base, gslot, j):
        slot = gslot * GROUP + j
        for k in range(ne):
            e = idx_ref[base + k]
            pltpu.make_async_copy(tbl_ref.at[e], bufs[slot].at[k], sem.at[gslot]).start()

    def wait_group(gslot):
        for j in range(GROUP):
            pltpu.make_async_copy(tbl_ref.at[pl.ds(0, ne)], bufs[gslot * GROUP + j],
                                  sem.at[gslot]).wait()

    def evaluate(t, slot):
        tile = tiles[slot]
        tile[...] = bufs[slot][...].reshape(ne, 2 * d)
        hrow = h2_ref[pl.ds(t, 1), :]
        grow = gate_ref[pl.ds(t, 1), :]
        lane = lax.broadcasted_iota(jnp.int32, (ROW_TILE, ne), 1)
        sub = lax.broadcasted_iota(jnp.int32, (ROW_TILE, ne), 0)
        acc = jnp.zeros((ROW_TILE, d), F32)
        for g in range(ne // ROW_TILE):
            rows = tile[g * ROW_TILE:(g + 1) * ROW_TILE, :]
            pre = jnp.sum(rows[:, :d] * hrow, axis=1, keepdims=True)
            act = 0.5 * pre * (1.0 + lax.erf(pre * (2.0 ** -0.5)))
            gcol = jnp.sum(jnp.where(lane == sub + g * ROW_TILE, grow, 0.0), axis=1, keepdims=True)
            acc = acc + (act * gcol) * rows[:, d:]
        y = jnp.sum(acc, axis=0, keepdims=True)
        xo = x1_ref[pl.ds(t, 1), :] + gt_ref[...] * y
        xo = xo * lax.rsqrt(jnp.mean(xo * xo, axis=-1, keepdims=True) + NORM_EPS) * fg_ref[...]
        o_ref[pl.ds(t, 1), :] = xo

    @pl.when(step == 0)
    def _():
        for g in range(AHEAD):
            for j in range(GROUP):
                issue_token((g * GROUP + j) * ne, g, j)

    def ring_turn(it, carry):
        for gs in range(N_GSLOTS):
            t0 = (it * N_GSLOTS + gs) * GROUP
            wait_group(gs)
            for j in range(GROUP):
                issue_token((t0 + AHEAD * GROUP + j) * ne, (gs + AHEAD) % N_GSLOTS, j)
                evaluate(t0 + j, gs * GROUP + j)
        return carry

    lax.fori_loop(0, tt // N_SLOTS, ring_turn, 0)

    @pl.when(step == last_step)
    def _():
        for g in range(AHEAD):
            wait_group(g)


def peer_expert(idx_tok, h2, gate_tok, x1, gt2, final_g, table, seq, tt):
    ntok, d = h2.shape
    ne = PEER_HEADS * PEER_TOPK
    tiles_per_b = seq // tt
    n_steps = ntok // tt
    assert tt % N_SLOTS == 0
    idx_steps = idx_tok.reshape(n_steps, tt * ne)
    tail = jnp.roll(idx_steps[:, :AHEAD * GROUP * ne], -1, axis=0)
    win = (tt + AHEAD * GROUP) * ne
    idx_win = jnp.concatenate([idx_steps, tail], axis=1).reshape(n_steps * win)
    return pl.pallas_call(
        functools.partial(_peer_expert_kernel, tt=tt, d=d),
        grid=(n_steps,),
        in_specs=[pl.BlockSpec((win,), lambda i: (i,), memory_space=pltpu.SMEM),
                  pl.BlockSpec((tt, d), lambda i: (i, 0)),
                  pl.BlockSpec((tt, ne), lambda i: (i, 0)),
                  pl.BlockSpec((tt, d), lambda i: (i, 0)),
                  pl.BlockSpec((None, 1, d), lambda i: (i // tiles_per_b, 0, 0)),
                  pl.BlockSpec((1, d), lambda i: (0, 0)),
                  pl.BlockSpec(memory_space=pl.ANY)],
        out_specs=pl.BlockSpec((tt, d), lambda i: (i, 0)),
        out_shape=jax.ShapeDtypeStruct((ntok, d), F32),
        scratch_shapes=([pltpu.VMEM((ne, 1, 2 * d), F32) for _ in range(N_SLOTS)]
                        + [pltpu.VMEM((ne, 2 * d), F32) for _ in range(N_SLOTS)]
                        + [pltpu.SemaphoreType.DMA((N_GSLOTS,))]),
        compiler_params=_cparams(("arbitrary",)),
        name="peer_expert",
    )(idx_win, h2, gate_tok, x1, gt2, final_g.reshape(1, d), table)


def kernel(x, c, ctx, c_ctx, ada_w, ada_b, norm1_g, w_in, gdn_conv_w, gdn_a_log, gdn_dt_bias,
           gdn_norm_w, rwkv_mu, rwkv_w0, rwkv_w2, rwkv_a0, rwkv_a2, rwkv_g2, rwkv_k_k, rwkv_k_a,
           rwkv_r_k, rwkv_gn_w, rwkv_gn_b, w_out, norm2_g, peer_w_query, peer_sub_keys,
           peer_down, peer_up, final_norm_g):
    bsz, seq, d = x.shape
    n_ctx = ctx.shape[1]
    assert ada_w.shape[0] == 1, "single-layer block"
    assert n_ctx % 256 == 0 and seq % 256 == 0 and seq % GRID_W == 0
    nh = GDN_HEADS
    n_qkvz = 4 * nh * GDN_DK
    gdn_cols = n_qkvz + 4 * nh

    cond = jnp.concatenate([c, c_ctx[None, :], jnp.zeros((16 - bsz - 1, d), F32)], axis=0)
    mods = ada_mod(cond, ada_w[0], ada_b[0])
    m_lat = mods[:bsz].reshape(bsz, N_MOD, 1, d)
    m_ctx = jnp.broadcast_to(mods[bsz].reshape(1, N_MOD, 1, d), (bsz, N_MOD, 1, d))
    shift1 = jnp.stack([m_ctx[:, 0], m_lat[:, 0]], axis=1)
    scale1 = jnp.stack([m_ctx[:, 1], m_lat[:, 1]], axis=1)
    gt1, sh2, sc2, gt2 = m_lat[:, 2], m_lat[:, 3], m_lat[:, 4], m_lat[:, 5]

    wl = w_in[0]
    w_gdn = jnp.concatenate([wl[:, :gdn_cols], jnp.zeros((d, LANES - 4 * nh), F32)], axis=1)
    w_rwkv = wl[:, gdn_cols:]
    xin = jnp.concatenate([ctx, x], axis=1)
    p_gdn, p_rwkv = in_proj(xin, norm1_g[0], shift1, scale1, w_gdn.astype(BF16),
                            w_rwkv.astype(BF16), n_ctx, 256)

    qkv = gdn_prep(p_gdn, gdn_conv_w[0], n_ctx)
    prm = jnp.zeros((8, LANES), F32)
    prm = prm.at[0, :2 * nh].set(gdn_a_log[0].reshape(-1))
    prm = prm.at[1, :2 * nh].set(gdn_dt_bias[0].reshape(-1))
    o_f, o_b = gdn_chunk(qkv, p_gdn, prm, n_ctx)

    xs = rwkv_shift(p_rwkv, rwkv_mu[0], n_ctx)
    vecs = jnp.zeros((8, RWKV_W), F32)
    for i, vec in enumerate((rwkv_k_k, rwkv_k_a, rwkv_r_k, rwkv_gn_w, rwkv_gn_b)):
        vecs = vecs.at[i].set(vec[0])
    y_f, y_b = rwkv_chunk(xs, vecs, rwkv_w0[0], rwkv_w2[0], rwkv_a0[0], rwkv_a2[0], n_ctx)

    x1 = mix_out(x, gt1, o_f, o_b, p_gdn, y_f, y_b, xs, gdn_norm_w[0].reshape(1, GDN_DK), vecs,
                 rwkv_a0[0], rwkv_a2[0], rwkv_g2[0], w_out[0].astype(BF16), n_ctx, 256)

    h2, idx_t, gate_t = peer_route(x1, norm2_g[0], sh2, sc2, peer_w_query[0].astype(BF16),
                                   peer_sub_keys[0], 256)
    table = jnp.concatenate([peer_down[0], peer_up[0]], axis=1)[:, None, :]
    out = peer_expert(idx_t.T, h2, gate_t.T, x1.reshape(bsz * seq, d), gt2,
                      final_norm_g, table, seq, 128)
    return out.reshape(bsz, seq, d)
```

```python
import functools
import math

import jax
import jax.numpy as jnp
from jax import lax
from jax.experimental import pallas as pl
from jax.experimental.pallas import tpu as pltpu

F32 = jnp.float32
BF16 = jnp.bfloat16

GRID_W = 64
GDN_HEADS = 4
GDN_DK = 128
GDN_CONV = 5
CHUNK = 64
RWKV_HEADS = 8
RWKV_HD = 64
RWKV_W = RWKV_HEADS * RWKV_HD
LORA = 64
PEER_HEADS = 8
PEER_NKEYS = 128
PEER_TOPK = 16
N_MOD = 6

NORM_EPS = 1e-6
L2_EPS = 1e-6
RWKV_GN_EPS = 64e-5

LANES = 128
VMEM_LIMIT = 56 * 1024 * 1024
NEG_INF = float("-inf")


def _cparams(sem):
    return pltpu.CompilerParams(dimension_semantics=sem, vmem_limit_bytes=VMEM_LIMIT)


def _dot(a, b):
    return jnp.dot(a.astype(BF16), b.astype(BF16), preferred_element_type=F32)


def _dot_nt(a, b):
    return lax.dot_general(a.astype(BF16), b.astype(BF16), (((1,), (1,)), ((), ())),
                           preferred_element_type=F32)


def _dot_tn(a, b):
    return lax.dot_general(a.astype(BF16), b.astype(BF16), (((0,), (0,)), ((), ())),
                           preferred_element_type=F32)


def _dot_hi(a, b):
    return jnp.dot(a, b, preferred_element_type=F32, precision=lax.Precision.HIGHEST)


def _sigmoid(x):
    return 1.0 / (1.0 + jnp.exp(-x))


def _silu(x):
    return x * _sigmoid(x)


def _softplus(x):
    return jnp.maximum(x, 0.0) + jnp.log(1.0 + jnp.exp(-jnp.abs(x)))


def _unit_lower_inverses(xs):
    c = xs[0].shape[0]
    eye = (lax.broadcasted_iota(jnp.int32, (c, c), 0) ==
           lax.broadcasted_iota(jnp.int32, (c, c), 1)).astype(F32)
    prods = [eye + x for x in xs]
    pows = [_dot(x, x) for x in xs]
    for _ in range(int(math.log2(c)) - 2):
        both = [_dot(jnp.concatenate([p, x], axis=0), x) for p, x in zip(prods, pows)]
        prods = [p + b[:c] for p, b in zip(prods, both)]
        pows = [b[c:] for b in both]
    return [p + _dot(p, x) for p, x in zip(prods, pows)]


def _order_masks(c, rev):
    r = lax.broadcasted_iota(jnp.int32, (c, c), 0)
    s = lax.broadcasted_iota(jnp.int32, (c, c), 1)
    if rev:
        return s >= r, s > r
    return s <= r, s < r


def _ada_kernel(c_ref, w_ref, b_ref, o_ref):
    o_ref[...] = _dot_hi(_silu(c_ref[...]), w_ref[...]) + b_ref[...]


def ada_mod(cond, w, b):
    n, d = cond.shape
    cols = w.shape[1]
    tn = 1024
    return pl.pallas_call(
        _ada_kernel,
        grid=(cols // tn,),
        in_specs=[pl.BlockSpec((n, d), lambda j: (0, 0)),
                  pl.BlockSpec((d, tn), lambda j: (0, j)),
                  pl.BlockSpec((1, tn), lambda j: (0, j))],
        out_specs=pl.BlockSpec((n, tn), lambda j: (0, j)),
        out_shape=jax.ShapeDtypeStruct((n, cols), F32),
        compiler_params=_cparams(("arbitrary",)),
        name="ada_mod",
    )(cond, w, b.reshape(1, cols))


def _in_proj_kernel(x_ref, g_ref, sh_ref, sc_ref, wg_ref, wr_ref, pg_ref, pr_ref):
    x = x_ref[...]
    y = x * lax.rsqrt(jnp.mean(x * x, axis=-1, keepdims=True) + NORM_EPS) * g_ref[...]
    h = (y * (1.0 + sc_ref[...]) + sh_ref[...]).astype(BF16)
    pg_ref[...] = jnp.dot(h, wg_ref[...], preferred_element_type=F32)
    pr_ref[...] = jnp.dot(h, wr_ref[...], preferred_element_type=F32)


def in_proj(xin, norm_g, shift, scale, w_gdn, w_rwkv, n_ctx, tm):
    bsz, t, d = xin.shape
    cg, cr = w_gdn.shape[1], w_rwkv.shape[1]
    nctx_tiles = n_ctx // tm

    def mod_map(b, i):
        return (b, jnp.where(i < nctx_tiles, 0, 1), 0, 0)

    return pl.pallas_call(
        _in_proj_kernel,
        grid=(bsz, t // tm),
        in_specs=[pl.BlockSpec((None, tm, d), lambda b, i: (b, i, 0)),
                  pl.BlockSpec((1, d), lambda b, i: (0, 0)),
                  pl.BlockSpec((None, None, 1, d), mod_map),
                  pl.BlockSpec((None, None, 1, d), mod_map),
                  pl.BlockSpec((d, cg), lambda b, i: (0, 0)),
                  pl.BlockSpec((d, cr), lambda b, i: (0, 0))],
        out_specs=[pl.BlockSpec((None, tm, cg), lambda b, i: (b, i, 0)),
                   pl.BlockSpec((None, tm, cr), lambda b, i: (b, i, 0))],
        out_shape=[jax.ShapeDtypeStruct((bsz, t, cg), F32),
                   jax.ShapeDtypeStruct((bsz, t, cr), F32)],
        compiler_params=_cparams(("parallel", "parallel")),
        name="in_proj",
    )(xin, norm_g.reshape(1, d), shift, scale, w_gdn, w_rwkv)


def _gdn_prep_kernel(p_ref, w_ref, o_ref, *, n_ctx):
    j = pl.program_id(1)
    x = p_ref[...]
    t_len = x.shape[0]
    w = w_ref[...]
    t = lax.broadcasted_iota(jnp.int32, x.shape, 0)
    is_ctx = t < n_ctx
    lo = jnp.where(is_ctx, 0, n_ctx)
    hi = jnp.where(is_ctx, n_ctx, t_len)
    pad = (GDN_CONV - 1) // 2
    acc = x * w[pad:pad + 1, :]
    for s in range(-pad, pad + 1):
        if s == 0:
            continue
        xs = pltpu.roll(x, (-s) % t_len, 0)
        valid = (t + s >= lo) & (t + s < hi)
        acc = acc + jnp.where(valid, xs, 0.0) * w[s + pad:s + pad + 1, :]
    y = _silu(acc)
    inv = lax.rsqrt(jnp.sum(y * y, axis=-1, keepdims=True) + L2_EPS)
    fac = jnp.where(j < GDN_HEADS, inv * (GDN_DK ** -0.5), jnp.where(j < 2 * GDN_HEADS, inv, 1.0))
    o_ref[...] = y * fac


def gdn_prep(p_gdn, conv_w, n_ctx):
    bsz, t, _ = p_gdn.shape
    ncol = conv_w.shape[1]
    return pl.pallas_call(
        functools.partial(_gdn_prep_kernel, n_ctx=n_ctx),
        grid=(bsz, ncol // LANES),
        in_specs=[pl.BlockSpec((None, t, LANES), lambda b, j: (b, 0, j)),
                  pl.BlockSpec((GDN_CONV, LANES), lambda b, j: (0, j))],
        out_specs=pl.BlockSpec((None, t, LANES), lambda b, j: (b, 0, j)),
        out_shape=jax.ShapeDtypeStruct((bsz, t, ncol), F32),
        compiler_params=_cparams(("parallel", "parallel")),
        name="gdn_prep",
    )(p_gdn, conv_w)


def _gdn_chunk_kernel(qf_ref, qb_ref, gf_ref, gb_ref, prm_ref, of_ref, ob_ref, s_ref):
    @pl.when(pl.program_id(1) == 0)
    def _():
        s_ref[...] = jnp.zeros_like(s_ref)

    nh, dk = GDN_HEADS, GDN_DK
    prm = prm_ref[...]
    a_log, dt_bias = prm[0:1, :], prm[1:2, :]
    q, k, v, beta, g_col, g_last, gamma, strict = [], [], [], [], [], [], [], []
    for d, (qkv_ref, gate_ref) in enumerate(((qf_ref, gf_ref), (qb_ref, gb_ref))):
        qkv = qkv_ref[...]
        gates = gate_ref[...]
        c = qkv.shape[0]
        incl, strict_d = _order_masks(c, d == 1)
        g_all = -jnp.exp(a_log) * _softplus(gates + dt_bias)
        beta_all = _sigmoid(gates)
        big_g = _dot_hi(incl.astype(F32), g_all)
        big_g_t = big_g.T
        last = 0 if d == 1 else c - 1
        for h in range(nh):
            col = d * nh + h
            q.append(qkv[:, h * dk:(h + 1) * dk])
            k.append(qkv[:, (nh + h) * dk:(nh + h + 1) * dk])
            v.append(qkv[:, (2 * nh + h) * dk:(2 * nh + h + 1) * dk])
            beta.append(beta_all[:, 2 * nh + col:2 * nh + col + 1])
            g_col.append(big_g[:, col:col + 1])
            g_last.append(big_g[last:last + 1, col:col + 1])
            g_row = big_g_t[col:col + 1, :]
            gamma.append(jnp.where(incl, jnp.exp(jnp.where(incl, g_col[-1] - g_row, 0.0)), 0.0))
            strict.append(strict_d)
    n = len(q)
    c = q[0].shape[0]
    e_g = [jnp.exp(g) for g in g_col]
    kq = [_dot_nt(jnp.concatenate([k[i], q[i]], axis=0), k[i]) for i in range(n)]
    a_mat = [jnp.where(strict[i], beta[i] * kq[i][:c] * gamma[i], 0.0) for i in range(n)]
    qk = [kq[i][c:] * gamma[i] for i in range(n)]
    t_inv = _unit_lower_inverses([-a for a in a_mat])
    uw = [_dot(t_inv[i], jnp.concatenate([beta[i] * v[i], (beta[i] * e_g[i]) * k[i]], axis=1))
          for i in range(n)]
    s = [s_ref[i // nh, i % nh] for i in range(n)]
    ws = [_dot(jnp.concatenate([uw[i][:, dk:], q[i] * e_g[i]], axis=0), s[i]) for i in range(n)]
    v_new = [uw[i][:, :dk] - ws[i][:c] for i in range(n)]
    o = [ws[i][c:] + _dot(qk[i], v_new[i]) for i in range(n)]
    s_new = [s[i] * jnp.exp(g_last[i]) + _dot_tn(k[i] * jnp.exp(g_last[i] - g_col[i]), v_new[i])
             for i in range(n)]
    for i in range(n):
        s_ref[i // nh, i % nh] = s_new[i]
    of_ref[...] = jnp.concatenate(o[:nh], axis=-1)
    ob_ref[...] = jnp.concatenate(o[nh:], axis=-1)


def _rev_chunk_map(n_ctx_chunks, n_chunks):
    def cb(s):
        return jnp.where(s < n_ctx_chunks, n_ctx_chunks - 1 - s, n_chunks + n_ctx_chunks - 1 - s)
    return cb


def gdn_chunk(qkv, p_gdn, prm, n_ctx):
    bsz, t, ncol = qkv.shape
    nv = GDN_HEADS * GDN_DK
    n_chunks = t // CHUNK
    cb = _rev_chunk_map(n_ctx // CHUNK, n_chunks)
    gate_tile = p_gdn.shape[2] // LANES - 1
    return pl.pallas_call(
        _gdn_chunk_kernel,
        grid=(bsz, n_chunks),
        in_specs=[pl.BlockSpec((None, CHUNK, ncol), lambda b, s: (b, s, 0)),
                  pl.BlockSpec((None, CHUNK, ncol), lambda b, s: (b, cb(s), 0)),
                  pl.BlockSpec((None, CHUNK, LANES), lambda b, s: (b, s, gate_tile)),
                  pl.BlockSpec((None, CHUNK, LANES), lambda b, s: (b, cb(s), gate_tile)),
                  pl.BlockSpec((8, LANES), lambda b, s: (0, 0))],
        out_specs=[pl.BlockSpec((None, CHUNK, nv), lambda b, s: (b, s, 0)),
                   pl.BlockSpec((None, CHUNK, nv), lambda b, s: (b, cb(s), 0))],
        out_shape=[jax.ShapeDtypeStruct((bsz, t, nv), F32)] * 2,
        scratch_shapes=[pltpu.VMEM((2, GDN_HEADS, GDN_DK, GDN_DK), F32)],
        compiler_params=_cparams(("parallel", "arbitrary")),
        name="gdn_chunk",
    )(qkv, qkv, p_gdn, p_gdn, prm)


def _rwkv_shift_kernel(p_ref, mu_ref, o_ref, *, n_ctx, n_cols):
    j = pl.program_id(1)
    x = p_ref[...]
    t_len = x.shape[0]
    t = lax.broadcasted_iota(jnp.int32, x.shape, 0)
    ch = lax.broadcasted_iota(jnp.int32, x.shape, 1) + j * LANES
    is_ctx = t < n_ctx
    col = (t - n_ctx) % GRID_W
    prev1 = pltpu.roll(x, 1, 0)
    next1 = pltpu.roll(x, t_len - 1, 0)
    up = pltpu.roll(x, GRID_W, 0)
    down = pltpu.roll(x, t_len - GRID_W, 0)
    quarter = n_cols // 4
    half = n_cols // 2
    ctx_sh = jnp.where(ch < half,
                       jnp.where(t >= 1, prev1, 0.0),
                       jnp.where(t < n_ctx - 1, next1, 0.0))
    lat_sh = jnp.where(ch < quarter, jnp.where(col >= 1, prev1, 0.0),
                       jnp.where(ch < 2 * quarter, jnp.where(col < GRID_W - 1, next1, 0.0),
                                 jnp.where(ch < 3 * quarter,
                                           jnp.where(t >= n_ctx + GRID_W, up, 0.0),
                                           jnp.where(t < t_len - GRID_W, down, 0.0))))
    sh = jnp.where(is_ctx, ctx_sh, lat_sh)
    o_ref[...] = x + mu_ref[...] * (sh - x)


def rwkv_shift(p_rwkv, mu, n_ctx):
    bsz, t, ncol = p_rwkv.shape
    return pl.pallas_call(
        functools.partial(_rwkv_shift_kernel, n_ctx=n_ctx, n_cols=ncol),
        grid=(bsz, ncol // LANES),
        in_specs=[pl.BlockSpec((None, t, LANES), lambda b, j: (b, 0, j)),
                  pl.BlockSpec((1, LANES), lambda b, j: (0, j))],
        out_specs=pl.BlockSpec((None, t, LANES), lambda b, j: (b, 0, j)),
        out_shape=jax.ShapeDtypeStruct((bsz, t, ncol), F32),
        compiler_params=_cparams(("parallel", "parallel")),
        name="rwkv_shift",
    )(p_rwkv, mu.reshape(1, ncol))


def _rwkv_gates(xs, d, w0_ref, w2_ref, a0_ref, a2_ref):
    w_ = RWKV_W
    xw = xs[:, 3 * w_ + d * LORA:3 * w_ + (d + 1) * LORA]
    xa = xs[:, 3 * w_ + 2 * LORA + d * LORA:3 * w_ + 2 * LORA + (d + 1) * LORA]
    w_pre = w0_ref[d:d + 1, :] + _dot(jnp.tanh(xw), w2_ref[d])
    log_w = -math.exp(-0.5) * _sigmoid(w_pre)
    a = _sigmoid(a0_ref[d:d + 1, :] + _dot(xa, a2_ref[d]))
    return log_w, a


def _rwkv_chunk_kernel(xf_ref, xb_ref, vec_ref, w0_ref, w2_ref, a0_ref, a2_ref, hsum_ref,
                       yf_ref, yb_ref, p_ref):
    @pl.when(pl.program_id(1) == 0)
    def _():
        p_ref[...] = jnp.zeros_like(p_ref)

    w_, n, nh = RWKV_W, RWKV_HD, RWKV_HEADS
    k_k, k_a = vec_ref[0:1, :], vec_ref[1:2, :]
    lhs, rhs_t, tail, vals, w_last, p0, strict, incl = [], [], [], [], [], [], [], []
    for d, x_ref in enumerate((xf_ref, xb_ref)):
        xs = x_ref[...]
        c = xs.shape[0]
        incl_d, strict_d = _order_masks(c, d == 1)
        r = xs[:, 0:w_]
        k = xs[:, w_:2 * w_]
        v = xs[:, 2 * w_:3 * w_]
        log_w, a = _rwkv_gates(xs, d, w0_ref, w2_ref, a0_ref, a2_ref)
        kk = k * k_k
        kk = kk * lax.rsqrt(_dot_hi(kk * kk, hsum_ref[...]) + L2_EPS)
        k_dir = k * (1.0 + (a - 1.0) * k_a)
        alpha = -(kk * a)
        big_g = _dot_hi(incl_d.astype(F32), log_w)
        last = 0 if d == 1 else c - 1
        g_last = big_g[last:last + 1, :]
        e_neg = jnp.exp(-big_g)
        e_tail = jnp.exp(g_last - big_g)
        r_t = r * jnp.exp(big_g)
        b_t = kk * jnp.exp(big_g - log_w)
        a_t = alpha * e_neg
        k_t = k_dir * e_neg
        a_h = alpha * e_tail
        k_h = k_dir * e_tail
        w_l = jnp.exp(g_last)
        for h in range(nh):
            sl = slice(h * n, (h + 1) * n)
            lhs.append(jnp.concatenate([b_t[:, sl], r_t[:, sl]], axis=0))
            rhs_t.append(jnp.concatenate([a_t[:, sl], k_t[:, sl]], axis=0))
            tail.append(jnp.concatenate([a_h[:, sl], k_h[:, sl]], axis=0))
            vals.append(v[:, sl])
            w_last.append(w_l[:, sl])
            p0.append(p_ref[d, h])
            strict.append(strict_d)
            incl.append(incl_d)
    m = len(lhs)
    c = vals[0].shape[0]
    quad = [_dot_nt(lhs[i], rhs_t[i]) for i in range(m)]
    a_ba = [jnp.where(strict[i], quad[i][:c, :c], 0.0) for i in range(m)]
    a_k = [jnp.concatenate([jnp.where(strict[i], quad[i][:c, c:], 0.0),
                            jnp.where(incl[i], quad[i][c:, c:], 0.0)], axis=0) for i in range(m)]
    a_ra = [jnp.where(incl[i], quad[i][c:, :c], 0.0) for i in range(m)]
    from_state = [_dot_nt(lhs[i], p0[i]) for i in range(m)]
    from_vals = [_dot(a_k[i], vals[i]) for i in range(m)]
    t_inv = _unit_lower_inverses(a_ba)
    u = [_dot(t_inv[i], from_state[i][:c] + from_vals[i][:c]) for i in range(m)]
    y = [from_state[i][c:] + _dot(a_ra[i], u[i]) + from_vals[i][c:] for i in range(m)]
    p_new = [p0[i] * w_last[i] + _dot_tn(jnp.concatenate([u[i], vals[i]], axis=0), tail[i])
             for i in range(m)]
    for i in range(m):
        p_ref[i // nh, i % nh] = p_new[i]
    yf_ref[...] = jnp.concatenate(y[:nh], axis=-1)
    yb_ref[...] = jnp.concatenate(y[nh:], axis=-1)


def _head_sum_matrix(width, group):
    i = jnp.arange(width)
    return (i[:, None] // group == i[None, :] // group).astype(F32)


def rwkv_chunk(xs, vecs, w0, w2, a0, a2, n_ctx):
    bsz, t, ncol = xs.shape
    n_chunks = t // CHUNK
    cb = _rev_chunk_map(n_ctx // CHUNK, n_chunks)
    hsum = _head_sum_matrix(RWKV_W, RWKV_HD)
    full = lambda *shape: pl.BlockSpec(shape, lambda b, s: (0,) * len(shape))
    return pl.pallas_call(
        _rwkv_chunk_kernel,
        grid=(bsz, n_chunks),
        in_specs=[pl.BlockSpec((None, CHUNK, ncol), lambda b, s: (b, s, 0)),
                  pl.BlockSpec((None, CHUNK, ncol), lambda b, s: (b, cb(s), 0)),
                  full(8, RWKV_W), full(2, RWKV_W), full(2, LORA, RWKV_W),
                  full(2, RWKV_W), full(2, LORA, RWKV_W), full(RWKV_W, RWKV_W)],
        out_specs=[pl.BlockSpec((None, CHUNK, RWKV_W), lambda b, s: (b, s, 0)),
                   pl.BlockSpec((None, CHUNK, RWKV_W), lambda b, s: (b, cb(s), 0))],
        out_shape=[jax.ShapeDtypeStruct((bsz, t, RWKV_W), F32)] * 2,
        scratch_shapes=[pltpu.VMEM((2, RWKV_HEADS, RWKV_HD, RWKV_HD), F32)],
        compiler_params=_cparams(("parallel", "arbitrary")),
        name="rwkv_chunk",
    )(xs, xs, vecs, w0, w2, a0, a2, hsum)


def _mix_out_kernel(x_ref, gt_ref, of_ref, ob_ref, z_ref, yf_ref, yb_ref, xs_ref,
                    gnw_ref, vec_ref, a0_ref, a2_ref, g2_ref, hsum_ref, wo_ref, o_ref):
    nh, dv = GDN_HEADS, GDN_DK
    w_ = RWKV_W
    o = of_ref[...] + ob_ref[...]
    z = z_ref[...]
    parts = []
    for h in range(nh):
        oh = o[:, h * dv:(h + 1) * dv]
        oh = oh * lax.rsqrt(jnp.mean(oh * oh, axis=-1, keepdims=True) + NORM_EPS)
        parts.append(oh * gnw_ref[...] * _silu(z[:, h * dv:(h + 1) * dv]))
    xs = xs_ref[...]
    r = xs[:, 0:w_]
    k = xs[:, w_:2 * w_]
    v = xs[:, 2 * w_:3 * w_]
    xg = xs[:, 3 * w_ + 4 * LORA:]
    k_a, r_k = vec_ref[1:2, :], vec_ref[2:3, :]
    gn_w, gn_b = vec_ref[3:4, :], vec_ref[4:5, :]
    hsum = hsum_ref[...]
    y = yf_ref[...] + yb_ref[...]
    mean = _dot_hi(y, hsum) * (1.0 / RWKV_HD)
    yc = y - mean
    var = _dot_hi(yc * yc, hsum) * (1.0 / RWKV_HD)
    yn = yc * lax.rsqrt(var + RWKV_GN_EPS) * gn_w + gn_b
    rk = jnp.zeros_like(r)
    for d in range(2):
        xa = xs[:, 3 * w_ + 2 * LORA + d * LORA:3 * w_ + 2 * LORA + (d + 1) * LORA]
        a = _sigmoid(a0_ref[d:d + 1, :] + _dot(xa, a2_ref[d]))
        rk = rk + r * (k * (1.0 + (a - 1.0) * k_a)) * r_k
    bonus = _dot_hi(rk, hsum) * v
    gate = _dot(_sigmoid(xg), g2_ref[...])
    parts.append((yn + bonus) * gate)
    mixed = jnp.concatenate(parts, axis=-1).astype(BF16)
    o_ref[...] = x_ref[...] + gt_ref[...] * jnp.dot(mixed, wo_ref[...], preferred_element_type=F32)


def mix_out(x, gt1, o_f, o_b, p_gdn, y_f, y_b, xs, gnw, vecs, a0, a2, g2, w_out, n_ctx, tm):
    bsz, seq, d = x.shape
    off = n_ctx // tm
    nv = GDN_HEADS * GDN_DK
    ncol = xs.shape[2]
    hsum = _head_sum_matrix(RWKV_W, RWKV_HD)
    lat = lambda width, cblk=0: pl.BlockSpec((None, tm, width), lambda b, i: (b, i + off, cblk))
    full = lambda *shape: pl.BlockSpec(shape, lambda b, i: (0,) * len(shape))
    return pl.pallas_call(
        _mix_out_kernel,
        grid=(bsz, seq // tm),
        in_specs=[pl.BlockSpec((None, tm, d), lambda b, i: (b, i, 0)),
                  pl.BlockSpec((None, 1, d), lambda b, i: (b, 0, 0)),
                  lat(nv), lat(nv), lat(nv, 3), lat(RWKV_W), lat(RWKV_W), lat(ncol),
                  full(1, GDN_DK), full(8, RWKV_W), full(2, RWKV_W), full(2, LORA, RWKV_W),
                  full(2 * LORA, RWKV_W), full(RWKV_W, RWKV_W), full(nv + RWKV_W, d)],
        out_specs=pl.BlockSpec((None, tm, d), lambda b, i: (b, i, 0)),
        out_shape=jax.ShapeDtypeStruct((bsz, seq, d), F32),
        compiler_params=_cparams(("parallel", "parallel")),
        name="mix_out",
    )(x, gt1, o_f, o_b, p_gdn, y_f, y_b, xs, gnw, vecs, a0, a2, g2, hsum, w_out)


def _extract_top(s, riota, n_rows, vals_ref, idx_ref, k):
    for i in range(k):
        m = jnp.max(s, axis=0, keepdims=True)
        am = jnp.min(jnp.where(s == m, riota, float(n_rows)), axis=0, keepdims=True)
        vals_ref[i:i + 1, :] = m
        idx_ref[i:i + 1, :] = am
        s = jnp.where(riota == am, NEG_INF, s)


def _candidate_blocks(k):
    split = math.isqrt(k)
    blocks = [(False, i, 0, k // (i + 1)) for i in range(split)]
    j = 0
    while k // (j + 1) > split:
        blocks.append((True, j, split, k // (j + 1)))
        j += 1
    return blocks


def _candidate_rows(k):
    return sum(-(-(hi - lo) // 8) * 8 for _, _, lo, hi in _candidate_blocks(k))


def _top_list_rows(k):
    need = max(lo + -(-(hi - lo) // 8) * 8 for _, _, lo, hi in _candidate_blocks(k))
    return -(-need // 8) * 8


def _peer_route_kernel(x_ref, g_ref, sh_ref, sc_ref, wq_ref, keys_ref,
                       h2_ref, idx_ref, gate_ref,
                       q_scr, tv_scr, ti_scr, cs_scr, ci_scr, cf_scr, bv_scr, bi_scr):
    h = pl.program_id(1)
    dq = 2 * LANES
    kk = PEER_TOPK

    @pl.when(h == 0)
    def _():
        x = x_ref[...]
        y = x * lax.rsqrt(jnp.mean(x * x, axis=-1, keepdims=True) + NORM_EPS) * g_ref[...]
        h2 = y * (1.0 + sc_ref[...]) + sh_ref[...]
        h2_ref[...] = h2
        hb = h2.astype(BF16)
        for hh in range(PEER_HEADS):
            q_scr[hh] = jnp.dot(hb, wq_ref[:, hh * dq:(hh + 1) * dq], preferred_element_type=F32)

    q = q_scr[h]
    tm = q.shape[0]
    riota = lax.broadcasted_iota(jnp.int32, (PEER_NKEYS, tm), 0).astype(F32)
    for p in range(2):
        s_t = _dot_nt(keys_ref[p], q[:, p * LANES:(p + 1) * LANES])
        _extract_top(s_t, riota, PEER_NKEYS, tv_scr.at[p], ti_scr.at[p], kk)
    tv_scr[:, kk:, :] = jnp.zeros_like(tv_scr[:, kk:, :])
    ti_scr[:, kk:, :] = jnp.zeros_like(ti_scr[:, kk:, :])
    row = 0
    for fixed_j, fixed, lo, hi in _candidate_blocks(kk):
        n_pad = -(-(hi - lo) // 8) * 8
        r = lax.broadcasted_iota(jnp.int32, (n_pad, tm), 0).astype(F32)
        run, one = (0, 1) if fixed_j else (1, 0)
        vals = tv_scr[run, lo:lo + n_pad, :] + tv_scr[one, fixed:fixed + 1, :]
        run_i, one_i = ti_scr[run, lo:lo + n_pad, :], ti_scr[one, fixed:fixed + 1, :]
        if fixed_j:
            experts = run_i * float(PEER_NKEYS) + one_i
            flat = (r + float(lo)) * float(kk) + float(fixed)
        else:
            experts = one_i * float(PEER_NKEYS) + run_i
            flat = float(fixed * kk + lo) + r
        cs_scr[row:row + n_pad, :] = jnp.where(r < float(hi - lo), vals, NEG_INF)
        ci_scr[row:row + n_pad, :] = experts
        cf_scr[row:row + n_pad, :] = flat
        row += n_pad
    cand = cs_scr[...]
    cand_i = ci_scr[...]
    ciota = cf_scr[...]
    for i in range(kk):
        m = jnp.max(cand, axis=0, keepdims=True)
        pos = jnp.min(jnp.where(cand == m, ciota, float(kk * kk)), axis=0, keepdims=True)
        hit = ciota == pos
        bv_scr[i:i + 1, :] = m
        bi_scr[i:i + 1, :] = jnp.sum(jnp.where(hit, cand_i, 0.0), axis=0, keepdims=True)
        cand = jnp.where(hit, NEG_INF, cand)
    best = bv_scr[...]
    e = jnp.exp(best - best[0:1, :])
    gate_ref[...] = e / jnp.sum(e, axis=0, keepdims=True)
    idx_ref[...] = bi_scr[...].astype(jnp.int32)


def peer_route(x1, norm_g, sh2, sc2, wq, sub_keys, tm):
    bsz, seq, d = x1.shape
    ntok = bsz * seq
    tiles_per_b = seq // tm
    kk = PEER_TOPK
    nq = wq.shape[1]
    return pl.pallas_call(
        _peer_route_kernel,
        grid=(ntok // tm, PEER_HEADS),
        in_specs=[pl.BlockSpec((tm, d), lambda i, h: (i, 0)),
                  pl.BlockSpec((1, d), lambda i, h: (0, 0)),
                  pl.BlockSpec((None, 1, d), lambda i, h: (i // tiles_per_b, 0, 0)),
                  pl.BlockSpec((None, 1, d), lambda i, h: (i // tiles_per_b, 0, 0)),
                  pl.BlockSpec((d, nq), lambda i, h: (0, 0)),
                  pl.BlockSpec((None, 2, PEER_NKEYS, LANES), lambda i, h: (h, 0, 0, 0))],
        out_specs=[pl.BlockSpec((tm, d), lambda i, h: (i, 0)),
                   pl.BlockSpec((kk, tm), lambda i, h: (h, i)),
                   pl.BlockSpec((kk, tm), lambda i, h: (h, i))],
        out_shape=[jax.ShapeDtypeStruct((ntok, d), F32),
                   jax.ShapeDtypeStruct((PEER_HEADS * kk, ntok), jnp.int32),
                   jax.ShapeDtypeStruct((PEER_HEADS * kk, ntok), F32)],
        scratch_shapes=[pltpu.VMEM((PEER_HEADS, tm, 2 * LANES), F32),
                        pltpu.VMEM((2, _top_list_rows(kk), tm), F32),
                        pltpu.VMEM((2, _top_list_rows(kk), tm), F32),
                        pltpu.VMEM((_candidate_rows(kk), tm), F32),
                        pltpu.VMEM((_candidate_rows(kk), tm), F32),
                        pltpu.VMEM((_candidate_rows(kk), tm), F32),
                        pltpu.VMEM((kk, tm), F32), pltpu.VMEM((kk, tm), F32)],
        compiler_params=_cparams(("parallel", "arbitrary")),
        name="peer_route",
    )(x1.reshape(ntok, d), norm_g.reshape(1, d), sh2, sc2, wq, sub_keys)


GROUP = 4
N_GSLOTS = 4
AHEAD = 2
N_SLOTS = GROUP * N_GSLOTS
ROW_TILE = 8
DMA_QUEUES = 2


def _peer_expert_kernel(idx_ref, h2_ref, gate_ref, x1_ref, gt_ref, fg_ref, tbl_ref,
                        o_ref, *scratch, tt, d):
    bufs, tiles, sem = scratch[:N_SLOTS], scratch[N_SLOTS:2 * N_SLOTS], scratch[2 * N_SLOTS]
    ne = PEER_HEADS * PEER_TOPK
    step = pl.program_id(0)
    last_step = pl.num_programs(0) - 1

    def issue_token(base, gslot, j):
        slot = gslot * GROUP + j
        for k in range(ne):
            e = idx_ref[base + k]
            pltpu.make_async_copy(tbl_ref.at[e], bufs[slot].at[k],
                                  sem.at[gslot]).start(priority=k % DMA_QUEUES)

    def wait_group(gslot):
        for j in range(GROUP):
            pltpu.make_async_copy(tbl_ref.at[pl.ds(0, ne)], bufs[gslot * GROUP + j],
                                  sem.at[gslot]).wait()

    def evaluate(t, slot):
        tile = tiles[slot]
        tile[...] = bufs[slot][...].reshape(ne, 2 * d)
        hrow = h2_ref[pl.ds(t, 1), :]
        grow = gate_ref[pl.ds(t, 1), :]
        lane = lax.broadcasted_iota(jnp.int32, (ROW_TILE, ne), 1)
        sub = lax.broadcasted_iota(jnp.int32, (ROW_TILE, ne), 0)
        acc = jnp.zeros((ROW_TILE, d), F32)
        for g in range(ne // ROW_TILE):
            rows = tile[g * ROW_TILE:(g + 1) * ROW_TILE, :]
            pre = jnp.sum(rows[:, :d] * hrow, axis=1, keepdims=True)
            act = 0.5 * pre * (1.0 + lax.erf(pre * (2.0 ** -0.5)))
            gcol = jnp.sum(jnp.where(lane == sub + g * ROW_TILE, grow, 0.0), axis=1, keepdims=True)
            acc = acc + (act * gcol) * rows[:, d:]
        y = jnp.sum(acc, axis=0, keepdims=True)
        xo = x1_ref[pl.ds(t, 1), :] + gt_ref[...] * y
        xo = xo * lax.rsqrt(jnp.mean(xo * xo, axis=-1, keepdims=True) + NORM_EPS) * fg_ref[...]
        o_ref[pl.ds(t, 1), :] = xo

    @pl.when(step == 0)
    def _():
        for g in range(AHEAD):
            for j in range(GROUP):
                issue_token((g * GROUP + j) * ne, g, j)

    def ring_turn(it, carry):
        for gs in range(N_GSLOTS):
            t0 = (it * N_GSLOTS + gs) * GROUP
            wait_group(gs)
            for j in range(GROUP):
                issue_token((t0 + AHEAD * GROUP + j) * ne, (gs + AHEAD) % N_GSLOTS, j)
                evaluate(t0 + j, gs * GROUP + j)
        return carry

    lax.fori_loop(0, tt // N_SLOTS, ring_turn, 0)

    @pl.when(step == last_step)
    def _():
        for g in range(AHEAD):
            wait_group(g)


def peer_expert(idx_tok, h2, gate_tok, x1, gt2, final_g, table, seq, tt):
    ntok, d = h2.shape
    ne = PEER_HEADS * PEER_TOPK
    tiles_per_b = seq // tt
    n_steps = ntok // tt
    assert tt % N_SLOTS == 0
    idx_steps = idx_tok.reshape(n_steps, tt * ne)
    tail = jnp.roll(idx_steps[:, :AHEAD * GROUP * ne], -1, axis=0)
    win = (tt + AHEAD * GROUP) * ne
    idx_win = jnp.concatenate([idx_steps, tail], axis=1).reshape(n_steps * win)
    return pl.pallas_call(
        functools.partial(_peer_expert_kernel, tt=tt, d=d),
        grid=(n_steps,),
        in_specs=[pl.BlockSpec((win,), lambda i: (i,), memory_space=pltpu.SMEM),
                  pl.BlockSpec((tt, d), lambda i: (i, 0)),
                  pl.BlockSpec((tt, ne), lambda i: (i, 0)),
                  pl.BlockSpec((tt, d), lambda i: (i, 0)),
                  pl.BlockSpec((None, 1, d), lambda i: (i // tiles_per_b, 0, 0)),
                  pl.BlockSpec((1, d), lambda i: (0, 0)),
                  pl.BlockSpec(memory_space=pl.ANY)],
        out_specs=pl.BlockSpec((tt, d), lambda i: (i, 0)),
        out_shape=jax.ShapeDtypeStruct((ntok, d), F32),
        scratch_shapes=([pltpu.VMEM((ne, 1, 2 * d), F32) for _ in range(N_SLOTS)]
                        + [pltpu.VMEM((ne, 2 * d), F32) for _ in range(N_SLOTS)]
                        + [pltpu.SemaphoreType.DMA((N_GSLOTS,))]),
        compiler_params=_cparams(("arbitrary",)),
        name="peer_expert",
    )(idx_win, h2, gate_tok, x1, gt2, final_g.reshape(1, d), table)


def kernel(x, c, ctx, c_ctx, ada_w, ada_b, norm1_g, w_in, gdn_conv_w, gdn_a_log, gdn_dt_bias,
           gdn_norm_w, rwkv_mu, rwkv_w0, rwkv_w2, rwkv_a0, rwkv_a2, rwkv_g2, rwkv_k_k, rwkv_k_a,
           rwkv_r_k, rwkv_gn_w, rwkv_gn_b, w_out, norm2_g, peer_w_query, peer_sub_keys,
           peer_down, peer_up, final_norm_g):
    bsz, seq, d = x.shape
    n_ctx = ctx.shape[1]
    assert ada_w.shape[0] == 1, "single-layer block"
    assert n_ctx % 256 == 0 and seq % 256 == 0 and seq % GRID_W == 0
    nh = GDN_HEADS
    n_qkvz = 4 * nh * GDN_DK
    gdn_cols = n_qkvz + 4 * nh

    cond = jnp.concatenate([c, c_ctx[None, :], jnp.zeros((16 - bsz - 1, d), F32)], axis=0)
    mods = ada_mod(cond, ada_w[0], ada_b[0])
    m_lat = mods[:bsz].reshape(bsz, N_MOD, 1, d)
    m_ctx = jnp.broadcast_to(mods[bsz].reshape(1, N_MOD, 1, d), (bsz, N_MOD, 1, d))
    shift1 = jnp.stack([m_ctx[:, 0], m_lat[:, 0]], axis=1)
    scale1 = jnp.stack([m_ctx[:, 1], m_lat[:, 1]], axis=1)
    gt1, sh2, sc2, gt2 = m_lat[:, 2], m_lat[:, 3], m_lat[:, 4], m_lat[:, 5]

    wl = w_in[0]
    w_gdn = jnp.concatenate([wl[:, :gdn_cols], jnp.zeros((d, LANES - 4 * nh), F32)], axis=1)
    w_rwkv = wl[:, gdn_cols:]
    xin = jnp.concatenate([ctx, x], axis=1)
    p_gdn, p_rwkv = in_proj(xin, norm1_g[0], shift1, scale1, w_gdn.astype(BF16),
                            w_rwkv.astype(BF16), n_ctx, 256)

    qkv = gdn_prep(p_gdn, gdn_conv_w[0], n_ctx)
    prm = jnp.zeros((8, LANES), F32)
    prm = prm.at[0, :2 * nh].set(gdn_a_log[0].reshape(-1))
    prm = prm.at[1, :2 * nh].set(gdn_dt_bias[0].reshape(-1))
    o_f, o_b = gdn_chunk(qkv, p_gdn, prm, n_ctx)

    xs = rwkv_shift(p_rwkv, rwkv_mu[0], n_ctx)
    vecs = jnp.zeros((8, RWKV_W), F32)
    for i, vec in enumerate((rwkv_k_k, rwkv_k_a, rwkv_r_k, rwkv_gn_w, rwkv_gn_b)):
        vecs = vecs.at[i].set(vec[0])
    y_f, y_b = rwkv_chunk(xs, vecs, rwkv_w0[0], rwkv_w2[0], rwkv_a0[0], rwkv_a2[0], n_ctx)

    x1 = mix_out(x, gt1, o_f, o_b, p_gdn, y_f, y_b, xs, gdn_norm_w[0].reshape(1, GDN_DK), vecs,
                 rwkv_a0[0], rwkv_a2[0], rwkv_g2[0], w_out[0].astype(BF16), n_ctx, 256)

    h2, idx_t, gate_t = peer_route(x1, norm2_g[0], sh2, sc2, peer_w_query[0].astype(BF16),
                                   peer_sub_keys[0], 256)
    table = jnp.concatenate([peer_down[0], peer_up[0]], axis=1)[:, None, :]
    out = peer_expert(idx_t.T, h2, gate_t.T, x1.reshape(bsz * seq, d), gt2,
                      final_norm_g, table, seq, 128)
    return out.reshape(bsz, seq, d)
```

```python
import functools
import math

import jax
import jax.numpy as jnp
from jax import lax
from jax.experimental import pallas as pl
from jax.experimental.pallas import tpu as pltpu

F32 = jnp.float32
BF16 = jnp.bfloat16

GRID_W = 64
GDN_HEADS = 4
GDN_DK = 128
GDN_CONV = 5
CHUNK = 64
RWKV_HEADS = 8
RWKV_HD = 64
RWKV_W = RWKV_HEADS * RWKV_HD
LORA = 64
PEER_HEADS = 8
PEER_NKEYS = 128
PEER_TOPK = 16
N_MOD = 6

NORM_EPS = 1e-6
L2_EPS = 1e-6
RWKV_GN_EPS = 64e-5

LANES = 128
VMEM_LIMIT = 56 * 1024 * 1024
NEG_INF = float("-inf")


def _cparams(sem):
    return pltpu.CompilerParams(dimension_semantics=sem, vmem_limit_bytes=VMEM_LIMIT)


def _dot(a, b):
    return jnp.dot(a.astype(BF16), b.astype(BF16), preferred_element_type=F32)


def _dot_nt(a, b):
    return lax.dot_general(a.astype(BF16), b.astype(BF16), (((1,), (1,)), ((), ())),
                           preferred_element_type=F32)


def _dot_tn(a, b):
    return lax.dot_general(a.astype(BF16), b.astype(BF16), (((0,), (0,)), ((), ())),
                           preferred_element_type=F32)


def _dot_hi(a, b):
    return jnp.dot(a, b, preferred_element_type=F32, precision=lax.Precision.HIGHEST)


def _split3(x):
    hi = x.astype(BF16)
    rest = x - hi.astype(F32)
    mid = rest.astype(BF16)
    lo = (rest - mid.astype(F32)).astype(BF16)
    return hi, mid, lo


def _dot_x01(a, m01):
    m = m01.astype(BF16)
    return sum(jnp.dot(p, m, preferred_element_type=F32) for p in _split3(a))


def _dot_01x(m01, b):
    m = m01.astype(BF16)
    return sum(jnp.dot(m, p, preferred_element_type=F32) for p in _split3(b))


def _sigmoid(x):
    return 1.0 / (1.0 + jnp.exp(-x))


def _silu(x):
    return x * _sigmoid(x)


def _softplus(x):
    return jnp.maximum(x, 0.0) + jnp.log(1.0 + jnp.exp(-jnp.abs(x)))


def _unit_lower_inverses(xs):
    c = xs[0].shape[0]
    eye = (lax.broadcasted_iota(jnp.int32, (c, c), 0) ==
           lax.broadcasted_iota(jnp.int32, (c, c), 1)).astype(F32)
    prods = [eye + x for x in xs]
    pows = [_dot(x, x) for x in xs]
    for _ in range(int(math.log2(c)) - 2):
        both = [_dot(jnp.concatenate([p, x], axis=0), x) for p, x in zip(prods, pows)]
        prods = [p + b[:c] for p, b in zip(prods, both)]
        pows = [b[c:] for b in both]
    return [p + _dot(p, x) for p, x in zip(prods, pows)]


def _order_masks(c, rev):
    r = lax.broadcasted_iota(jnp.int32, (c, c), 0)
    s = lax.broadcasted_iota(jnp.int32, (c, c), 1)
    if rev:
        return s >= r, s > r
    return s <= r, s < r


def _ada_kernel(c_ref, w_ref, b_ref, o_ref):
    o_ref[...] = _dot_hi(_silu(c_ref[...]), w_ref[...]) + b_ref[...]


def ada_mod(cond, w, b):
    n, d = cond.shape
    cols = w.shape[1]
    tn = 1024
    return pl.pallas_call(
        _ada_kernel,
        grid=(cols // tn,),
        in_specs=[pl.BlockSpec((n, d), lambda j: (0, 0)),
                  pl.BlockSpec((d, tn), lambda j: (0, j)),
                  pl.BlockSpec((1, tn), lambda j: (0, j))],
        out_specs=pl.BlockSpec((n, tn), lambda j: (0, j)),
        out_shape=jax.ShapeDtypeStruct((n, cols), F32),
        compiler_params=_cparams(("arbitrary",)),
        name="ada_mod",
    )(cond, w, b.reshape(1, cols))


def _in_proj_kernel(x_ref, g_ref, sh_ref, sc_ref, wg_ref, wr_ref, pg_ref, pr_ref):
    x = x_ref[...]
    y = x * lax.rsqrt(jnp.mean(x * x, axis=-1, keepdims=True) + NORM_EPS) * g_ref[...]
    h = (y * (1.0 + sc_ref[...]) + sh_ref[...]).astype(BF16)
    pg_ref[...] = jnp.dot(h, wg_ref[...], preferred_element_type=F32)
    pr_ref[...] = jnp.dot(h, wr_ref[...], preferred_element_type=F32)


def in_proj(xin, norm_g, shift, scale, w_gdn, w_rwkv, n_ctx, tm):
    bsz, t, d = xin.shape
    cg, cr = w_gdn.shape[1], w_rwkv.shape[1]
    nctx_tiles = n_ctx // tm

    def mod_map(b, i):
        return (b, jnp.where(i < nctx_tiles, 0, 1), 0, 0)

    return pl.pallas_call(
        _in_proj_kernel,
        grid=(bsz, t // tm),
        in_specs=[pl.BlockSpec((None, tm, d), lambda b, i: (b, i, 0)),
                  pl.BlockSpec((1, d), lambda b, i: (0, 0)),
                  pl.BlockSpec((None, None, 1, d), mod_map),
                  pl.BlockSpec((None, None, 1, d), mod_map),
                  pl.BlockSpec((d, cg), lambda b, i: (0, 0)),
                  pl.BlockSpec((d, cr), lambda b, i: (0, 0))],
        out_specs=[pl.BlockSpec((None, tm, cg), lambda b, i: (b, i, 0)),
                   pl.BlockSpec((None, tm, cr), lambda b, i: (b, i, 0))],
        out_shape=[jax.ShapeDtypeStruct((bsz, t, cg), F32),
                   jax.ShapeDtypeStruct((bsz, t, cr), F32)],
        compiler_params=_cparams(("parallel", "parallel")),
        name="in_proj",
    )(xin, norm_g.reshape(1, d), shift, scale, w_gdn, w_rwkv)


def _gdn_prep_kernel(p_ref, w_ref, o_ref, *, n_ctx):
    j = pl.program_id(1)
    x = p_ref[...]
    t_len = x.shape[0]
    w = w_ref[...]
    t = lax.broadcasted_iota(jnp.int32, x.shape, 0)
    is_ctx = t < n_ctx
    lo = jnp.where(is_ctx, 0, n_ctx)
    hi = jnp.where(is_ctx, n_ctx, t_len)
    pad = (GDN_CONV - 1) // 2
    acc = x * w[pad:pad + 1, :]
    for s in range(-pad, pad + 1):
        if s == 0:
            continue
        xs = pltpu.roll(x, (-s) % t_len, 0)
        valid = (t + s >= lo) & (t + s < hi)
        acc = acc + jnp.where(valid, xs, 0.0) * w[s + pad:s + pad + 1, :]
    y = _silu(acc)
    inv = lax.rsqrt(jnp.sum(y * y, axis=-1, keepdims=True) + L2_EPS)
    fac = jnp.where(j < GDN_HEADS, inv * (GDN_DK ** -0.5), jnp.where(j < 2 * GDN_HEADS, inv, 1.0))
    o_ref[...] = y * fac


def gdn_prep(p_gdn, conv_w, n_ctx):
    bsz, t, _ = p_gdn.shape
    ncol = conv_w.shape[1]
    return pl.pallas_call(
        functools.partial(_gdn_prep_kernel, n_ctx=n_ctx),
        grid=(bsz, ncol // LANES),
        in_specs=[pl.BlockSpec((None, t, LANES), lambda b, j: (b, 0, j)),
                  pl.BlockSpec((GDN_CONV, LANES), lambda b, j: (0, j))],
        out_specs=pl.BlockSpec((None, t, LANES), lambda b, j: (b, 0, j)),
        out_shape=jax.ShapeDtypeStruct((bsz, t, ncol), F32),
        compiler_params=_cparams(("parallel", "parallel")),
        name="gdn_prep",
    )(p_gdn, conv_w)


def _gdn_chunk_kernel(qf_ref, qb_ref, gf_ref, gb_ref, prm_ref, of_ref, ob_ref, s_ref):
    @pl.when(pl.program_id(1) == 0)
    def _():
        s_ref[...] = jnp.zeros_like(s_ref)

    nh, dk = GDN_HEADS, GDN_DK
    nb = qf_ref.shape[0]
    prm = prm_ref[...]
    a_log, dt_bias = prm[0:1, :], prm[1:2, :]
    q, k, v, beta, g_col, g_last, gamma, strict = [], [], [], [], [], [], [], []
    for bi, d in [(bi, d) for bi in range(nb) for d in range(2)]:
        qkv = (qf_ref, qb_ref)[d][bi]
        gates = (gf_ref, gb_ref)[d][bi]
        c = qkv.shape[0]
        incl, strict_d = _order_masks(c, d == 1)
        g_all = -jnp.exp(a_log) * _softplus(gates + dt_bias)
        beta_all = _sigmoid(gates)
        big_g = _dot_01x(incl.astype(F32), g_all)
        big_g_t = big_g.T
        last = 0 if d == 1 else c - 1
        for h in range(nh):
            col = d * nh + h
            q.append(qkv[:, h * dk:(h + 1) * dk])
            k.append(qkv[:, (nh + h) * dk:(nh + h + 1) * dk])
            v.append(qkv[:, (2 * nh + h) * dk:(2 * nh + h + 1) * dk])
            beta.append(beta_all[:, 2 * nh + col:2 * nh + col + 1])
            g_col.append(big_g[:, col:col + 1])
            g_last.append(big_g[last:last + 1, col:col + 1])
            g_row = big_g_t[col:col + 1, :]
            gamma.append(jnp.where(incl, jnp.exp(jnp.where(incl, g_col[-1] - g_row, 0.0)), 0.0))
            strict.append(strict_d)
    n = len(q)
    c = q[0].shape[0]
    e_g = [jnp.exp(g) for g in g_col]
    kq = [_dot_nt(jnp.concatenate([k[i], q[i]], axis=0), k[i]) for i in range(n)]
    a_mat = [jnp.where(strict[i], beta[i] * kq[i][:c] * gamma[i], 0.0) for i in range(n)]
    qk = [kq[i][c:] * gamma[i] for i in range(n)]
    t_inv = _unit_lower_inverses([-a for a in a_mat])
    uw = [_dot(t_inv[i], jnp.concatenate([beta[i] * v[i], (beta[i] * e_g[i]) * k[i]], axis=1))
          for i in range(n)]
    s = [s_ref[i // nh, i % nh] for i in range(n)]
    ws = [_dot(jnp.concatenate([uw[i][:, dk:], q[i] * e_g[i]], axis=0), s[i]) for i in range(n)]
    v_new = [uw[i][:, :dk] - ws[i][:c] for i in range(n)]
    o = [ws[i][c:] + _dot(qk[i], v_new[i]) for i in range(n)]
    s_new = [s[i] * jnp.exp(g_last[i]) + _dot_tn(k[i] * jnp.exp(g_last[i] - g_col[i]), v_new[i])
             for i in range(n)]
    for i in range(n):
        s_ref[i // nh, i % nh] = s_new[i]
    for bi in range(nb):
        base = 2 * nh * bi
        of_ref[bi] = jnp.concatenate(o[base:base + nh], axis=-1)
        ob_ref[bi] = jnp.concatenate(o[base + nh:base + 2 * nh], axis=-1)


BATCH_PER_STEP = 2


def _rev_chunk_map(n_ctx_chunks, n_chunks):
    def cb(s):
        return jnp.where(s < n_ctx_chunks, n_ctx_chunks - 1 - s, n_chunks + n_ctx_chunks - 1 - s)
    return cb


def gdn_chunk(qkv, p_gdn, prm, n_ctx):
    bsz, t, ncol = qkv.shape
    nv = GDN_HEADS * GDN_DK
    n_chunks = t // CHUNK
    cb = _rev_chunk_map(n_ctx // CHUNK, n_chunks)
    gate_tile = p_gdn.shape[2] // LANES - 1
    nb = BATCH_PER_STEP if bsz % BATCH_PER_STEP == 0 else 1
    return pl.pallas_call(
        _gdn_chunk_kernel,
        grid=(bsz // nb, n_chunks),
        in_specs=[pl.BlockSpec((nb, CHUNK, ncol), lambda b, s: (b, s, 0)),
                  pl.BlockSpec((nb, CHUNK, ncol), lambda b, s: (b, cb(s), 0)),
                  pl.BlockSpec((nb, CHUNK, LANES), lambda b, s: (b, s, gate_tile)),
                  pl.BlockSpec((nb, CHUNK, LANES), lambda b, s: (b, cb(s), gate_tile)),
                  pl.BlockSpec((8, LANES), lambda b, s: (0, 0))],
        out_specs=[pl.BlockSpec((nb, CHUNK, nv), lambda b, s: (b, s, 0)),
                   pl.BlockSpec((nb, CHUNK, nv), lambda b, s: (b, cb(s), 0))],
        out_shape=[jax.ShapeDtypeStruct((bsz, t, nv), F32)] * 2,
        scratch_shapes=[pltpu.VMEM((2 * nb, GDN_HEADS, GDN_DK, GDN_DK), F32)],
        compiler_params=_cparams(("parallel", "arbitrary")),
        name="gdn_chunk",
    )(qkv, qkv, p_gdn, p_gdn, prm)


def _rwkv_shift_kernel(p_ref, mu_ref, o_ref, *, n_ctx, n_cols):
    j = pl.program_id(1)
    x = p_ref[...]
    t_len = x.shape[0]
    t = lax.broadcasted_iota(jnp.int32, x.shape, 0)
    ch = lax.broadcasted_iota(jnp.int32, x.shape, 1) + j * LANES
    is_ctx = t < n_ctx
    col = (t - n_ctx) % GRID_W
    prev1 = pltpu.roll(x, 1, 0)
    next1 = pltpu.roll(x, t_len - 1, 0)
    up = pltpu.roll(x, GRID_W, 0)
    down = pltpu.roll(x, t_len - GRID_W, 0)
    quarter = n_cols // 4
    half = n_cols // 2
    ctx_sh = jnp.where(ch < half,
                       jnp.where(t >= 1, prev1, 0.0),
                       jnp.where(t < n_ctx - 1, next1, 0.0))
    lat_sh = jnp.where(ch < quarter, jnp.where(col >= 1, prev1, 0.0),
                       jnp.where(ch < 2 * quarter, jnp.where(col < GRID_W - 1, next1, 0.0),
                                 jnp.where(ch < 3 * quarter,
                                           jnp.where(t >= n_ctx + GRID_W, up, 0.0),
                                           jnp.where(t < t_len - GRID_W, down, 0.0))))
    sh = jnp.where(is_ctx, ctx_sh, lat_sh)
    o_ref[...] = x + mu_ref[...] * (sh - x)


def rwkv_shift(p_rwkv, mu, n_ctx):
    bsz, t, ncol = p_rwkv.shape
    return pl.pallas_call(
        functools.partial(_rwkv_shift_kernel, n_ctx=n_ctx, n_cols=ncol),
        grid=(bsz, ncol // LANES),
        in_specs=[pl.BlockSpec((None, t, LANES), lambda b, j: (b, 0, j)),
                  pl.BlockSpec((1, LANES), lambda b, j: (0, j))],
        out_specs=pl.BlockSpec((None, t, LANES), lambda b, j: (b, 0, j)),
        out_shape=jax.ShapeDtypeStruct((bsz, t, ncol), F32),
        compiler_params=_cparams(("parallel", "parallel")),
        name="rwkv_shift",
    )(p_rwkv, mu.reshape(1, ncol))


def _rwkv_gates(xs, d, w0_ref, w2_ref, a0_ref, a2_ref):
    w_ = RWKV_W
    xw = xs[:, 3 * w_ + d * LORA:3 * w_ + (d + 1) * LORA]
    xa = xs[:, 3 * w_ + 2 * LORA + d * LORA:3 * w_ + 2 * LORA + (d + 1) * LORA]
    w_pre = w0_ref[d:d + 1, :] + _dot(jnp.tanh(xw), w2_ref[d])
    log_w = -math.exp(-0.5) * _sigmoid(w_pre)
    a = _sigmoid(a0_ref[d:d + 1, :] + _dot(xa, a2_ref[d]))
    return log_w, a


def _rwkv_chunk_kernel(xf_ref, xb_ref, vec_ref, w0_ref, w2_ref, a0_ref, a2_ref, hsum_ref,
                       yf_ref, yb_ref, p_ref):
    @pl.when(pl.program_id(1) == 0)
    def _():
        p_ref[...] = jnp.zeros_like(p_ref)

    w_, n, nh = RWKV_W, RWKV_HD, RWKV_HEADS
    nb = xf_ref.shape[0]
    k_k, k_a = vec_ref[0:1, :], vec_ref[1:2, :]
    lhs, rhs_t, tail, vals, w_last, p0, strict, incl = [], [], [], [], [], [], [], []
    for bi, d in [(bi, d) for bi in range(nb) for d in range(2)]:
        xs = (xf_ref, xb_ref)[d][bi]
        c = xs.shape[0]
        incl_d, strict_d = _order_masks(c, d == 1)
        r = xs[:, 0:w_]
        k = xs[:, w_:2 * w_]
        v = xs[:, 2 * w_:3 * w_]
        log_w, a = _rwkv_gates(xs, d, w0_ref, w2_ref, a0_ref, a2_ref)
        kk = k * k_k
        kk = kk * lax.rsqrt(_dot_x01(kk * kk, hsum_ref[...]) + L2_EPS)
        k_dir = k * (1.0 + (a - 1.0) * k_a)
        alpha = -(kk * a)
        big_g = _dot_01x(incl_d.astype(F32), log_w)
        last = 0 if d == 1 else c - 1
        g_last = big_g[last:last + 1, :]
        e_neg = jnp.exp(-big_g)
        e_tail = jnp.exp(g_last - big_g)
        r_t = r * jnp.exp(big_g)
        b_t = kk * jnp.exp(big_g - log_w)
        a_t = alpha * e_neg
        k_t = k_dir * e_neg
        a_h = alpha * e_tail
        k_h = k_dir * e_tail
        w_l = jnp.exp(g_last)
        for h in range(nh):
            sl = slice(h * n, (h + 1) * n)
            lhs.append(jnp.concatenate([b_t[:, sl], r_t[:, sl]], axis=0))
            rhs_t.append(jnp.concatenate([a_t[:, sl], k_t[:, sl]], axis=0))
            tail.append(jnp.concatenate([a_h[:, sl], k_h[:, sl]], axis=0))
            vals.append(v[:, sl])
            w_last.append(w_l[:, sl])
            p0.append(p_ref[2 * bi + d, h])
            strict.append(strict_d)
            incl.append(incl_d)
    m = len(lhs)
    c = vals[0].shape[0]
    quad = [_dot_nt(lhs[i], rhs_t[i]) for i in range(m)]
    a_ba = [jnp.where(strict[i], quad[i][:c, :c], 0.0) for i in range(m)]
    a_k = [jnp.concatenate([jnp.where(strict[i], quad[i][:c, c:], 0.0),
                            jnp.where(incl[i], quad[i][c:, c:], 0.0)], axis=0) for i in range(m)]
    a_ra = [jnp.where(incl[i], quad[i][c:, :c], 0.0) for i in range(m)]
    from_state = [_dot_nt(lhs[i], p0[i]) for i in range(m)]
    from_vals = [_dot(a_k[i], vals[i]) for i in range(m)]
    t_inv = _unit_lower_inverses(a_ba)
    u = [_dot(t_inv[i], from_state[i][:c] + from_vals[i][:c]) for i in range(m)]
    y = [from_state[i][c:] + _dot(a_ra[i], u[i]) + from_vals[i][c:] for i in range(m)]
    p_new = [p0[i] * w_last[i] + _dot_tn(jnp.concatenate([u[i], vals[i]], axis=0), tail[i])
             for i in range(m)]
    for i in range(m):
        p_ref[i // nh, i % nh] = p_new[i]
    for bi in range(nb):
        base = 2 * nh * bi
        yf_ref[bi] = jnp.concatenate(y[base:base + nh], axis=-1)
        yb_ref[bi] = jnp.concatenate(y[base + nh:base + 2 * nh], axis=-1)


def _head_sum_matrix(width, group):
    i = jnp.arange(width)
    return (i[:, None] // group == i[None, :] // group).astype(F32)


def rwkv_chunk(xs, vecs, w0, w2, a0, a2, n_ctx):
    bsz, t, ncol = xs.shape
    n_chunks = t // CHUNK
    cb = _rev_chunk_map(n_ctx // CHUNK, n_chunks)
    hsum = _head_sum_matrix(RWKV_W, RWKV_HD)
    full = lambda *shape: pl.BlockSpec(shape, lambda b, s: (0,) * len(shape))
    nb = BATCH_PER_STEP if bsz % BATCH_PER_STEP == 0 else 1
    return pl.pallas_call(
        _rwkv_chunk_kernel,
        grid=(bsz // nb, n_chunks),
        in_specs=[pl.BlockSpec((nb, CHUNK, ncol), lambda b, s: (b, s, 0)),
                  pl.BlockSpec((nb, CHUNK, ncol), lambda b, s: (b, cb(s), 0)),
                  full(8, RWKV_W), full(2, RWKV_W), full(2, LORA, RWKV_W),
                  full(2, RWKV_W), full(2, LORA, RWKV_W), full(RWKV_W, RWKV_W)],
        out_specs=[pl.BlockSpec((nb, CHUNK, RWKV_W), lambda b, s: (b, s, 0)),
                   pl.BlockSpec((nb, CHUNK, RWKV_W), lambda b, s: (b, cb(s), 0))],
        out_shape=[jax.ShapeDtypeStruct((bsz, t, RWKV_W), F32)] * 2,
        scratch_shapes=[pltpu.VMEM((2 * nb, RWKV_HEADS, RWKV_HD, RWKV_HD), F32)],
        compiler_params=_cparams(("parallel", "arbitrary")),
        name="rwkv_chunk",
    )(xs, xs, vecs, w0, w2, a0, a2, hsum)


def _mix_out_kernel(x_ref, gt_ref, of_ref, ob_ref, z_ref, yf_ref, yb_ref, xs_ref,
                    gnw_ref, vec_ref, a0_ref, a2_ref, g2_ref, hsum_ref, wo_ref, o_ref):
    nh, dv = GDN_HEADS, GDN_DK
    w_ = RWKV_W
    o = of_ref[...] + ob_ref[...]
    z = z_ref[...]
    parts = []
    for h in range(nh):
        oh = o[:, h * dv:(h + 1) * dv]
        oh = oh * lax.rsqrt(jnp.mean(oh * oh, axis=-1, keepdims=True) + NORM_EPS)
        parts.append(oh * gnw_ref[...] * _silu(z[:, h * dv:(h + 1) * dv]))
    xs = xs_ref[...]
    r = xs[:, 0:w_]
    k = xs[:, w_:2 * w_]
    v = xs[:, 2 * w_:3 * w_]
    xg = xs[:, 3 * w_ + 4 * LORA:]
    k_a, r_k = vec_ref[1:2, :], vec_ref[2:3, :]
    gn_w, gn_b = vec_ref[3:4, :], vec_ref[4:5, :]
    hsum = hsum_ref[...]
    y = yf_ref[...] + yb_ref[...]
    mean = _dot_x01(y, hsum) * (1.0 / RWKV_HD)
    yc = y - mean
    var = _dot_x01(yc * yc, hsum) * (1.0 / RWKV_HD)
    yn = yc * lax.rsqrt(var + RWKV_GN_EPS) * gn_w + gn_b
    rk = jnp.zeros_like(r)
    for d in range(2):
        xa = xs[:, 3 * w_ + 2 * LORA + d * LORA:3 * w_ + 2 * LORA + (d + 1) * LORA]
        a = _sigmoid(a0_ref[d:d + 1, :] + _dot(xa, a2_ref[d]))
        rk = rk + r * (k * (1.0 + (a - 1.0) * k_a)) * r_k
    bonus = _dot_x01(rk, hsum) * v
    gate = _dot(_sigmoid(xg), g2_ref[...])
    parts.append((yn + bonus) * gate)
    mixed = jnp.concatenate(parts, axis=-1).astype(BF16)
    o_ref[...] = x_ref[...] + gt_ref[...] * jnp.dot(mixed, wo_ref[...], preferred_element_type=F32)


def mix_out(x, gt1, o_f, o_b, p_gdn, y_f, y_b, xs, gnw, vecs, a0, a2, g2, w_out, n_ctx, tm):
    bsz, seq, d = x.shape
    off = n_ctx // tm
    nv = GDN_HEADS * GDN_DK
    ncol = xs.shape[2]
    hsum = _head_sum_matrix(RWKV_W, RWKV_HD)
    lat = lambda width, cblk=0: pl.BlockSpec((None, tm, width), lambda b, i: (b, i + off, cblk))
    full = lambda *shape: pl.BlockSpec(shape, lambda b, i: (0,) * len(shape))
    return pl.pallas_call(
        _mix_out_kernel,
        grid=(bsz, seq // tm),
        in_specs=[pl.BlockSpec((None, tm, d), lambda b, i: (b, i, 0)),
                  pl.BlockSpec((None, 1, d), lambda b, i: (b, 0, 0)),
                  lat(nv), lat(nv), lat(nv, 3), lat(RWKV_W), lat(RWKV_W), lat(ncol),
                  full(1, GDN_DK), full(8, RWKV_W), full(2, RWKV_W), full(2, LORA, RWKV_W),
                  full(2 * LORA, RWKV_W), full(RWKV_W, RWKV_W), full(nv + RWKV_W, d)],
        out_specs=pl.BlockSpec((None, tm, d), lambda b, i: (b, i, 0)),
        out_shape=jax.ShapeDtypeStruct((bsz, seq, d), F32),
        compiler_params=_cparams(("parallel", "parallel")),
        name="mix_out",
    )(x, gt1, o_f, o_b, p_gdn, y_f, y_b, xs, gnw, vecs, a0, a2, g2, hsum, w_out)


def _extract_top(s, riota, n_rows, vals_ref, idx_ref, k):
    for i in range(k):
        m = jnp.max(s, axis=0, keepdims=True)
        am = jnp.min(jnp.where(s == m, riota, float(n_rows)), axis=0, keepdims=True)
        vals_ref[i:i + 1, :] = m
        idx_ref[i:i + 1, :] = am
        s = jnp.where(riota == am, NEG_INF, s)


def _candidate_blocks(k):
    split = math.isqrt(k)
    blocks = [(False, i, 0, k // (i + 1)) for i in range(split)]
    j = 0
    while k // (j + 1) > split:
        blocks.append((True, j, split, k // (j + 1)))
        j += 1
    return blocks


def _candidate_rows(k):
    return sum(-(-(hi - lo) // 8) * 8 for _, _, lo, hi in _candidate_blocks(k))


def _top_list_rows(k):
    need = max(lo + -(-(hi - lo) // 8) * 8 for _, _, lo, hi in _candidate_blocks(k))
    return -(-need // 8) * 8


def _peer_route_kernel(x_ref, g_ref, sh_ref, sc_ref, wq_ref, keys_ref,
                       h2_ref, idx_ref, gate_ref,
                       q_scr, tv_scr, ti_scr, cs_scr, ci_scr, cf_scr, bv_scr, bi_scr):
    h = pl.program_id(1)
    dq = 2 * LANES
    kk = PEER_TOPK

    @pl.when(h == 0)
    def _():
        x = x_ref[...]
        y = x * lax.rsqrt(jnp.mean(x * x, axis=-1, keepdims=True) + NORM_EPS) * g_ref[...]
        h2 = y * (1.0 + sc_ref[...]) + sh_ref[...]
        h2_ref[...] = h2
        hb = h2.astype(BF16)
        for hh in range(PEER_HEADS):
            q_scr[hh] = jnp.dot(hb, wq_ref[:, hh * dq:(hh + 1) * dq], preferred_element_type=F32)

    q = q_scr[h]
    tm = q.shape[0]
    riota = lax.broadcasted_iota(jnp.int32, (PEER_NKEYS, tm), 0).astype(F32)
    for p in range(2):
        s_t = _dot_nt(keys_ref[p], q[:, p * LANES:(p + 1) * LANES])
        _extract_top(s_t, riota, PEER_NKEYS, tv_scr.at[p], ti_scr.at[p], kk)
    tv_scr[:, kk:, :] = jnp.zeros_like(tv_scr[:, kk:, :])
    ti_scr[:, kk:, :] = jnp.zeros_like(ti_scr[:, kk:, :])
    row = 0
    for fixed_j, fixed, lo, hi in _candidate_blocks(kk):
        n_pad = -(-(hi - lo) // 8) * 8
        r = lax.broadcasted_iota(jnp.int32, (n_pad, tm), 0).astype(F32)
        run, one = (0, 1) if fixed_j else (1, 0)
        vals = tv_scr[run, lo:lo + n_pad, :] + tv_scr[one, fixed:fixed + 1, :]
        run_i, one_i = ti_scr[run, lo:lo + n_pad, :], ti_scr[one, fixed:fixed + 1, :]
        if fixed_j:
            experts = run_i * float(PEER_NKEYS) + one_i
            flat = (r + float(lo)) * float(kk) + float(fixed)
        else:
            experts = one_i * float(PEER_NKEYS) + run_i
            flat = float(fixed * kk + lo) + r
        cs_scr[row:row + n_pad, :] = jnp.where(r < float(hi - lo), vals, NEG_INF)
        ci_scr[row:row + n_pad, :] = experts
        cf_scr[row:row + n_pad, :] = flat
        row += n_pad
    cand = cs_scr[...]
    cand_i = ci_scr[...]
    ciota = cf_scr[...]
    for i in range(kk):
        m = jnp.max(cand, axis=0, keepdims=True)
        pos = jnp.min(jnp.where(cand == m, ciota, float(kk * kk)), axis=0, keepdims=True)
        hit = ciota == pos
        bv_scr[i:i + 1, :] = m
        bi_scr[i:i + 1, :] = jnp.sum(jnp.where(hit, cand_i, 0.0), axis=0, keepdims=True)
        cand = jnp.where(hit, NEG_INF, cand)
    best = bv_scr[...]
    e = jnp.exp(best - best[0:1, :])
    gate_ref[...] = e / jnp.sum(e, axis=0, keepdims=True)
    idx_ref[...] = bi_scr[...].astype(jnp.int32)


def peer_route(x1, norm_g, sh2, sc2, wq, sub_keys, tm):
    bsz, seq, d = x1.shape
    ntok = bsz * seq
    tiles_per_b = seq // tm
    kk = PEER_TOPK
    nq = wq.shape[1]
    return pl.pallas_call(
        _peer_route_kernel,
        grid=(ntok // tm, PEER_HEADS),
        in_specs=[pl.BlockSpec((tm, d), lambda i, h: (i, 0)),
                  pl.BlockSpec((1, d), lambda i, h: (0, 0)),
                  pl.BlockSpec((None, 1, d), lambda i, h: (i // tiles_per_b, 0, 0)),
                  pl.BlockSpec((None, 1, d), lambda i, h: (i // tiles_per_b, 0, 0)),
                  pl.BlockSpec((d, nq), lambda i, h: (0, 0)),
                  pl.BlockSpec((None, 2, PEER_NKEYS, LANES), lambda i, h: (h, 0, 0, 0))],
        out_specs=[pl.BlockSpec((tm, d), lambda i, h: (i, 0)),
                   pl.BlockSpec((kk, tm), lambda i, h: (h, i)),
                   pl.BlockSpec((kk, tm), lambda i, h: (h, i))],
        out_shape=[jax.ShapeDtypeStruct((ntok, d), F32),
                   jax.ShapeDtypeStruct((PEER_HEADS * kk, ntok), jnp.int32),
                   jax.ShapeDtypeStruct((PEER_HEADS * kk, ntok), F32)],
        scratch_shapes=[pltpu.VMEM((PEER_HEADS, tm, 2 * LANES), F32),
                        pltpu.VMEM((2, _top_list_rows(kk), tm), F32),
                        pltpu.VMEM((2, _top_list_rows(kk), tm), F32),
                        pltpu.VMEM((_candidate_rows(kk), tm), F32),
                        pltpu.VMEM((_candidate_rows(kk), tm), F32),
                        pltpu.VMEM((_candidate_rows(kk), tm), F32),
                        pltpu.VMEM((kk, tm), F32), pltpu.VMEM((kk, tm), F32)],
        compiler_params=_cparams(("parallel", "arbitrary")),
        name="peer_route",
    )(x1.reshape(ntok, d), norm_g.reshape(1, d), sh2, sc2, wq, sub_keys)


GROUP = 4
N_GSLOTS = 4
AHEAD = 2
N_SLOTS = GROUP * N_GSLOTS
ROW_TILE = 8
DMA_QUEUES = 2


def _peer_expert_kernel(idx_ref, h2_ref, gate_ref, x1_ref, gt_ref, fg_ref, tbl_ref,
                        o_ref, *scratch, tt, d):
    bufs, sem = scratch[:N_SLOTS], scratch[N_SLOTS]
    ne = PEER_HEADS * PEER_TOPK
    step = pl.program_id(0)
    last_step = pl.num_programs(0) - 1

    def issue_token(base, gslot, j):
        slot = gslot * GROUP + j
        for k in range(ne):
            e = idx_ref[base + k]
            pltpu.make_async_copy(tbl_ref.at[e], bufs[slot].at[pl.ds(k, 1), :],
                                  sem.at[gslot]).start(priority=k % DMA_QUEUES)

    def wait_group(gslot):
        for j in range(GROUP):
            pltpu.make_async_copy(tbl_ref.at[pl.ds(0, ne), 0], bufs[gslot * GROUP + j],
                                  sem.at[gslot]).wait()

    def evaluate(t, slot):
        buf = bufs[slot]
        n_grp = ne // ROW_TILE
        n_lt = d // LANES

        def rows(g, j):
            return buf[g * ROW_TILE:(g + 1) * ROW_TILE, j * LANES:(j + 1) * LANES]

        hrow = h2_ref[pl.ds(t, 1), :]
        lane = lax.broadcasted_iota(jnp.int32, (ROW_TILE, LANES), 1)
        pre = jnp.zeros((ROW_TILE, LANES), F32)
        for g in range(n_grp):
            part = rows(g, 0) * hrow[:, 0:LANES]
            for j in range(1, n_lt):
                part = part + rows(g, j) * hrow[:, j * LANES:(j + 1) * LANES]
            pre = jnp.where(lane == g, jnp.sum(part, axis=1, keepdims=True), pre)
        act = 0.5 * pre * (1.0 + lax.erf(pre * (2.0 ** -0.5)))
        coef = act[:, :n_grp] * gate_ref[pl.ds(pl.multiple_of(t * ROW_TILE, ROW_TILE), ROW_TILE), :]
        acc = [jnp.zeros((ROW_TILE, LANES), F32) for _ in range(n_lt)]
        for g in range(n_grp):
            c = coef[:, g:g + 1]
            for j in range(n_lt):
                acc[j] = acc[j] + c * rows(g, n_lt + j)
        y = jnp.concatenate([jnp.sum(a, axis=0, keepdims=True) for a in acc], axis=1)
        xo = x1_ref[pl.ds(t, 1), :] + gt_ref[...] * y
        xo = xo * lax.rsqrt(jnp.mean(xo * xo, axis=-1, keepdims=True) + NORM_EPS) * fg_ref[...]
        o_ref[pl.ds(t, 1), :] = xo

    @pl.when(step == 0)
    def _():
        for g in range(AHEAD):
            for j in range(GROUP):
                issue_token((g * GROUP + j) * ne, g, j)

    def ring_turn(it, carry):
        for gs in range(N_GSLOTS):
            t0 = (it * N_GSLOTS + gs) * GROUP
            wait_group(gs)
            for j in range(GROUP):
                issue_token((t0 + AHEAD * GROUP + j) * ne, (gs + AHEAD) % N_GSLOTS, j)
                evaluate(t0 + j, gs * GROUP + j)
        return carry

    lax.fori_loop(0, tt // N_SLOTS, ring_turn, 0)

    @pl.when(step == last_step)
    def _():
        for g in range(AHEAD):
            wait_group(g)


def peer_expert(idx_tok, h2, gate_t, x1, gt2, final_g, table, seq, tt):
    ntok, d = h2.shape
    ne = PEER_HEADS * PEER_TOPK
    n_grp = ne // ROW_TILE
    tiles_per_b = seq // tt
    n_steps = ntok // tt
    assert tt % N_SLOTS == 0
    gate_sg = gate_t.reshape(n_grp, ROW_TILE, ntok).transpose(2, 1, 0).reshape(ntok * ROW_TILE, n_grp)
    idx_steps = idx_tok.reshape(n_steps, tt * ne)
    tail = jnp.roll(idx_steps[:, :AHEAD * GROUP * ne], -1, axis=0)
    win = (tt + AHEAD * GROUP) * ne
    idx_win = jnp.concatenate([idx_steps, tail], axis=1).reshape(n_steps * win)
    return pl.pallas_call(
        functools.partial(_peer_expert_kernel, tt=tt, d=d),
        grid=(n_steps,),
        in_specs=[pl.BlockSpec((win,), lambda i: (i,), memory_space=pltpu.SMEM),
                  pl.BlockSpec((tt, d), lambda i: (i, 0)),
                  pl.BlockSpec((tt * ROW_TILE, n_grp), lambda i: (i, 0)),
                  pl.BlockSpec((tt, d), lambda i: (i, 0)),
                  pl.BlockSpec((None, 1, d), lambda i: (i // tiles_per_b, 0, 0)),
                  pl.BlockSpec((1, d), lambda i: (0, 0)),
                  pl.BlockSpec(memory_space=pl.ANY)],
        out_specs=pl.BlockSpec((tt, d), lambda i: (i, 0)),
        out_shape=jax.ShapeDtypeStruct((ntok, d), F32),
        scratch_shapes=([pltpu.VMEM((ne, 2 * d), F32) for _ in range(N_SLOTS)]
                        + [pltpu.SemaphoreType.DMA((N_GSLOTS,))]),
        compiler_params=_cparams(("arbitrary",)),
        name="peer_expert",
    )(idx_win, h2, gate_sg, x1, gt2, final_g.reshape(1, d), table)


def kernel(x, c, ctx, c_ctx, ada_w, ada_b, norm1_g, w_in, gdn_conv_w, gdn_a_log, gdn_dt_bias,
           gdn_norm_w, rwkv_mu, rwkv_w0, rwkv_w2, rwkv_a0, rwkv_a2, rwkv_g2, rwkv_k_k, rwkv_k_a,
           rwkv_r_k, rwkv_gn_w, rwkv_gn_b, w_out, norm2_g, peer_w_query, peer_sub_keys,
           peer_down, peer_up, final_norm_g):
    bsz, seq, d = x.shape
    n_ctx = ctx.shape[1]
    assert ada_w.shape[0] == 1, "single-layer block"
    assert n_ctx % 256 == 0 and seq % 256 == 0 and seq % GRID_W == 0
    nh = GDN_HEADS
    n_qkvz = 4 * nh * GDN_DK
    gdn_cols = n_qkvz + 4 * nh

    cond = jnp.concatenate([c, c_ctx[None, :], jnp.zeros((16 - bsz - 1, d), F32)], axis=0)
    mods = ada_mod(cond, ada_w[0], ada_b[0])
    m_lat = mods[:bsz].reshape(bsz, N_MOD, 1, d)
    m_ctx = jnp.broadcast_to(mods[bsz].reshape(1, N_MOD, 1, d), (bsz, N_MOD, 1, d))
    shift1 = jnp.stack([m_ctx[:, 0], m_lat[:, 0]], axis=1)
    scale1 = jnp.stack([m_ctx[:, 1], m_lat[:, 1]], axis=1)
    gt1, sh2, sc2, gt2 = m_lat[:, 2], m_lat[:, 3], m_lat[:, 4], m_lat[:, 5]

    wl = w_in[0]
    w_gdn = jnp.concatenate([wl[:, :gdn_cols], jnp.zeros((d, LANES - 4 * nh), F32)], axis=1)
    w_rwkv = wl[:, gdn_cols:]
    xin = jnp.concatenate([ctx, x], axis=1)
    p_gdn, p_rwkv = in_proj(xin, norm1_g[0], shift1, scale1, w_gdn.astype(BF16),
                            w_rwkv.astype(BF16), n_ctx, 256)

    qkv = gdn_prep(p_gdn, gdn_conv_w[0], n_ctx)
    prm = jnp.zeros((8, LANES), F32)
    prm = prm.at[0, :2 * nh].set(gdn_a_log[0].reshape(-1))
    prm = prm.at[1, :2 * nh].set(gdn_dt_bias[0].reshape(-1))
    o_f, o_b = gdn_chunk(qkv, p_gdn, prm, n_ctx)

    xs = rwkv_shift(p_rwkv, rwkv_mu[0], n_ctx)
    vecs = jnp.zeros((8, RWKV_W), F32)
    for i, vec in enumerate((rwkv_k_k, rwkv_k_a, rwkv_r_k, rwkv_gn_w, rwkv_gn_b)):
        vecs = vecs.at[i].set(vec[0])
    y_f, y_b = rwkv_chunk(xs, vecs, rwkv_w0[0], rwkv_w2[0], rwkv_a0[0], rwkv_a2[0], n_ctx)

    x1 = mix_out(x, gt1, o_f, o_b, p_gdn, y_f, y_b, xs, gdn_norm_w[0].reshape(1, GDN_DK), vecs,
                 rwkv_a0[0], rwkv_a2[0], rwkv_g2[0], w_out[0].astype(BF16), n_ctx, 256)

    h2, idx_t, gate_t = peer_route(x1, norm2_g[0], sh2, sc2, peer_w_query[0].astype(BF16),
                                   peer_sub_keys[0], 256)
    table = jnp.concatenate([peer_down[0], peer_up[0]], axis=1)[:, None, :]
    out = peer_expert(idx_t.T, h2, gate_t, x1.reshape(bsz * seq, d), gt2,
                      final_norm_g, table, seq, 128)
    return out.reshape(bsz, seq, d)
```

```python
import functools
import math

import jax
import jax.numpy as jnp
from jax import lax
from jax.experimental import pallas as pl
from jax.experimental.pallas import tpu as pltpu

F32 = jnp.float32
BF16 = jnp.bfloat16

GRID_W = 64
GDN_HEADS = 4
GDN_DK = 128
GDN_CONV = 5
CHUNK = 64
RWKV_HEADS = 8
RWKV_HD = 64
RWKV_W = RWKV_HEADS * RWKV_HD
LORA = 64
PEER_HEADS = 8
PEER_NKEYS = 128
PEER_TOPK = 16
N_MOD = 6

NORM_EPS = 1e-6
L2_EPS = 1e-6
RWKV_GN_EPS = 64e-5

LANES = 128
VMEM_LIMIT = 56 * 1024 * 1024
NEG_INF = float("-inf")


def _cparams(sem):
    return pltpu.CompilerParams(dimension_semantics=sem, vmem_limit_bytes=VMEM_LIMIT)


def _dot(a, b):
    return jnp.dot(a.astype(BF16), b.astype(BF16), preferred_element_type=F32)


def _dot_nt(a, b):
    return lax.dot_general(a.astype(BF16), b.astype(BF16), (((1,), (1,)), ((), ())),
                           preferred_element_type=F32)


def _dot_tn(a, b):
    return lax.dot_general(a.astype(BF16), b.astype(BF16), (((0,), (0,)), ((), ())),
                           preferred_element_type=F32)


def _dot_hi(a, b):
    return jnp.dot(a, b, preferred_element_type=F32, precision=lax.Precision.HIGHEST)


def _split3(x):
    hi = x.astype(BF16)
    rest = x - hi.astype(F32)
    mid = rest.astype(BF16)
    lo = (rest - mid.astype(F32)).astype(BF16)
    return hi, mid, lo


def _dot_x01(a, m01):
    m = m01.astype(BF16)
    return sum(jnp.dot(p, m, preferred_element_type=F32) for p in _split3(a))


def _dot_01x(m01, b):
    m = m01.astype(BF16)
    return sum(jnp.dot(m, p, preferred_element_type=F32) for p in _split3(b))


def _sigmoid(x):
    return 1.0 / (1.0 + jnp.exp(-x))


def _silu(x):
    return x * _sigmoid(x)


def _softplus(x):
    return jnp.maximum(x, 0.0) + jnp.log(1.0 + jnp.exp(-jnp.abs(x)))


def _unit_lower_inverses(xs):
    c = xs[0].shape[0]
    eye = (lax.broadcasted_iota(jnp.int32, (c, c), 0) ==
           lax.broadcasted_iota(jnp.int32, (c, c), 1)).astype(F32)
    prods = [eye + x for x in xs]
    pows = [_dot(x, x) for x in xs]
    for _ in range(int(math.log2(c)) - 2):
        both = [_dot(jnp.concatenate([p, x], axis=0), x) for p, x in zip(prods, pows)]
        prods = [p + b[:c] for p, b in zip(prods, both)]
        pows = [b[c:] for b in both]
    return [p + _dot(p, x) for p, x in zip(prods, pows)]


def _order_masks(c, rev):
    r = lax.broadcasted_iota(jnp.int32, (c, c), 0)
    s = lax.broadcasted_iota(jnp.int32, (c, c), 1)
    if rev:
        return s >= r, s > r
    return s <= r, s < r


def _ada_kernel(c_ref, w_ref, b_ref, o_ref):
    o_ref[...] = _dot_hi(_silu(c_ref[...]), w_ref[...]) + b_ref[...]


def ada_mod(cond, w, b):
    n, d = cond.shape
    cols = w.shape[1]
    tn = 1024
    return pl.pallas_call(
        _ada_kernel,
        grid=(cols // tn,),
        in_specs=[pl.BlockSpec((n, d), lambda j: (0, 0)),
                  pl.BlockSpec((d, tn), lambda j: (0, j)),
                  pl.BlockSpec((1, tn), lambda j: (0, j))],
        out_specs=pl.BlockSpec((n, tn), lambda j: (0, j)),
        out_shape=jax.ShapeDtypeStruct((n, cols), F32),
        compiler_params=_cparams(("arbitrary",)),
        name="ada_mod",
    )(cond, w, b.reshape(1, cols))


def _in_proj_kernel(x_ref, g_ref, sh_ref, sc_ref, wg_ref, wr_ref, pg_ref, pr_ref):
    x = x_ref[...]
    y = x * lax.rsqrt(jnp.mean(x * x, axis=-1, keepdims=True) + NORM_EPS) * g_ref[...]
    h = (y * (1.0 + sc_ref[...]) + sh_ref[...]).astype(BF16)
    pg_ref[...] = jnp.dot(h, wg_ref[...], preferred_element_type=F32)
    pr_ref[...] = jnp.dot(h, wr_ref[...], preferred_element_type=F32)


def in_proj(xin, norm_g, shift, scale, w_gdn, w_rwkv, n_ctx, tm):
    bsz, t, d = xin.shape
    cg, cr = w_gdn.shape[1], w_rwkv.shape[1]
    nctx_tiles = n_ctx // tm

    def mod_map(b, i):
        return (b, jnp.where(i < nctx_tiles, 0, 1), 0, 0)

    return pl.pallas_call(
        _in_proj_kernel,
        grid=(bsz, t // tm),
        in_specs=[pl.BlockSpec((None, tm, d), lambda b, i: (b, i, 0)),
                  pl.BlockSpec((1, d), lambda b, i: (0, 0)),
                  pl.BlockSpec((None, None, 1, d), mod_map),
                  pl.BlockSpec((None, None, 1, d), mod_map),
                  pl.BlockSpec((d, cg), lambda b, i: (0, 0)),
                  pl.BlockSpec((d, cr), lambda b, i: (0, 0))],
        out_specs=[pl.BlockSpec((None, tm, cg), lambda b, i: (b, i, 0)),
                   pl.BlockSpec((None, tm, cr), lambda b, i: (b, i, 0))],
        out_shape=[jax.ShapeDtypeStruct((bsz, t, cg), F32),
                   jax.ShapeDtypeStruct((bsz, t, cr), F32)],
        compiler_params=_cparams(("parallel", "parallel")),
        name="in_proj",
    )(xin, norm_g.reshape(1, d), shift, scale, w_gdn, w_rwkv)


def _gdn_prep_kernel(p_ref, w_ref, o_ref, *, n_ctx):
    j = pl.program_id(1)
    x = p_ref[...]
    t_len = x.shape[0]
    w = w_ref[...]
    t = lax.broadcasted_iota(jnp.int32, x.shape, 0)
    is_ctx = t < n_ctx
    lo = jnp.where(is_ctx, 0, n_ctx)
    hi = jnp.where(is_ctx, n_ctx, t_len)
    pad = (GDN_CONV - 1) // 2
    acc = x * w[pad:pad + 1, :]
    for s in range(-pad, pad + 1):
        if s == 0:
            continue
        xs = pltpu.roll(x, (-s) % t_len, 0)
        valid = (t + s >= lo) & (t + s < hi)
        acc = acc + jnp.where(valid, xs, 0.0) * w[s + pad:s + pad + 1, :]
    y = _silu(acc)
    inv = lax.rsqrt(jnp.sum(y * y, axis=-1, keepdims=True) + L2_EPS)
    fac = jnp.where(j < GDN_HEADS, inv * (GDN_DK ** -0.5), jnp.where(j < 2 * GDN_HEADS, inv, 1.0))
    o_ref[...] = y * fac


def gdn_prep(p_gdn, conv_w, n_ctx):
    bsz, t, _ = p_gdn.shape
    ncol = conv_w.shape[1]
    return pl.pallas_call(
        functools.partial(_gdn_prep_kernel, n_ctx=n_ctx),
        grid=(bsz, ncol // LANES),
        in_specs=[pl.BlockSpec((None, t, LANES), lambda b, j: (b, 0, j)),
                  pl.BlockSpec((GDN_CONV, LANES), lambda b, j: (0, j))],
        out_specs=pl.BlockSpec((None, t, LANES), lambda b, j: (b, 0, j)),
        out_shape=jax.ShapeDtypeStruct((bsz, t, ncol), F32),
        compiler_params=_cparams(("parallel", "parallel")),
        name="gdn_prep",
    )(p_gdn, conv_w)


def _gdn_chunk_kernel(qf_ref, qb_ref, gf_ref, gb_ref, prm_ref, of_ref, ob_ref, s_ref):
    @pl.when(pl.program_id(1) == 0)
    def _():
        s_ref[...] = jnp.zeros_like(s_ref)

    nh, dk = GDN_HEADS, GDN_DK
    nb = qf_ref.shape[0]
    prm = prm_ref[...]
    a_log, dt_bias = prm[0:1, :], prm[1:2, :]
    q, k, v, beta, g_col, g_last, gamma, strict = [], [], [], [], [], [], [], []
    for bi, d in [(bi, d) for bi in range(nb) for d in range(2)]:
        qkv = (qf_ref, qb_ref)[d][bi]
        gates = (gf_ref, gb_ref)[d][bi]
        c = qkv.shape[0]
        incl, strict_d = _order_masks(c, d == 1)
        g_all = -jnp.exp(a_log) * _softplus(gates + dt_bias)
        beta_all = _sigmoid(gates)
        big_g = _dot_01x(incl.astype(F32), g_all)
        big_g_t = big_g.T
        last = 0 if d == 1 else c - 1
        for h in range(nh):
            col = d * nh + h
            q.append(qkv[:, h * dk:(h + 1) * dk])
            k.append(qkv[:, (nh + h) * dk:(nh + h + 1) * dk])
            v.append(qkv[:, (2 * nh + h) * dk:(2 * nh + h + 1) * dk])
            beta.append(beta_all[:, 2 * nh + col:2 * nh + col + 1])
            g_col.append(big_g[:, col:col + 1])
            g_last.append(big_g[last:last + 1, col:col + 1])
            g_row = big_g_t[col:col + 1, :]
            gamma.append(jnp.where(incl, jnp.exp(jnp.where(incl, g_col[-1] - g_row, 0.0)), 0.0))
            strict.append(strict_d)
    n = len(q)
    c = q[0].shape[0]
    e_g = [jnp.exp(g) for g in g_col]
    kq = [_dot_nt(jnp.concatenate([k[i], q[i]], axis=0), k[i]) for i in range(n)]
    a_mat = [jnp.where(strict[i], beta[i] * kq[i][:c] * gamma[i], 0.0) for i in range(n)]
    qk = [kq[i][c:] * gamma[i] for i in range(n)]
    t_inv = _unit_lower_inverses([-a for a in a_mat])
    uw = [_dot(t_inv[i], jnp.concatenate([beta[i] * v[i], (beta[i] * e_g[i]) * k[i]], axis=1))
          for i in range(n)]
    s = [s_ref[i // nh, i % nh] for i in range(n)]
    ws = [_dot(jnp.concatenate([uw[i][:, dk:], q[i] * e_g[i]], axis=0), s[i]) for i in range(n)]
    v_new = [uw[i][:, :dk] - ws[i][:c] for i in range(n)]
    o = [ws[i][c:] + _dot(qk[i], v_new[i]) for i in range(n)]
    s_new = [s[i] * jnp.exp(g_last[i]) + _dot_tn(k[i] * jnp.exp(g_last[i] - g_col[i]), v_new[i])
             for i in range(n)]
    for i in range(n):
        s_ref[i // nh, i % nh] = s_new[i]
    for bi in range(nb):
        base = 2 * nh * bi
        of_ref[bi] = jnp.concatenate(o[base:base + nh], axis=-1)
        ob_ref[bi] = jnp.concatenate(o[base + nh:base + 2 * nh], axis=-1)


BATCH_PER_STEP = 2


def _rev_chunk_map(n_ctx_chunks, n_chunks):
    def cb(s):
        return jnp.where(s < n_ctx_chunks, n_ctx_chunks - 1 - s, n_chunks + n_ctx_chunks - 1 - s)
    return cb


def gdn_chunk(qkv, p_gdn, prm, n_ctx):
    bsz, t, ncol = qkv.shape
    nv = GDN_HEADS * GDN_DK
    n_chunks = t // CHUNK
    cb = _rev_chunk_map(n_ctx // CHUNK, n_chunks)
    gate_tile = p_gdn.shape[2] // LANES - 1
    nb = BATCH_PER_STEP if bsz % BATCH_PER_STEP == 0 else 1
    return pl.pallas_call(
        _gdn_chunk_kernel,
        grid=(bsz // nb, n_chunks),
        in_specs=[pl.BlockSpec((nb, CHUNK, ncol), lambda b, s: (b, s, 0)),
                  pl.BlockSpec((nb, CHUNK, ncol), lambda b, s: (b, cb(s), 0)),
                  pl.BlockSpec((nb, CHUNK, LANES), lambda b, s: (b, s, gate_tile)),
                  pl.BlockSpec((nb, CHUNK, LANES), lambda b, s: (b, cb(s), gate_tile)),
                  pl.BlockSpec((8, LANES), lambda b, s: (0, 0))],
        out_specs=[pl.BlockSpec((nb, CHUNK, nv), lambda b, s: (b, s, 0)),
                   pl.BlockSpec((nb, CHUNK, nv), lambda b, s: (b, cb(s), 0))],
        out_shape=[jax.ShapeDtypeStruct((bsz, t, nv), F32)] * 2,
        scratch_shapes=[pltpu.VMEM((2 * nb, GDN_HEADS, GDN_DK, GDN_DK), F32)],
        compiler_params=_cparams(("parallel", "arbitrary")),
        name="gdn_chunk",
    )(qkv, qkv, p_gdn, p_gdn, prm)


def _rwkv_shift_kernel(p_ref, mu_ref, o_ref, *, n_ctx, n_cols):
    j = pl.program_id(1)
    x = p_ref[...]
    t_len = x.shape[0]
    t = lax.broadcasted_iota(jnp.int32, x.shape, 0)
    ch = lax.broadcasted_iota(jnp.int32, x.shape, 1) + j * LANES
    is_ctx = t < n_ctx
    col = (t - n_ctx) % GRID_W
    prev1 = pltpu.roll(x, 1, 0)
    next1 = pltpu.roll(x, t_len - 1, 0)
    up = pltpu.roll(x, GRID_W, 0)
    down = pltpu.roll(x, t_len - GRID_W, 0)
    quarter = n_cols // 4
    half = n_cols // 2
    ctx_sh = jnp.where(ch < half,
                       jnp.where(t >= 1, prev1, 0.0),
                       jnp.where(t < n_ctx - 1, next1, 0.0))
    lat_sh = jnp.where(ch < quarter, jnp.where(col >= 1, prev1, 0.0),
                       jnp.where(ch < 2 * quarter, jnp.where(col < GRID_W - 1, next1, 0.0),
                                 jnp.where(ch < 3 * quarter,
                                           jnp.where(t >= n_ctx + GRID_W, up, 0.0),
                                           jnp.where(t < t_len - GRID_W, down, 0.0))))
    sh = jnp.where(is_ctx, ctx_sh, lat_sh)
    o_ref[...] = x + mu_ref[...] * (sh - x)


def rwkv_shift(p_rwkv, mu, n_ctx):
    bsz, t, ncol = p_rwkv.shape
    return pl.pallas_call(
        functools.partial(_rwkv_shift_kernel, n_ctx=n_ctx, n_cols=ncol),
        grid=(bsz, ncol // LANES),
        in_specs=[pl.BlockSpec((None, t, LANES), lambda b, j: (b, 0, j)),
                  pl.BlockSpec((1, LANES), lambda b, j: (0, j))],
        out_specs=pl.BlockSpec((None, t, LANES), lambda b, j: (b, 0, j)),
        out_shape=jax.ShapeDtypeStruct((bsz, t, ncol), F32),
        compiler_params=_cparams(("parallel", "parallel")),
        name="rwkv_shift",
    )(p_rwkv, mu.reshape(1, ncol))


def _rwkv_gates(xs, d, w0_ref, w2_ref, a0_ref, a2_ref):
    w_ = RWKV_W
    xw = xs[:, 3 * w_ + d * LORA:3 * w_ + (d + 1) * LORA]
    xa = xs[:, 3 * w_ + 2 * LORA + d * LORA:3 * w_ + 2 * LORA + (d + 1) * LORA]
    w_pre = w0_ref[d:d + 1, :] + _dot(jnp.tanh(xw), w2_ref[d])
    log_w = -math.exp(-0.5) * _sigmoid(w_pre)
    a = _sigmoid(a0_ref[d:d + 1, :] + _dot(xa, a2_ref[d]))
    return log_w, a


def _rwkv_chunk_kernel(xf_ref, xb_ref, vec_ref, w0_ref, w2_ref, a0_ref, a2_ref, hsum_ref,
                       yf_ref, yb_ref, p_ref):
    @pl.when(pl.program_id(1) == 0)
    def _():
        p_ref[...] = jnp.zeros_like(p_ref)

    w_, n, nh = RWKV_W, RWKV_HD, RWKV_HEADS
    nb = xf_ref.shape[0]
    k_k, k_a = vec_ref[0:1, :], vec_ref[1:2, :]
    lhs, rhs_t, tail, vals, w_last, p0, strict, incl = [], [], [], [], [], [], [], []
    for bi, d in [(bi, d) for bi in range(nb) for d in range(2)]:
        xs = (xf_ref, xb_ref)[d][bi]
        c = xs.shape[0]
        incl_d, strict_d = _order_masks(c, d == 1)
        r = xs[:, 0:w_]
        k = xs[:, w_:2 * w_]
        v = xs[:, 2 * w_:3 * w_]
        log_w, a = _rwkv_gates(xs, d, w0_ref, w2_ref, a0_ref, a2_ref)
        kk = k * k_k
        kk = kk * lax.rsqrt(_dot_x01(kk * kk, hsum_ref[...]) + L2_EPS)
        k_dir = k * (1.0 + (a - 1.0) * k_a)
        alpha = -(kk * a)
        big_g = _dot_01x(incl_d.astype(F32), log_w)
        last = 0 if d == 1 else c - 1
        g_last = big_g[last:last + 1, :]
        e_neg = jnp.exp(-big_g)
        e_tail = jnp.exp(g_last - big_g)
        r_t = r * jnp.exp(big_g)
        b_t = kk * jnp.exp(big_g - log_w)
        a_t = alpha * e_neg
        k_t = k_dir * e_neg
        a_h = alpha * e_tail
        k_h = k_dir * e_tail
        w_l = jnp.exp(g_last)
        for h in range(nh):
            sl = slice(h * n, (h + 1) * n)
            lhs.append(jnp.concatenate([b_t[:, sl], r_t[:, sl]], axis=0))
            rhs_t.append(jnp.concatenate([a_t[:, sl], k_t[:, sl]], axis=0))
            tail.append(jnp.concatenate([a_h[:, sl], k_h[:, sl]], axis=0))
            vals.append(v[:, sl])
            w_last.append(w_l[:, sl])
            p0.append(p_ref[2 * bi + d, h])
            strict.append(strict_d)
            incl.append(incl_d)
    m = len(lhs)
    c = vals[0].shape[0]
    quad = [_dot_nt(lhs[i], rhs_t[i]) for i in range(m)]
    a_ba = [jnp.where(strict[i], quad[i][:c, :c], 0.0) for i in range(m)]
    a_k = [jnp.concatenate([jnp.where(strict[i], quad[i][:c, c:], 0.0),
                            jnp.where(incl[i], quad[i][c:, c:], 0.0)], axis=0) for i in range(m)]
    a_ra = [jnp.where(incl[i], quad[i][c:, :c], 0.0) for i in range(m)]
    from_state = [_dot_nt(lhs[i], p0[i]) for i in range(m)]
    from_vals = [_dot(a_k[i], vals[i]) for i in range(m)]
    t_inv = _unit_lower_inverses(a_ba)
    u = [_dot(t_inv[i], from_state[i][:c] + from_vals[i][:c]) for i in range(m)]
    y = [from_state[i][c:] + _dot(a_ra[i], u[i]) + from_vals[i][c:] for i in range(m)]
    p_new = [p0[i] * w_last[i] + _dot_tn(jnp.concatenate([u[i], vals[i]], axis=0), tail[i])
             for i in range(m)]
    for i in range(m):
        p_ref[i // nh, i % nh] = p_new[i]
    for bi in range(nb):
        base = 2 * nh * bi
        yf_ref[bi] = jnp.concatenate(y[base:base + nh], axis=-1)
        yb_ref[bi] = jnp.concatenate(y[base + nh:base + 2 * nh], axis=-1)


def _head_sum_matrix(width, group):
    i = jnp.arange(width)
    return (i[:, None] // group == i[None, :] // group).astype(F32)


def rwkv_chunk(xs, vecs, w0, w2, a0, a2, n_ctx):
    bsz, t, ncol = xs.shape
    n_chunks = t // CHUNK
    cb = _rev_chunk_map(n_ctx // CHUNK, n_chunks)
    hsum = _head_sum_matrix(RWKV_W, RWKV_HD)
    full = lambda *shape: pl.BlockSpec(shape, lambda b, s: (0,) * len(shape))
    nb = BATCH_PER_STEP if bsz % BATCH_PER_STEP == 0 else 1
    return pl.pallas_call(
        _rwkv_chunk_kernel,
        grid=(bsz // nb, n_chunks),
        in_specs=[pl.BlockSpec((nb, CHUNK, ncol), lambda b, s: (b, s, 0)),
                  pl.BlockSpec((nb, CHUNK, ncol), lambda b, s: (b, cb(s), 0)),
                  full(8, RWKV_W), full(2, RWKV_W), full(2, LORA, RWKV_W),
                  full(2, RWKV_W), full(2, LORA, RWKV_W), full(RWKV_W, RWKV_W)],
        out_specs=[pl.BlockSpec((nb, CHUNK, RWKV_W), lambda b, s: (b, s, 0)),
                   pl.BlockSpec((nb, CHUNK, RWKV_W), lambda b, s: (b, cb(s), 0))],
        out_shape=[jax.ShapeDtypeStruct((bsz, t, RWKV_W), F32)] * 2,
        scratch_shapes=[pltpu.VMEM((2 * nb, RWKV_HEADS, RWKV_HD, RWKV_HD), F32)],
        compiler_params=_cparams(("parallel", "arbitrary")),
        name="rwkv_chunk",
    )(xs, xs, vecs, w0, w2, a0, a2, hsum)


def _mix_out_kernel(x_ref, gt_ref, of_ref, ob_ref, z_ref, yf_ref, yb_ref, xs_ref,
                    gnw_ref, vec_ref, a0_ref, a2_ref, g2_ref, hsum_ref, wo_ref, o_ref):
    nh, dv = GDN_HEADS, GDN_DK
    w_ = RWKV_W
    o = of_ref[...] + ob_ref[...]
    z = z_ref[...]
    parts = []
    for h in range(nh):
        oh = o[:, h * dv:(h + 1) * dv]
        oh = oh * lax.rsqrt(jnp.mean(oh * oh, axis=-1, keepdims=True) + NORM_EPS)
        parts.append(oh * gnw_ref[...] * _silu(z[:, h * dv:(h + 1) * dv]))
    xs = xs_ref[...]
    r = xs[:, 0:w_]
    k = xs[:, w_:2 * w_]
    v = xs[:, 2 * w_:3 * w_]
    xg = xs[:, 3 * w_ + 4 * LORA:]
    k_a, r_k = vec_ref[1:2, :], vec_ref[2:3, :]
    gn_w, gn_b = vec_ref[3:4, :], vec_ref[4:5, :]
    hsum = hsum_ref[...]
    y = yf_ref[...] + yb_ref[...]
    mean = _dot_x01(y, hsum) * (1.0 / RWKV_HD)
    yc = y - mean
    var = _dot_x01(yc * yc, hsum) * (1.0 / RWKV_HD)
    yn = yc * lax.rsqrt(var + RWKV_GN_EPS) * gn_w + gn_b
    rk = jnp.zeros_like(r)
    for d in range(2):
        xa = xs[:, 3 * w_ + 2 * LORA + d * LORA:3 * w_ + 2 * LORA + (d + 1) * LORA]
        a = _sigmoid(a0_ref[d:d + 1, :] + _dot(xa, a2_ref[d]))
        rk = rk + r * (k * (1.0 + (a - 1.0) * k_a)) * r_k
    bonus = _dot_x01(rk, hsum) * v
    gate = _dot(_sigmoid(xg), g2_ref[...])
    parts.append((yn + bonus) * gate)
    mixed = jnp.concatenate(parts, axis=-1).astype(BF16)
    o_ref[...] = x_ref[...] + gt_ref[...] * jnp.dot(mixed, wo_ref[...], preferred_element_type=F32)


def mix_out(x, gt1, o_f, o_b, p_gdn, y_f, y_b, xs, gnw, vecs, a0, a2, g2, w_out, n_ctx, tm):
    bsz, seq, d = x.shape
    off = n_ctx // tm
    nv = GDN_HEADS * GDN_DK
    ncol = xs.shape[2]
    hsum = _head_sum_matrix(RWKV_W, RWKV_HD)
    lat = lambda width, cblk=0: pl.BlockSpec((None, tm, width), lambda b, i: (b, i + off, cblk))
    full = lambda *shape: pl.BlockSpec(shape, lambda b, i: (0,) * len(shape))
    return pl.pallas_call(
        _mix_out_kernel,
        grid=(bsz, seq // tm),
        in_specs=[pl.BlockSpec((None, tm, d), lambda b, i: (b, i, 0)),
                  pl.BlockSpec((None, 1, d), lambda b, i: (b, 0, 0)),
                  lat(nv), lat(nv), lat(nv, 3), lat(RWKV_W), lat(RWKV_W), lat(ncol),
                  full(1, GDN_DK), full(8, RWKV_W), full(2, RWKV_W), full(2, LORA, RWKV_W),
                  full(2 * LORA, RWKV_W), full(RWKV_W, RWKV_W), full(nv + RWKV_W, d)],
        out_specs=pl.BlockSpec((None, tm, d), lambda b, i: (b, i, 0)),
        out_shape=jax.ShapeDtypeStruct((bsz, seq, d), F32),
        compiler_params=_cparams(("parallel", "parallel")),
        name="mix_out",
    )(x, gt1, o_f, o_b, p_gdn, y_f, y_b, xs, gnw, vecs, a0, a2, g2, hsum, w_out)


def _extract_top(s, riota, n_rows, vals_ref, idx_ref, k):
    for i in range(k):
        m = jnp.max(s, axis=0, keepdims=True)
        am = jnp.min(jnp.where(s == m, riota, float(n_rows)), axis=0, keepdims=True)
        vals_ref[i:i + 1, :] = m
        idx_ref[i:i + 1, :] = am
        s = jnp.where(riota == am, NEG_INF, s)


def _candidate_blocks(k):
    split = math.isqrt(k)
    blocks = [(False, i, 0, k // (i + 1)) for i in range(split)]
    j = 0
    while k // (j + 1) > split:
        blocks.append((True, j, split, k // (j + 1)))
        j += 1
    return blocks


def _candidate_rows(k):
    return sum(-(-(hi - lo) // 8) * 8 for _, _, lo, hi in _candidate_blocks(k))


def _top_list_rows(k):
    need = max(lo + -(-(hi - lo) // 8) * 8 for _, _, lo, hi in _candidate_blocks(k))
    return -(-need // 8) * 8


def _peer_route_kernel(x_ref, g_ref, sh_ref, sc_ref, wq_ref, keys_ref,
                       h2_ref, idx_ref, gate_ref,
                       q_scr, tv_scr, ti_scr, cs_scr, ci_scr, cf_scr, bv_scr, bi_scr):
    h = pl.program_id(1)
    dq = 2 * LANES
    kk = PEER_TOPK

    @pl.when(h == 0)
    def _():
        x = x_ref[...]
        y = x * lax.rsqrt(jnp.mean(x * x, axis=-1, keepdims=True) + NORM_EPS) * g_ref[...]
        h2 = y * (1.0 + sc_ref[...]) + sh_ref[...]
        h2_ref[...] = h2
        hb = h2.astype(BF16)
        for hh in range(PEER_HEADS):
            q_scr[hh] = jnp.dot(hb, wq_ref[:, hh * dq:(hh + 1) * dq], preferred_element_type=F32)

    q = q_scr[h]
    tm = q.shape[0]
    riota = lax.broadcasted_iota(jnp.int32, (PEER_NKEYS, tm), 0).astype(F32)
    for p in range(2):
        s_t = _dot_nt(keys_ref[p], q[:, p * LANES:(p + 1) * LANES])
        _extract_top(s_t, riota, PEER_NKEYS, tv_scr.at[p], ti_scr.at[p], kk)
    tv_scr[:, kk:, :] = jnp.zeros_like(tv_scr[:, kk:, :])
    ti_scr[:, kk:, :] = jnp.zeros_like(ti_scr[:, kk:, :])
    row = 0
    for fixed_j, fixed, lo, hi in _candidate_blocks(kk):
        n_pad = -(-(hi - lo) // 8) * 8
        r = lax.broadcasted_iota(jnp.int32, (n_pad, tm), 0).astype(F32)
        run, one = (0, 1) if fixed_j else (1, 0)
        vals = tv_scr[run, lo:lo + n_pad, :] + tv_scr[one, fixed:fixed + 1, :]
        run_i, one_i = ti_scr[run, lo:lo + n_pad, :], ti_scr[one, fixed:fixed + 1, :]
        if fixed_j:
            experts = run_i * float(PEER_NKEYS) + one_i
            flat = (r + float(lo)) * float(kk) + float(fixed)
        else:
            experts = one_i * float(PEER_NKEYS) + run_i
            flat = float(fixed * kk + lo) + r
        cs_scr[row:row + n_pad, :] = jnp.where(r < float(hi - lo), vals, NEG_INF)
        ci_scr[row:row + n_pad, :] = experts
        cf_scr[row:row + n_pad, :] = flat
        row += n_pad
    cand = cs_scr[...]
    cand_i = ci_scr[...]
    ciota = cf_scr[...]
    for i in range(kk):
        m = jnp.max(cand, axis=0, keepdims=True)
        pos = jnp.min(jnp.where(cand == m, ciota, float(kk * kk)), axis=0, keepdims=True)
        hit = ciota == pos
        bv_scr[i:i + 1, :] = m
        bi_scr[i:i + 1, :] = jnp.sum(jnp.where(hit, cand_i, 0.0), axis=0, keepdims=True)
        cand = jnp.where(hit, NEG_INF, cand)
    best = bv_scr[...]
    e = jnp.exp(best - best[0:1, :])
    gate_ref[...] = e / jnp.sum(e, axis=0, keepdims=True)
    idx_ref[...] = bi_scr[...].astype(jnp.int32)


def peer_route(x1, norm_g, sh2, sc2, wq, sub_keys, tm):
    bsz, seq, d = x1.shape
    ntok = bsz * seq
    tiles_per_b = seq // tm
    kk = PEER_TOPK
    nq = wq.shape[1]
    return pl.pallas_call(
        _peer_route_kernel,
        grid=(ntok // tm, PEER_HEADS),
        in_specs=[pl.BlockSpec((tm, d), lambda i, h: (i, 0)),
                  pl.BlockSpec((1, d), lambda i, h: (0, 0)),
                  pl.BlockSpec((None, 1, d), lambda i, h: (i // tiles_per_b, 0, 0)),
                  pl.BlockSpec((None, 1, d), lambda i, h: (i // tiles_per_b, 0, 0)),
                  pl.BlockSpec((d, nq), lambda i, h: (0, 0)),
                  pl.BlockSpec((None, 2, PEER_NKEYS, LANES), lambda i, h: (h, 0, 0, 0))],
        out_specs=[pl.BlockSpec((tm, d), lambda i, h: (i, 0)),
                   pl.BlockSpec((kk, tm), lambda i, h: (h, i)),
                   pl.BlockSpec((kk, tm), lambda i, h: (h, i))],
        out_shape=[jax.ShapeDtypeStruct((ntok, d), F32),
                   jax.ShapeDtypeStruct((PEER_HEADS * kk, ntok), jnp.int32),
                   jax.ShapeDtypeStruct((PEER_HEADS * kk, ntok), F32)],
        scratch_shapes=[pltpu.VMEM((PEER_HEADS, tm, 2 * LANES), F32),
                        pltpu.VMEM((2, _top_list_rows(kk), tm), F32),
                        pltpu.VMEM((2, _top_list_rows(kk), tm), F32),
                        pltpu.VMEM((_candidate_rows(kk), tm), F32),
                        pltpu.VMEM((_candidate_rows(kk), tm), F32),
                        pltpu.VMEM((_candidate_rows(kk), tm), F32),
                        pltpu.VMEM((kk, tm), F32), pltpu.VMEM((kk, tm), F32)],
        compiler_params=_cparams(("parallel", "arbitrary")),
        name="peer_route",
    )(x1.reshape(ntok, d), norm_g.reshape(1, d), sh2, sc2, wq, sub_keys)


GROUP = 4
N_GSLOTS = 4
AHEAD = 2
N_SLOTS = GROUP * N_GSLOTS
ROW_TILE = 8
DMA_QUEUES = 2


def _peer_expert_kernel(idx_ref, h2_ref, gate_ref, x1_ref, gt_ref, fg_ref, tbl_ref,
                        o_ref, *scratch, tt, d):
    bufs, tiles, sem = scratch[:N_SLOTS], scratch[N_SLOTS:2 * N_SLOTS], scratch[2 * N_SLOTS]
    ne = PEER_HEADS * PEER_TOPK
    step = pl.program_id(0)
    last_step = pl.num_programs(0) - 1

    def issue_token(base, gslot, j):
        slot = gslot * GROUP + j
        for k in range(ne):
            e = idx_ref[base + k]
            pltpu.make_async_copy(tbl_ref.at[e], bufs[slot].at[k],
                                  sem.at[gslot]).start(priority=k % DMA_QUEUES)

    def wait_group(gslot):
        for j in range(GROUP):
            pltpu.make_async_copy(tbl_ref.at[pl.ds(0, ne)], bufs[gslot * GROUP + j],
                                  sem.at[gslot]).wait()

    def evaluate(t, slot):
        tile = tiles[slot]
        tile[...] = bufs[slot][...].reshape(ne, 2 * d)
        hrow = h2_ref[pl.ds(t, 1), :]
        grow = gate_ref[pl.ds(t, 1), :]
        lane = lax.broadcasted_iota(jnp.int32, (ROW_TILE, ne), 1)
        sub = lax.broadcasted_iota(jnp.int32, (ROW_TILE, ne), 0)
        acc = jnp.zeros((ROW_TILE, d), F32)
        for g in range(ne // ROW_TILE):
            rows = tile[g * ROW_TILE:(g + 1) * ROW_TILE, :]
            pre = jnp.sum(rows[:, :d] * hrow, axis=1, keepdims=True)
            act = 0.5 * pre * (1.0 + lax.erf(pre * (2.0 ** -0.5)))
            gcol = jnp.sum(jnp.where(lane == sub + g * ROW_TILE, grow, 0.0), axis=1, keepdims=True)
            acc = acc + (act * gcol) * rows[:, d:]
        y = jnp.sum(acc, axis=0, keepdims=True)
        xo = x1_ref[pl.ds(t, 1), :] + gt_ref[...] * y
        xo = xo * lax.rsqrt(jnp.mean(xo * xo, axis=-1, keepdims=True) + NORM_EPS) * fg_ref[...]
        o_ref[pl.ds(t, 1), :] = xo

    @pl.when(step == 0)
    def _():
        for g in range(AHEAD):
            for j in range(GROUP):
                issue_token((g * GROUP + j) * ne, g, j)

    def ring_turn(it, carry):
        for gs in range(N_GSLOTS):
            t0 = (it * N_GSLOTS + gs) * GROUP
            wait_group(gs)
            for j in range(GROUP):
                issue_token((t0 + AHEAD * GROUP + j) * ne, (gs + AHEAD) % N_GSLOTS, j)
                evaluate(t0 + j, gs * GROUP + j)
        return carry

    lax.fori_loop(0, tt // N_SLOTS, ring_turn, 0)

    @pl.when(step == last_step)
    def _():
        for g in range(AHEAD):
            wait_group(g)


def peer_expert(idx_tok, h2, gate_tok, x1, gt2, final_g, table, seq, tt):
    ntok, d = h2.shape
    ne = PEER_HEADS * PEER_TOPK
    tiles_per_b = seq // tt
    n_steps = ntok // tt
    assert tt % N_SLOTS == 0
    idx_steps = idx_tok.reshape(n_steps, tt * ne)
    tail = jnp.roll(idx_steps[:, :AHEAD * GROUP * ne], -1, axis=0)
    win = (tt + AHEAD * GROUP) * ne
    idx_win = jnp.concatenate([idx_steps, tail], axis=1).reshape(n_steps * win)
    return pl.pallas_call(
        functools.partial(_peer_expert_kernel, tt=tt, d=d),
        grid=(n_steps,),
        in_specs=[pl.BlockSpec((win,), lambda i: (i,), memory_space=pltpu.SMEM),
                  pl.BlockSpec((tt, d), lambda i: (i, 0)),
                  pl.BlockSpec((tt, ne), lambda i: (i, 0)),
                  pl.BlockSpec((tt, d), lambda i: (i, 0)),
                  pl.BlockSpec((None, 1, d), lambda i: (i // tiles_per_b, 0, 0)),
                  pl.BlockSpec((1, d), lambda i: (0, 0)),
                  pl.BlockSpec(memory_space=pl.ANY)],
        out_specs=pl.BlockSpec((tt, d), lambda i: (i, 0)),
        out_shape=jax.ShapeDtypeStruct((ntok, d), F32),
        scratch_shapes=([pltpu.VMEM((ne, 1, 2 * d), F32) for _ in range(N_SLOTS)]
                        + [pltpu.VMEM((ne, 2 * d), F32) for _ in range(N_SLOTS)]
                        + [pltpu.SemaphoreType.DMA((N_GSLOTS,))]),
        compiler_params=_cparams(("arbitrary",)),
        name="peer_expert",
    )(idx_win, h2, gate_tok, x1, gt2, final_g.reshape(1, d), table)


def kernel(x, c, ctx, c_ctx, ada_w, ada_b, norm1_g, w_in, gdn_conv_w, gdn_a_log, gdn_dt_bias,
           gdn_norm_w, rwkv_mu, rwkv_w0, rwkv_w2, rwkv_a0, rwkv_a2, rwkv_g2, rwkv_k_k, rwkv_k_a,
           rwkv_r_k, rwkv_gn_w, rwkv_gn_b, w_out, norm2_g, peer_w_query, peer_sub_keys,
           peer_down, peer_up, final_norm_g):
    bsz, seq, d = x.shape
    n_ctx = ctx.shape[1]
    assert ada_w.shape[0] == 1, "single-layer block"
    assert n_ctx % 256 == 0 and seq % 256 == 0 and seq % GRID_W == 0
    nh = GDN_HEADS
    n_qkvz = 4 * nh * GDN_DK
    gdn_cols = n_qkvz + 4 * nh

    cond = jnp.concatenate([c, c_ctx[None, :], jnp.zeros((16 - bsz - 1, d), F32)], axis=0)
    mods = ada_mod(cond, ada_w[0], ada_b[0])
    m_lat = mods[:bsz].reshape(bsz, N_MOD, 1, d)
    m_ctx = jnp.broadcast_to(mods[bsz].reshape(1, N_MOD, 1, d), (bsz, N_MOD, 1, d))
    shift1 = jnp.stack([m_ctx[:, 0], m_lat[:, 0]], axis=1)
    scale1 = jnp.stack([m_ctx[:, 1], m_lat[:, 1]], axis=1)
    gt1, sh2, sc2, gt2 = m_lat[:, 2], m_lat[:, 3], m_lat[:, 4], m_lat[:, 5]

    wl = w_in[0]
    w_gdn = jnp.concatenate([wl[:, :gdn_cols], jnp.zeros((d, LANES - 4 * nh), F32)], axis=1)
    w_rwkv = wl[:, gdn_cols:]
    xin = jnp.concatenate([ctx, x], axis=1)
    p_gdn, p_rwkv = in_proj(xin, norm1_g[0], shift1, scale1, w_gdn.astype(BF16),
                            w_rwkv.astype(BF16), n_ctx, 256)

    qkv = gdn_prep(p_gdn, gdn_conv_w[0], n_ctx)
    prm = jnp.zeros((8, LANES), F32)
    prm = prm.at[0, :2 * nh].set(gdn_a_log[0].reshape(-1))
    prm = prm.at[1, :2 * nh].set(gdn_dt_bias[0].reshape(-1))
    o_f, o_b = gdn_chunk(qkv, p_gdn, prm, n_ctx)

    xs = rwkv_shift(p_rwkv, rwkv_mu[0], n_ctx)
    vecs = jnp.zeros((8, RWKV_W), F32)
    for i, vec in enumerate((rwkv_k_k, rwkv_k_a, rwkv_r_k, rwkv_gn_w, rwkv_gn_b)):
        vecs = vecs.at[i].set(vec[0])
    y_f, y_b = rwkv_chunk(xs, vecs, rwkv_w0[0], rwkv_w2[0], rwkv_a0[0], rwkv_a2[0], n_ctx)

    x1 = mix_out(x, gt1, o_f, o_b, p_gdn, y_f, y_b, xs, gdn_norm_w[0].reshape(1, GDN_DK), vecs,
                 rwkv_a0[0], rwkv_a2[0], rwkv_g2[0], w_out[0].astype(BF16), n_ctx, 256)

    h2, idx_t, gate_t = peer_route(x1, norm2_g[0], sh2, sc2, peer_w_query[0].astype(BF16),
                                   peer_sub_keys[0], 256)
    table = jnp.concatenate([peer_down[0], peer_up[0]], axis=1)[:, None, :]
    out = peer_expert(idx_t.T, h2, gate_t.T, x1.reshape(bsz * seq, d), gt2,
                      final_norm_g, table, seq, 128)
    return out.reshape(bsz, seq, d)
```

```python
import functools
import math

import jax
import jax.numpy as jnp
from jax import lax
from jax.experimental import pallas as pl
from jax.experimental.pallas import tpu as pltpu

F32 = jnp.float32
BF16 = jnp.bfloat16

GRID_W = 64
GDN_HEADS = 4
GDN_DK = 128
GDN_CONV = 5
CHUNK = 64
RWKV_HEADS = 8
RWKV_HD = 64
RWKV_W = RWKV_HEADS * RWKV_HD
LORA = 64
PEER_HEADS = 8
PEER_NKEYS = 128
PEER_TOPK = 16
N_MOD = 6

NORM_EPS = 1e-6
L2_EPS = 1e-6
RWKV_GN_EPS = 64e-5

LANES = 128
VMEM_LIMIT = 56 * 1024 * 1024
NEG_INF = float("-inf")


def _cparams(sem):
    return pltpu.CompilerParams(dimension_semantics=sem, vmem_limit_bytes=VMEM_LIMIT)


def _dot(a, b):
    return jnp.dot(a.astype(BF16), b.astype(BF16), preferred_element_type=F32)


def _dot_nt(a, b):
    return lax.dot_general(a.astype(BF16), b.astype(BF16), (((1,), (1,)), ((), ())),
                           preferred_element_type=F32)


def _dot_tn(a, b):
    return lax.dot_general(a.astype(BF16), b.astype(BF16), (((0,), (0,)), ((), ())),
                           preferred_element_type=F32)


def _dot_hi(a, b):
    return jnp.dot(a, b, preferred_element_type=F32, precision=lax.Precision.HIGHEST)


def _split3(x):
    hi = x.astype(BF16)
    rest = x - hi.astype(F32)
    mid = rest.astype(BF16)
    lo = (rest - mid.astype(F32)).astype(BF16)
    return hi, mid, lo


def _dot_x01(a, m01):
    m = m01.astype(BF16)
    return sum(jnp.dot(p, m, preferred_element_type=F32) for p in _split3(a))


def _dot_01x(m01, b):
    m = m01.astype(BF16)
    return sum(jnp.dot(m, p, preferred_element_type=F32) for p in _split3(b))


def _sigmoid(x):
    return 1.0 / (1.0 + jnp.exp(-x))


def _silu(x):
    return x * _sigmoid(x)


def _softplus(x):
    return jnp.maximum(x, 0.0) + jnp.log(1.0 + jnp.exp(-jnp.abs(x)))


def _unit_lower_inverses(xs):
    c = xs[0].shape[0]
    eye = (lax.broadcasted_iota(jnp.int32, (c, c), 0) ==
           lax.broadcasted_iota(jnp.int32, (c, c), 1)).astype(F32)
    prods = [eye + x for x in xs]
    pows = [_dot(x, x) for x in xs]
    for _ in range(int(math.log2(c)) - 2):
        both = [_dot(jnp.concatenate([p, x], axis=0), x) for p, x in zip(prods, pows)]
        prods = [p + b[:c] for p, b in zip(prods, both)]
        pows = [b[c:] for b in both]
    return [p + _dot(p, x) for p, x in zip(prods, pows)]


def _order_masks(c, rev):
    r = lax.broadcasted_iota(jnp.int32, (c, c), 0)
    s = lax.broadcasted_iota(jnp.int32, (c, c), 1)
    if rev:
        return s >= r, s > r
    return s <= r, s < r


def _ada_kernel(c_ref, w_ref, b_ref, o_ref):
    o_ref[...] = _dot_hi(_silu(c_ref[...]), w_ref[...]) + b_ref[...]


def ada_mod(cond, w, b):
    n, d = cond.shape
    cols = w.shape[1]
    tn = 1024
    return pl.pallas_call(
        _ada_kernel,
        grid=(cols // tn,),
        in_specs=[pl.BlockSpec((n, d), lambda j: (0, 0)),
                  pl.BlockSpec((d, tn), lambda j: (0, j)),
                  pl.BlockSpec((1, tn), lambda j: (0, j))],
        out_specs=pl.BlockSpec((n, tn), lambda j: (0, j)),
        out_shape=jax.ShapeDtypeStruct((n, cols), F32),
        compiler_params=_cparams(("arbitrary",)),
        name="ada_mod",
    )(cond, w, b.reshape(1, cols))


def _in_proj_kernel(x_ref, g_ref, sh_ref, sc_ref, wg_ref, wr_ref, pg_ref, pr_ref):
    x = x_ref[...]
    y = x * lax.rsqrt(jnp.mean(x * x, axis=-1, keepdims=True) + NORM_EPS) * g_ref[...]
    h = (y * (1.0 + sc_ref[...]) + sh_ref[...]).astype(BF16)
    pg_ref[...] = jnp.dot(h, wg_ref[...], preferred_element_type=F32)
    pr_ref[...] = jnp.dot(h, wr_ref[...], preferred_element_type=F32)


def in_proj(xin, norm_g, shift, scale, w_gdn, w_rwkv, n_ctx, tm):
    bsz, t, d = xin.shape
    cg, cr = w_gdn.shape[1], w_rwkv.shape[1]
    nctx_tiles = n_ctx // tm

    def mod_map(b, i):
        return (b, jnp.where(i < nctx_tiles, 0, 1), 0, 0)

    return pl.pallas_call(
        _in_proj_kernel,
        grid=(bsz, t // tm),
        in_specs=[pl.BlockSpec((None, tm, d), lambda b, i: (b, i, 0)),
                  pl.BlockSpec((1, d), lambda b, i: (0, 0)),
                  pl.BlockSpec((None, None, 1, d), mod_map),
                  pl.BlockSpec((None, None, 1, d), mod_map),
                  pl.BlockSpec((d, cg), lambda b, i: (0, 0)),
                  pl.BlockSpec((d, cr), lambda b, i: (0, 0))],
        out_specs=[pl.BlockSpec((None, tm, cg), lambda b, i: (b, i, 0)),
                   pl.BlockSpec((None, tm, cr), lambda b, i: (b, i, 0))],
        out_shape=[jax.ShapeDtypeStruct((bsz, t, cg), F32),
                   jax.ShapeDtypeStruct((bsz, t, cr), F32)],
        compiler_params=_cparams(("parallel", "parallel")),
        name="in_proj",
    )(xin, norm_g.reshape(1, d), shift, scale, w_gdn, w_rwkv)


def _gdn_prep_kernel(p_ref, w_ref, o_ref, *, n_ctx):
    j = pl.program_id(1)
    x = p_ref[...]
    t_len = x.shape[0]
    w = w_ref[...]
    t = lax.broadcasted_iota(jnp.int32, x.shape, 0)
    is_ctx = t < n_ctx
    lo = jnp.where(is_ctx, 0, n_ctx)
    hi = jnp.where(is_ctx, n_ctx, t_len)
    pad = (GDN_CONV - 1) // 2
    acc = x * w[pad:pad + 1, :]
    for s in range(-pad, pad + 1):
        if s == 0:
            continue
        xs = pltpu.roll(x, (-s) % t_len, 0)
        valid = (t + s >= lo) & (t + s < hi)
        acc = acc + jnp.where(valid, xs, 0.0) * w[s + pad:s + pad + 1, :]
    y = _silu(acc)
    inv = lax.rsqrt(jnp.sum(y * y, axis=-1, keepdims=True) + L2_EPS)
    fac = jnp.where(j < GDN_HEADS, inv * (GDN_DK ** -0.5), jnp.where(j < 2 * GDN_HEADS, inv, 1.0))
    o_ref[...] = y * fac


def gdn_prep(p_gdn, conv_w, n_ctx):
    bsz, t, _ = p_gdn.shape
    ncol = conv_w.shape[1]
    return pl.pallas_call(
        functools.partial(_gdn_prep_kernel, n_ctx=n_ctx),
        grid=(bsz, ncol // LANES),
        in_specs=[pl.BlockSpec((None, t, LANES), lambda b, j: (b, 0, j)),
                  pl.BlockSpec((GDN_CONV, LANES), lambda b, j: (0, j))],
        out_specs=pl.BlockSpec((None, t, LANES), lambda b, j: (b, 0, j)),
        out_shape=jax.ShapeDtypeStruct((bsz, t, ncol), F32),
        compiler_params=_cparams(("parallel", "parallel")),
        name="gdn_prep",
    )(p_gdn, conv_w)


def _gdn_chunk_kernel(qf_ref, qb_ref, gf_ref, gb_ref, prm_ref, of_ref, ob_ref, s_ref):
    @pl.when(pl.program_id(1) == 0)
    def _():
        s_ref[...] = jnp.zeros_like(s_ref)

    nh, dk = GDN_HEADS, GDN_DK
    nb = qf_ref.shape[0]
    prm = prm_ref[...]
    a_log, dt_bias = prm[0:1, :], prm[1:2, :]
    q, k, v, beta, g_col, g_last, gamma, strict = [], [], [], [], [], [], [], []
    for bi, d in [(bi, d) for bi in range(nb) for d in range(2)]:
        qkv = (qf_ref, qb_ref)[d][bi]
        gates = (gf_ref, gb_ref)[d][bi]
        c = qkv.shape[0]
        incl, strict_d = _order_masks(c, d == 1)
        g_all = -jnp.exp(a_log) * _softplus(gates + dt_bias)
        beta_all = _sigmoid(gates)
        big_g = _dot_01x(incl.astype(F32), g_all)
        big_g_t = big_g.T
        last = 0 if d == 1 else c - 1
        for h in range(nh):
            col = d * nh + h
            q.append(qkv[:, h * dk:(h + 1) * dk])
            k.append(qkv[:, (nh + h) * dk:(nh + h + 1) * dk])
            v.append(qkv[:, (2 * nh + h) * dk:(2 * nh + h + 1) * dk])
            beta.append(beta_all[:, 2 * nh + col:2 * nh + col + 1])
            g_col.append(big_g[:, col:col + 1])
            g_last.append(big_g[last:last + 1, col:col + 1])
            g_row = big_g_t[col:col + 1, :]
            gamma.append(jnp.where(incl, jnp.exp(jnp.where(incl, g_col[-1] - g_row, 0.0)), 0.0))
            strict.append(strict_d)
    n = len(q)
    c = q[0].shape[0]
    e_g = [jnp.exp(g) for g in g_col]
    kq = [_dot_nt(jnp.concatenate([k[i], q[i]], axis=0), k[i]) for i in range(n)]
    a_mat = [jnp.where(strict[i], beta[i] * kq[i][:c] * gamma[i], 0.0) for i in range(n)]
    qk = [kq[i][c:] * gamma[i] for i in range(n)]
    t_inv = _unit_lower_inverses([-a for a in a_mat])
    uw = [_dot(t_inv[i], jnp.concatenate([beta[i] * v[i], (beta[i] * e_g[i]) * k[i]], axis=1))
          for i in range(n)]
    s = [s_ref[i // nh, i % nh] for i in range(n)]
    ws = [_dot(jnp.concatenate([uw[i][:, dk:], q[i] * e_g[i]], axis=0), s[i]) for i in range(n)]
    v_new = [uw[i][:, :dk] - ws[i][:c] for i in range(n)]
    o = [ws[i][c:] + _dot(qk[i], v_new[i]) for i in range(n)]
    s_new = [s[i] * jnp.exp(g_last[i]) + _dot_tn(k[i] * jnp.exp(g_last[i] - g_col[i]), v_new[i])
             for i in range(n)]
    for i in range(n):
        s_ref[i // nh, i % nh] = s_new[i]
    for bi in range(nb):
        base = 2 * nh * bi
        of_ref[bi] = jnp.concatenate(o[base:base + nh], axis=-1)
        ob_ref[bi] = jnp.concatenate(o[base + nh:base + 2 * nh], axis=-1)


BATCH_PER_STEP = 2


def _rev_chunk_map(n_ctx_chunks, n_chunks):
    def cb(s):
        return jnp.where(s < n_ctx_chunks, n_ctx_chunks - 1 - s, n_chunks + n_ctx_chunks - 1 - s)
    return cb


def gdn_chunk(qkv, p_gdn, prm, n_ctx):
    bsz, t, ncol = qkv.shape
    nv = GDN_HEADS * GDN_DK
    n_chunks = t // CHUNK
    cb = _rev_chunk_map(n_ctx // CHUNK, n_chunks)
    gate_tile = p_gdn.shape[2] // LANES - 1
    nb = BATCH_PER_STEP if bsz % BATCH_PER_STEP == 0 else 1
    return pl.pallas_call(
        _gdn_chunk_kernel,
        grid=(bsz // nb, n_chunks),
        in_specs=[pl.BlockSpec((nb, CHUNK, ncol), lambda b, s: (b, s, 0)),
                  pl.BlockSpec((nb, CHUNK, ncol), lambda b, s: (b, cb(s), 0)),
                  pl.BlockSpec((nb, CHUNK, LANES), lambda b, s: (b, s, gate_tile)),
                  pl.BlockSpec((nb, CHUNK, LANES), lambda b, s: (b, cb(s), gate_tile)),
                  pl.BlockSpec((8, LANES), lambda b, s: (0, 0))],
        out_specs=[pl.BlockSpec((nb, CHUNK, nv), lambda b, s: (b, s, 0)),
                   pl.BlockSpec((nb, CHUNK, nv), lambda b, s: (b, cb(s), 0))],
        out_shape=[jax.ShapeDtypeStruct((bsz, t, nv), F32)] * 2,
        scratch_shapes=[pltpu.VMEM((2 * nb, GDN_HEADS, GDN_DK, GDN_DK), F32)],
        compiler_params=_cparams(("parallel", "arbitrary")),
        name="gdn_chunk",
    )(qkv, qkv, p_gdn, p_gdn, prm)


def _rwkv_shift_kernel(p_ref, mu_ref, o_ref, *, n_ctx, n_cols):
    j = pl.program_id(1)
    x = p_ref[...]
    t_len = x.shape[0]
    t = lax.broadcasted_iota(jnp.int32, x.shape, 0)
    ch = lax.broadcasted_iota(jnp.int32, x.shape, 1) + j * LANES
    is_ctx = t < n_ctx
    col = (t - n_ctx) % GRID_W
    prev1 = pltpu.roll(x, 1, 0)
    next1 = pltpu.roll(x, t_len - 1, 0)
    up = pltpu.roll(x, GRID_W, 0)
    down = pltpu.roll(x, t_len - GRID_W, 0)
    quarter = n_cols // 4
    half = n_cols // 2
    ctx_sh = jnp.where(ch < half,
                       jnp.where(t >= 1, prev1, 0.0),
                       jnp.where(t < n_ctx - 1, next1, 0.0))
    lat_sh = jnp.where(ch < quarter, jnp.where(col >= 1, prev1, 0.0),
                       jnp.where(ch < 2 * quarter, jnp.where(col < GRID_W - 1, next1, 0.0),
                                 jnp.where(ch < 3 * quarter,
                                           jnp.where(t >= n_ctx + GRID_W, up, 0.0),
                                           jnp.where(t < t_len - GRID_W, down, 0.0))))
    sh = jnp.where(is_ctx, ctx_sh, lat_sh)
    o_ref[...] = x + mu_ref[...] * (sh - x)


def rwkv_shift(p_rwkv, mu, n_ctx):
    bsz, t, ncol = p_rwkv.shape
    return pl.pallas_call(
        functools.partial(_rwkv_shift_kernel, n_ctx=n_ctx, n_cols=ncol),
        grid=(bsz, ncol // LANES),
        in_specs=[pl.BlockSpec((None, t, LANES), lambda b, j: (b, 0, j)),
                  pl.BlockSpec((1, LANES), lambda b, j: (0, j))],
        out_specs=pl.BlockSpec((None, t, LANES), lambda b, j: (b, 0, j)),
        out_shape=jax.ShapeDtypeStruct((bsz, t, ncol), F32),
        compiler_params=_cparams(("parallel", "parallel")),
        name="rwkv_shift",
    )(p_rwkv, mu.reshape(1, ncol))


def _rwkv_gates(xs, d, w0_ref, w2_ref, a0_ref, a2_ref):
    w_ = RWKV_W
    xw = xs[:, 3 * w_ + d * LORA:3 * w_ + (d + 1) * LORA]
    xa = xs[:, 3 * w_ + 2 * LORA + d * LORA:3 * w_ + 2 * LORA + (d + 1) * LORA]
    w_pre = w0_ref[d:d + 1, :] + _dot(jnp.tanh(xw), w2_ref[d])
    log_w = -math.exp(-0.5) * _sigmoid(w_pre)
    a = _sigmoid(a0_ref[d:d + 1, :] + _dot(xa, a2_ref[d]))
    return log_w, a


def _rwkv_chunk_kernel(xf_ref, xb_ref, vec_ref, w0_ref, w2_ref, a0_ref, a2_ref, hsum_ref,
                       yf_ref, yb_ref, p_ref):
    @pl.when(pl.program_id(1) == 0)
    def _():
        p_ref[...] = jnp.zeros_like(p_ref)

    w_, n, nh = RWKV_W, RWKV_HD, RWKV_HEADS
    nb = xf_ref.shape[0]
    k_k, k_a = vec_ref[0:1, :], vec_ref[1:2, :]
    lhs, rhs_t, tail, vals, w_last, p0, strict, incl = [], [], [], [], [], [], [], []
    for bi, d in [(bi, d) for bi in range(nb) for d in range(2)]:
        xs = (xf_ref, xb_ref)[d][bi]
        c = xs.shape[0]
        incl_d, strict_d = _order_masks(c, d == 1)
        r = xs[:, 0:w_]
        k = xs[:, w_:2 * w_]
        v = xs[:, 2 * w_:3 * w_]
        log_w, a = _rwkv_gates(xs, d, w0_ref, w2_ref, a0_ref, a2_ref)
        kk = k * k_k
        kk = kk * lax.rsqrt(_dot_x01(kk * kk, hsum_ref[...]) + L2_EPS)
        k_dir = k * (1.0 + (a - 1.0) * k_a)
        alpha = -(kk * a)
        big_g = _dot_01x(incl_d.astype(F32), log_w)
        last = 0 if d == 1 else c - 1
        g_last = big_g[last:last + 1, :]
        e_neg = jnp.exp(-big_g)
        e_tail = jnp.exp(g_last - big_g)
        r_t = r * jnp.exp(big_g)
        b_t = kk * jnp.exp(big_g - log_w)
        a_t = alpha * e_neg
        k_t = k_dir * e_neg
        a_h = alpha * e_tail
        k_h = k_dir * e_tail
        w_l = jnp.exp(g_last)
        for h in range(nh):
            sl = slice(h * n, (h + 1) * n)
            lhs.append(jnp.concatenate([b_t[:, sl], r_t[:, sl]], axis=0))
            rhs_t.append(jnp.concatenate([a_t[:, sl], k_t[:, sl]], axis=0))
            tail.append(jnp.concatenate([a_h[:, sl], k_h[:, sl]], axis=0))
            vals.append(v[:, sl])
            w_last.append(w_l[:, sl])
            p0.append(p_ref[2 * bi + d, h])
            strict.append(strict_d)
            incl.append(incl_d)
    m = len(lhs)
    c = vals[0].shape[0]
    quad = [_dot_nt(lhs[i], rhs_t[i]) for i in range(m)]
    a_ba = [jnp.where(strict[i], quad[i][:c, :c], 0.0) for i in range(m)]
    a_k = [jnp.concatenate([jnp.where(strict[i], quad[i][:c, c:], 0.0),
                            jnp.where(incl[i], quad[i][c:, c:], 0.0)], axis=0) for i in range(m)]
    a_ra = [jnp.where(incl[i], quad[i][c:, :c], 0.0) for i in range(m)]
    from_state = [_dot_nt(lhs[i], p0[i]) for i in range(m)]
    from_vals = [_dot(a_k[i], vals[i]) for i in range(m)]
    t_inv = _unit_lower_inverses(a_ba)
    u = [_dot(t_inv[i], from_state[i][:c] + from_vals[i][:c]) for i in range(m)]
    y = [from_state[i][c:] + _dot(a_ra[i], u[i]) + from_vals[i][c:] for i in range(m)]
    p_new = [p0[i] * w_last[i] + _dot_tn(jnp.concatenate([u[i], vals[i]], axis=0), tail[i])
             for i in range(m)]
    for i in range(m):
        p_ref[i // nh, i % nh] = p_new[i]
    for bi in range(nb):
        base = 2 * nh * bi
        yf_ref[bi] = jnp.concatenate(y[base:base + nh], axis=-1)
        yb_ref[bi] = jnp.concatenate(y[base + nh:base + 2 * nh], axis=-1)


def _head_sum_matrix(width, group):
    i = jnp.arange(width)
    return (i[:, None] // group == i[None, :] // group).astype(F32)


def rwkv_chunk(xs, vecs, w0, w2, a0, a2, n_ctx):
    bsz, t, ncol = xs.shape
    n_chunks = t // CHUNK
    cb = _rev_chunk_map(n_ctx // CHUNK, n_chunks)
    hsum = _head_sum_matrix(RWKV_W, RWKV_HD)
    full = lambda *shape: pl.BlockSpec(shape, lambda b, s: (0,) * len(shape))
    nb = BATCH_PER_STEP if bsz % BATCH_PER_STEP == 0 else 1
    return pl.pallas_call(
        _rwkv_chunk_kernel,
        grid=(bsz // nb, n_chunks),
        in_specs=[pl.BlockSpec((nb, CHUNK, ncol), lambda b, s: (b, s, 0)),
                  pl.BlockSpec((nb, CHUNK, ncol), lambda b, s: (b, cb(s), 0)),
                  full(8, RWKV_W), full(2, RWKV_W), full(2, LORA, RWKV_W),
                  full(2, RWKV_W), full(2, LORA, RWKV_W), full(RWKV_W, RWKV_W)],
        out_specs=[pl.BlockSpec((nb, CHUNK, RWKV_W), lambda b, s: (b, s, 0)),
                   pl.BlockSpec((nb, CHUNK, RWKV_W), lambda b, s: (b, cb(s), 0))],
        out_shape=[jax.ShapeDtypeStruct((bsz, t, RWKV_W), F32)] * 2,
        scratch_shapes=[pltpu.VMEM((2 * nb, RWKV_HEADS, RWKV_HD, RWKV_HD), F32)],
        compiler_params=_cparams(("parallel", "arbitrary")),
        name="rwkv_chunk",
    )(xs, xs, vecs, w0, w2, a0, a2, hsum)


def _mix_out_kernel(x_ref, gt_ref, of_ref, ob_ref, z_ref, yf_ref, yb_ref, xs_ref,
                    gnw_ref, vec_ref, a0_ref, a2_ref, g2_ref, hsum_ref, wo_ref, o_ref):
    nh, dv = GDN_HEADS, GDN_DK
    w_ = RWKV_W
    o = of_ref[...] + ob_ref[...]
    z = z_ref[...]
    parts = []
    for h in range(nh):
        oh = o[:, h * dv:(h + 1) * dv]
        oh = oh * lax.rsqrt(jnp.mean(oh * oh, axis=-1, keepdims=True) + NORM_EPS)
        parts.append(oh * gnw_ref[...] * _silu(z[:, h * dv:(h + 1) * dv]))
    xs = xs_ref[...]
    r = xs[:, 0:w_]
    k = xs[:, w_:2 * w_]
    v = xs[:, 2 * w_:3 * w_]
    xg = xs[:, 3 * w_ + 4 * LORA:]
    k_a, r_k = vec_ref[1:2, :], vec_ref[2:3, :]
    gn_w, gn_b = vec_ref[3:4, :], vec_ref[4:5, :]
    hsum = hsum_ref[...]
    y = yf_ref[...] + yb_ref[...]
    mean = _dot_x01(y, hsum) * (1.0 / RWKV_HD)
    yc = y - mean
    var = _dot_x01(yc * yc, hsum) * (1.0 / RWKV_HD)
    yn = yc * lax.rsqrt(var + RWKV_GN_EPS) * gn_w + gn_b
    rk = jnp.zeros_like(r)
    for d in range(2):
        xa = xs[:, 3 * w_ + 2 * LORA + d * LORA:3 * w_ + 2 * LORA + (d + 1) * LORA]
        a = _sigmoid(a0_ref[d:d + 1, :] + _dot(xa, a2_ref[d]))
        rk = rk + r * (k * (1.0 + (a - 1.0) * k_a)) * r_k
    bonus = _dot_x01(rk, hsum) * v
    gate = _dot(_sigmoid(xg), g2_ref[...])
    parts.append((yn + bonus) * gate)
    mixed = jnp.concatenate(parts, axis=-1).astype(BF16)
    o_ref[...] = x_ref[...] + gt_ref[...] * jnp.dot(mixed, wo_ref[...], preferred_element_type=F32)


def mix_out(x, gt1, o_f, o_b, p_gdn, y_f, y_b, xs, gnw, vecs, a0, a2, g2, w_out, n_ctx, tm):
    bsz, seq, d = x.shape
    off = n_ctx // tm
    nv = GDN_HEADS * GDN_DK
    ncol = xs.shape[2]
    hsum = _head_sum_matrix(RWKV_W, RWKV_HD)
    lat = lambda width, cblk=0: pl.BlockSpec((None, tm, width), lambda b, i: (b, i + off, cblk))
    full = lambda *shape: pl.BlockSpec(shape, lambda b, i: (0,) * len(shape))
    return pl.pallas_call(
        _mix_out_kernel,
        grid=(bsz, seq // tm),
        in_specs=[pl.BlockSpec((None, tm, d), lambda b, i: (b, i, 0)),
                  pl.BlockSpec((None, 1, d), lambda b, i: (b, 0, 0)),
                  lat(nv), lat(nv), lat(nv, 3), lat(RWKV_W), lat(RWKV_W), lat(ncol),
                  full(1, GDN_DK), full(8, RWKV_W), full(2, RWKV_W), full(2, LORA, RWKV_W),
                  full(2 * LORA, RWKV_W), full(RWKV_W, RWKV_W), full(nv + RWKV_W, d)],
        out_specs=pl.BlockSpec((None, tm, d), lambda b, i: (b, i, 0)),
        out_shape=jax.ShapeDtypeStruct((bsz, seq, d), F32),
        compiler_params=_cparams(("parallel", "parallel")),
        name="mix_out",
    )(x, gt1, o_f, o_b, p_gdn, y_f, y_b, xs, gnw, vecs, a0, a2, g2, hsum, w_out)


def _extract_top(s, riota, n_rows, vals_ref, idx_ref, k):
    for i in range(k):
        m = jnp.max(s, axis=0, keepdims=True)
        am = jnp.min(jnp.where(s == m, riota, float(n_rows)), axis=0, keepdims=True)
        vals_ref[i:i + 1, :] = m
        idx_ref[i:i + 1, :] = am
        s = jnp.where(riota == am, NEG_INF, s)


def _candidate_blocks(k):
    split = math.isqrt(k)
    blocks = [(False, i, 0, k // (i + 1)) for i in range(split)]
    j = 0
    while k // (j + 1) > split:
        blocks.append((True, j, split, k // (j + 1)))
        j += 1
    return blocks


def _candidate_rows(k):
    return sum(-(-(hi - lo) // 8) * 8 for _, _, lo, hi in _candidate_blocks(k))


def _top_list_rows(k):
    need = max(lo + -(-(hi - lo) // 8) * 8 for _, _, lo, hi in _candidate_blocks(k))
    return -(-need // 8) * 8


def _peer_route_kernel(x_ref, g_ref, sh_ref, sc_ref, wq_ref, keys_ref,
                       h2_ref, idx_ref, gate_ref,
                       q_scr, tv_scr, ti_scr, cs_scr, ci_scr, cf_scr, bv_scr, bi_scr):
    h = pl.program_id(1)
    dq = 2 * LANES
    kk = PEER_TOPK

    @pl.when(h == 0)
    def _():
        x = x_ref[...]
        y = x * lax.rsqrt(jnp.mean(x * x, axis=-1, keepdims=True) + NORM_EPS) * g_ref[...]
        h2 = y * (1.0 + sc_ref[...]) + sh_ref[...]
        h2_ref[...] = h2
        hb = h2.astype(BF16)
        for hh in range(PEER_HEADS):
            q_scr[hh] = jnp.dot(hb, wq_ref[:, hh * dq:(hh + 1) * dq], preferred_element_type=F32)

    q = q_scr[h]
    tm = q.shape[0]
    riota = lax.broadcasted_iota(jnp.int32, (PEER_NKEYS, tm), 0).astype(F32)
    for p in range(2):
        s_t = _dot_nt(keys_ref[p], q[:, p * LANES:(p + 1) * LANES])
        _extract_top(s_t, riota, PEER_NKEYS, tv_scr.at[p], ti_scr.at[p], kk)
    tv_scr[:, kk:, :] = jnp.zeros_like(tv_scr[:, kk:, :])
    ti_scr[:, kk:, :] = jnp.zeros_like(ti_scr[:, kk:, :])
    row = 0
    for fixed_j, fixed, lo, hi in _candidate_blocks(kk):
        n_pad = -(-(hi - lo) // 8) * 8
        r = lax.broadcasted_iota(jnp.int32, (n_pad, tm), 0).astype(F32)
        run, one = (0, 1) if fixed_j else (1, 0)
        vals = tv_scr[run, lo:lo + n_pad, :] + tv_scr[one, fixed:fixed + 1, :]
        run_i, one_i = ti_scr[run, lo:lo + n_pad, :], ti_scr[one, fixed:fixed + 1, :]
        if fixed_j:
            experts = run_i * float(PEER_NKEYS) + one_i
            flat = (r + float(lo)) * float(kk) + float(fixed)
        else:
            experts = one_i * float(PEER_NKEYS) + run_i
            flat = float(fixed * kk + lo) + r
        cs_scr[row:row + n_pad, :] = jnp.where(r < float(hi - lo), vals, NEG_INF)
        ci_scr[row:row + n_pad, :] = experts
        cf_scr[row:row + n_pad, :] = flat
        row += n_pad
    cand = cs_scr[...]
    cand_i = ci_scr[...]
    ciota = cf_scr[...]
    for i in range(kk):
        m = jnp.max(cand, axis=0, keepdims=True)
        pos = jnp.min(jnp.where(cand == m, ciota, float(kk * kk)), axis=0, keepdims=True)
        hit = ciota == pos
        bv_scr[i:i + 1, :] = m
        bi_scr[i:i + 1, :] = jnp.sum(jnp.where(hit, cand_i, 0.0), axis=0, keepdims=True)
        cand = jnp.where(hit, NEG_INF, cand)
    best = bv_scr[...]
    e = jnp.exp(best - best[0:1, :])
    gate_ref[...] = e / jnp.sum(e, axis=0, keepdims=True)
    idx_ref[...] = bi_scr[...].astype(jnp.int32)


def peer_route(x1, norm_g, sh2, sc2, wq, sub_keys, tm):
    bsz, seq, d = x1.shape
    ntok = bsz * seq
    tiles_per_b = seq // tm
    kk = PEER_TOPK
    nq = wq.shape[1]
    return pl.pallas_call(
        _peer_route_kernel,
        grid=(ntok // tm, PEER_HEADS),
        in_specs=[pl.BlockSpec((tm, d), lambda i, h: (i, 0)),
                  pl.BlockSpec((1, d), lambda i, h: (0, 0)),
                  pl.BlockSpec((None, 1, d), lambda i, h: (i // tiles_per_b, 0, 0)),
                  pl.BlockSpec((None, 1, d), lambda i, h: (i // tiles_per_b, 0, 0)),
                  pl.BlockSpec((d, nq), lambda i, h: (0, 0)),
                  pl.BlockSpec((None, 2, PEER_NKEYS, LANES), lambda i, h: (h, 0, 0, 0))],
        out_specs=[pl.BlockSpec((tm, d), lambda i, h: (i, 0)),
                   pl.BlockSpec((kk, tm), lambda i, h: (h, i)),
                   pl.BlockSpec((kk, tm), lambda i, h: (h, i))],
        out_shape=[jax.ShapeDtypeStruct((ntok, d), F32),
                   jax.ShapeDtypeStruct((PEER_HEADS * kk, ntok), jnp.int32),
                   jax.ShapeDtypeStruct((PEER_HEADS * kk, ntok), F32)],
        scratch_shapes=[pltpu.VMEM((PEER_HEADS, tm, 2 * LANES), F32),
                        pltpu.VMEM((2, _top_list_rows(kk), tm), F32),
                        pltpu.VMEM((2, _top_list_rows(kk), tm), F32),
                        pltpu.VMEM((_candidate_rows(kk), tm), F32),
                        pltpu.VMEM((_candidate_rows(kk), tm), F32),
                        pltpu.VMEM((_candidate_rows(kk), tm), F32),
                        pltpu.VMEM((kk, tm), F32), pltpu.VMEM((kk, tm), F32)],
        compiler_params=_cparams(("parallel", "arbitrary")),
        name="peer_route",
    )(x1.reshape(ntok, d), norm_g.reshape(1, d), sh2, sc2, wq, sub_keys)


GROUP = 4
N_GSLOTS = 4
AHEAD = 2
N_SLOTS = GROUP * N_GSLOTS
ROW_TILE = 8
DMA_QUEUES = 2


def _peer_expert_kernel(idx_ref, h2_ref, gate_ref, x1_ref, gt_ref, fg_ref, tbl_ref,
                        o_ref, *scratch, tt, d):
    bufs, tiles, sem = scratch[:N_SLOTS], scratch[N_SLOTS:2 * N_SLOTS], scratch[2 * N_SLOTS]
    ne = PEER_HEADS * PEER_TOPK
    step = pl.program_id(0)
    last_step = pl.num_programs(0) - 1

    def issue_token(base, gslot, j):
        slot = gslot * GROUP + j
        for k in range(ne):
            e = idx_ref[base + k]
            pltpu.make_async_copy(tbl_ref.at[e], bufs[slot].at[k],
                                  sem.at[gslot]).start(priority=k % DMA_QUEUES)

    def wait_group(gslot):
        for j in range(GROUP):
            pltpu.make_async_copy(tbl_ref.at[pl.ds(0, ne)], bufs[gslot * GROUP + j],
                                  sem.at[gslot]).wait()

    def evaluate(t, slot):
        tile = tiles[slot]
        tile[...] = bufs[slot][...].reshape(ne, 2 * d)
        hrow = h2_ref[pl.ds(t, 1), :]
        grow = gate_ref[pl.ds(t, 1), :]
        lane = lax.broadcasted_iota(jnp.int32, (ROW_TILE, ne), 1)
        sub = lax.broadcasted_iota(jnp.int32, (ROW_TILE, ne), 0)
        acc = jnp.zeros((ROW_TILE, d), F32)
        for g in range(ne // ROW_TILE):
            rows = tile[g * ROW_TILE:(g + 1) * ROW_TILE, :]
            pre = jnp.sum(rows[:, :d] * hrow, axis=1, keepdims=True)
            act = 0.5 * pre * (1.0 + lax.erf(pre * (2.0 ** -0.5)))
            gcol = jnp.sum(jnp.where(lane == sub + g * ROW_TILE, grow, 0.0), axis=1, keepdims=True)
            acc = acc + (act * gcol) * rows[:, d:]
        y = jnp.sum(acc, axis=0, keepdims=True)
        xo = x1_ref[pl.ds(t, 1), :] + gt_ref[...] * y
        xo = xo * lax.rsqrt(jnp.mean(xo * xo, axis=-1, keepdims=True) + NORM_EPS) * fg_ref[...]
        o_ref[pl.ds(t, 1), :] = xo

    @pl.when(step == 0)
    def _():
        for g in range(AHEAD):
            for j in range(GROUP):
                issue_token((g * GROUP + j) * ne, g, j)

    def ring_turn(it, carry):
        for gs in range(N_GSLOTS):
            t0 = (it * N_GSLOTS + gs) * GROUP
            wait_group(gs)
            for j in range(GROUP):
                issue_token((t0 + AHEAD * GROUP + j) * ne, (gs + AHEAD) % N_GSLOTS, j)
                evaluate(t0 + j, gs * GROUP + j)
        return carry

    lax.fori_loop(0, tt // N_SLOTS, ring_turn, 0)

    @pl.when(step == last_step)
    def _():
        for g in range(AHEAD):
            wait_group(g)


def peer_expert(idx_tok, h2, gate_tok, x1, gt2, final_g, table, seq, tt):
    ntok, d = h2.shape
    ne = PEER_HEADS * PEER_TOPK
    tiles_per_b = seq // tt
    n_steps = ntok // tt
    assert tt % N_SLOTS == 0
    idx_steps = idx_tok.reshape(n_steps, tt * ne)
    tail = jnp.roll(idx_steps[:, :AHEAD * GROUP * ne], -1, axis=0)
    win = (tt + AHEAD * GROUP) * ne
    idx_win = jnp.concatenate([idx_steps, tail], axis=1).reshape(n_steps * win)
    return pl.pallas_call(
        functools.partial(_peer_expert_kernel, tt=tt, d=d),
        grid=(n_steps,),
        in_specs=[pl.BlockSpec((win,), lambda i: (i,), memory_space=pltpu.SMEM),
                  pl.BlockSpec((tt, d), lambda i: (i, 0)),
                  pl.BlockSpec((tt, ne), lambda i: (i, 0)),
                  pl.BlockSpec((tt, d), lambda i: (i, 0)),
                  pl.BlockSpec((None, 1, d), lambda i: (i // tiles_per_b, 0, 0)),
                  pl.BlockSpec((1, d), lambda i: (0, 0)),
                  pl.BlockSpec(memory_space=pl.ANY)],
        out_specs=pl.BlockSpec((tt, d), lambda i: (i, 0)),
        out_shape=jax.ShapeDtypeStruct((ntok, d), F32),
        scratch_shapes=([pltpu.VMEM((ne, 1, 2 * d), F32) for _ in range(N_SLOTS)]
                        + [pltpu.VMEM((ne, 2 * d), F32) for _ in range(N_SLOTS)]
                        + [pltpu.SemaphoreType.DMA((N_GSLOTS,))]),
        compiler_params=_cparams(("arbitrary",)),
        name="peer_expert",
    )(idx_win, h2, gate_tok, x1, gt2, final_g.reshape(1, d), table)


ROW_TILE_TOKENS = 256
ROUTE_TILE_TOKENS = 512
GATHER_STEP_TOKENS = 128


def kernel(x, c, ctx, c_ctx, ada_w, ada_b, norm1_g, w_in, gdn_conv_w, gdn_a_log, gdn_dt_bias,
           gdn_norm_w, rwkv_mu, rwkv_w0, rwkv_w2, rwkv_a0, rwkv_a2, rwkv_g2, rwkv_k_k, rwkv_k_a,
           rwkv_r_k, rwkv_gn_w, rwkv_gn_b, w_out, norm2_g, peer_w_query, peer_sub_keys,
           peer_down, peer_up, final_norm_g):
    bsz, seq, d = x.shape
    n_ctx = ctx.shape[1]
    assert ada_w.shape[0] == 1, "single-layer block"
    assert n_ctx % ROW_TILE_TOKENS == 0 and seq % ROUTE_TILE_TOKENS == 0 and seq % GRID_W == 0
    nh = GDN_HEADS
    n_qkvz = 4 * nh * GDN_DK
    gdn_cols = n_qkvz + 4 * nh

    cond = jnp.concatenate([c, c_ctx[None, :], jnp.zeros((16 - bsz - 1, d), F32)], axis=0)
    mods = ada_mod(cond, ada_w[0], ada_b[0])
    m_lat = mods[:bsz].reshape(bsz, N_MOD, 1, d)
    m_ctx = jnp.broadcast_to(mods[bsz].reshape(1, N_MOD, 1, d), (bsz, N_MOD, 1, d))
    shift1 = jnp.stack([m_ctx[:, 0], m_lat[:, 0]], axis=1)
    scale1 = jnp.stack([m_ctx[:, 1], m_lat[:, 1]], axis=1)
    gt1, sh2, sc2, gt2 = m_lat[:, 2], m_lat[:, 3], m_lat[:, 4], m_lat[:, 5]

    wl = w_in[0]
    w_gdn = jnp.concatenate([wl[:, :gdn_cols], jnp.zeros((d, LANES - 4 * nh), F32)], axis=1)
    w_rwkv = wl[:, gdn_cols:]
    xin = jnp.concatenate([ctx, x], axis=1)
    p_gdn, p_rwkv = in_proj(xin, norm1_g[0], shift1, scale1, w_gdn.astype(BF16),
                            w_rwkv.astype(BF16), n_ctx, ROW_TILE_TOKENS)

    qkv = gdn_prep(p_gdn, gdn_conv_w[0], n_ctx)
    prm = jnp.zeros((8, LANES), F32)
    prm = prm.at[0, :2 * nh].set(gdn_a_log[0].reshape(-1))
    prm = prm.at[1, :2 * nh].set(gdn_dt_bias[0].reshape(-1))
    o_f, o_b = gdn_chunk(qkv, p_gdn, prm, n_ctx)

    xs = rwkv_shift(p_rwkv, rwkv_mu[0], n_ctx)
    vecs = jnp.zeros((8, RWKV_W), F32)
    for i, vec in enumerate((rwkv_k_k, rwkv_k_a, rwkv_r_k, rwkv_gn_w, rwkv_gn_b)):
        vecs = vecs.at[i].set(vec[0])
    y_f, y_b = rwkv_chunk(xs, vecs, rwkv_w0[0], rwkv_w2[0], rwkv_a0[0], rwkv_a2[0], n_ctx)

    x1 = mix_out(x, gt1, o_f, o_b, p_gdn, y_f, y_b, xs, gdn_norm_w[0].reshape(1, GDN_DK), vecs,
                 rwkv_a0[0], rwkv_a2[0], rwkv_g2[0], w_out[0].astype(BF16), n_ctx, ROW_TILE_TOKENS)

    h2, idx_t, gate_t = peer_route(x1, norm2_g[0], sh2, sc2, peer_w_query[0].astype(BF16),
                                   peer_sub_keys[0], ROUTE_TILE_TOKENS)
    table = jnp.concatenate([peer_down[0][:, None, :], peer_up[0][:, None, :]], axis=2)
    out = peer_expert(idx_t.T, h2, gate_t.T, x1.reshape(bsz * seq, d), gt2,
                      final_norm_g, table, seq, GATHER_STEP_TOKENS)
    return out.reshape(bsz, seq, d)
```

```python
import functools
import math

import jax
import jax.numpy as jnp
from jax import lax
from jax.experimental import pallas as pl
from jax.experimental.pallas import tpu as pltpu

F32 = jnp.float32
BF16 = jnp.bfloat16

GRID_W = 64
GDN_HEADS = 4
GDN_DK = 128
GDN_CONV = 5
CHUNK = 64
RWKV_HEADS = 8
RWKV_HD = 64
RWKV_W = RWKV_HEADS * RWKV_HD
LORA = 64
PEER_HEADS = 8
PEER_NKEYS = 128
PEER_TOPK = 16
N_MOD = 6

NORM_EPS = 1e-6
L2_EPS = 1e-6
RWKV_GN_EPS = 64e-5

LANES = 128
VMEM_LIMIT = 56 * 1024 * 1024
NEG_INF = float("-inf")


def _cparams(sem):
    return pltpu.CompilerParams(dimension_semantics=sem, vmem_limit_bytes=VMEM_LIMIT)


def _dot(a, b):
    return jnp.dot(a.astype(BF16), b.astype(BF16), preferred_element_type=F32)


def _dot_nt(a, b):
    return lax.dot_general(a.astype(BF16), b.astype(BF16), (((1,), (1,)), ((), ())),
                           preferred_element_type=F32)


def _dot_tn(a, b):
    return lax.dot_general(a.astype(BF16), b.astype(BF16), (((0,), (0,)), ((), ())),
                           preferred_element_type=F32)


def _dot_hi(a, b):
    return jnp.dot(a, b, preferred_element_type=F32, precision=lax.Precision.HIGHEST)


def _split3(x):
    hi = x.astype(BF16)
    rest = x - hi.astype(F32)
    mid = rest.astype(BF16)
    lo = (rest - mid.astype(F32)).astype(BF16)
    return hi, mid, lo


def _dot_x01(a, m01):
    m = m01.astype(BF16)
    return sum(jnp.dot(p, m, preferred_element_type=F32) for p in _split3(a))


def _dot_01x(m01, b):
    m = m01.astype(BF16)
    return sum(jnp.dot(m, p, preferred_element_type=F32) for p in _split3(b))


def _sigmoid(x):
    return 1.0 / (1.0 + jnp.exp(-x))


def _silu(x):
    return x * _sigmoid(x)


def _softplus(x):
    return jnp.maximum(x, 0.0) + jnp.log(1.0 + jnp.exp(-jnp.abs(x)))


def _unit_lower_inverses(xs):
    c = xs[0].shape[0]
    eye = (lax.broadcasted_iota(jnp.int32, (c, c), 0) ==
           lax.broadcasted_iota(jnp.int32, (c, c), 1)).astype(F32)
    prods = [eye + x for x in xs]
    pows = [_dot(x, x) for x in xs]
    for _ in range(int(math.log2(c)) - 2):
        both = [_dot(jnp.concatenate([p, x], axis=0), x) for p, x in zip(prods, pows)]
        prods = [p + b[:c] for p, b in zip(prods, both)]
        pows = [b[c:] for b in both]
    return [p + _dot(p, x) for p, x in zip(prods, pows)]


def _order_masks(c, rev):
    r = lax.broadcasted_iota(jnp.int32, (c, c), 0)
    s = lax.broadcasted_iota(jnp.int32, (c, c), 1)
    if rev:
        return s >= r, s > r
    return s <= r, s < r


def _ada_kernel(c_ref, w_ref, b_ref, o_ref):
    o_ref[...] = _dot_hi(_silu(c_ref[...]), w_ref[...]) + b_ref[...]


def ada_mod(cond, w, b):
    n, d = cond.shape
    cols = w.shape[1]
    tn = 1024
    return pl.pallas_call(
        _ada_kernel,
        grid=(cols // tn,),
        in_specs=[pl.BlockSpec((n, d), lambda j: (0, 0)),
                  pl.BlockSpec((d, tn), lambda j: (0, j)),
                  pl.BlockSpec((1, tn), lambda j: (0, j))],
        out_specs=pl.BlockSpec((n, tn), lambda j: (0, j)),
        out_shape=jax.ShapeDtypeStruct((n, cols), F32),
        compiler_params=_cparams(("arbitrary",)),
        name="ada_mod",
    )(cond, w, b.reshape(1, cols))


def _in_proj_kernel(x_ref, g_ref, sh_ref, sc_ref, wg_ref, wr_ref, pg_ref, pr_ref):
    x = x_ref[...]
    y = x * lax.rsqrt(jnp.mean(x * x, axis=-1, keepdims=True) + NORM_EPS) * g_ref[...]
    h = (y * (1.0 + sc_ref[...]) + sh_ref[...]).astype(BF16)
    pg_ref[...] = jnp.dot(h, wg_ref[...], preferred_element_type=F32)
    pr_ref[...] = jnp.dot(h, wr_ref[...], preferred_element_type=F32)


def in_proj(xin, norm_g, shift, scale, w_gdn, w_rwkv, n_ctx, tm):
    bsz, t, d = xin.shape
    cg, cr = w_gdn.shape[1], w_rwkv.shape[1]
    nctx_tiles = n_ctx // tm

    def mod_map(b, i):
        return (b, jnp.where(i < nctx_tiles, 0, 1), 0, 0)

    return pl.pallas_call(
        _in_proj_kernel,
        grid=(bsz, t // tm),
        in_specs=[pl.BlockSpec((None, tm, d), lambda b, i: (b, i, 0)),
                  pl.BlockSpec((1, d), lambda b, i: (0, 0)),
                  pl.BlockSpec((None, None, 1, d), mod_map),
                  pl.BlockSpec((None, None, 1, d), mod_map),
                  pl.BlockSpec((d, cg), lambda b, i: (0, 0)),
                  pl.BlockSpec((d, cr), lambda b, i: (0, 0))],
        out_specs=[pl.BlockSpec((None, tm, cg), lambda b, i: (b, i, 0)),
                   pl.BlockSpec((None, tm, cr), lambda b, i: (b, i, 0))],
        out_shape=[jax.ShapeDtypeStruct((bsz, t, cg), F32),
                   jax.ShapeDtypeStruct((bsz, t, cr), F32)],
        compiler_params=_cparams(("parallel", "parallel")),
        name="in_proj",
    )(xin, norm_g.reshape(1, d), shift, scale, w_gdn, w_rwkv)


def _gdn_prep_kernel(p_ref, w_ref, o_ref, *, n_ctx):
    j = pl.program_id(1)
    x = p_ref[...]
    t_len = x.shape[0]
    w = w_ref[...]
    t = lax.broadcasted_iota(jnp.int32, x.shape, 0)
    is_ctx = t < n_ctx
    lo = jnp.where(is_ctx, 0, n_ctx)
    hi = jnp.where(is_ctx, n_ctx, t_len)
    pad = (GDN_CONV - 1) // 2
    acc = x * w[pad:pad + 1, :]
    for s in range(-pad, pad + 1):
        if s == 0:
            continue
        xs = pltpu.roll(x, (-s) % t_len, 0)
        valid = (t + s >= lo) & (t + s < hi)
        acc = acc + jnp.where(valid, xs, 0.0) * w[s + pad:s + pad + 1, :]
    y = _silu(acc)
    inv = lax.rsqrt(jnp.sum(y * y, axis=-1, keepdims=True) + L2_EPS)
    fac = jnp.where(j < GDN_HEADS, inv * (GDN_DK ** -0.5), jnp.where(j < 2 * GDN_HEADS, inv, 1.0))
    o_ref[...] = y * fac


def gdn_prep(p_gdn, conv_w, n_ctx):
    bsz, t, _ = p_gdn.shape
    ncol = conv_w.shape[1]
    return pl.pallas_call(
        functools.partial(_gdn_prep_kernel, n_ctx=n_ctx),
        grid=(bsz, ncol // LANES),
        in_specs=[pl.BlockSpec((None, t, LANES), lambda b, j: (b, 0, j)),
                  pl.BlockSpec((GDN_CONV, LANES), lambda b, j: (0, j))],
        out_specs=pl.BlockSpec((None, t, LANES), lambda b, j: (b, 0, j)),
        out_shape=jax.ShapeDtypeStruct((bsz, t, ncol), F32),
        compiler_params=_cparams(("parallel", "parallel")),
        name="gdn_prep",
    )(p_gdn, conv_w)


def _gdn_chunk_kernel(qf_ref, qb_ref, gf_ref, gb_ref, prm_ref, of_ref, ob_ref, s_ref):
    @pl.when(pl.program_id(1) == 0)
    def _():
        s_ref[...] = jnp.zeros_like(s_ref)

    nh, dk = GDN_HEADS, GDN_DK
    nb = qf_ref.shape[0]
    prm = prm_ref[...]
    a_log, dt_bias = prm[0:1, :], prm[1:2, :]
    q, k, v, beta, g_col, g_last, gamma, strict = [], [], [], [], [], [], [], []
    for bi, d in [(bi, d) for bi in range(nb) for d in range(2)]:
        qkv = (qf_ref, qb_ref)[d][bi]
        gates = (gf_ref, gb_ref)[d][bi]
        c = qkv.shape[0]
        incl, strict_d = _order_masks(c, d == 1)
        g_all = -jnp.exp(a_log) * _softplus(gates + dt_bias)
        beta_all = _sigmoid(gates)
        big_g = _dot_01x(incl.astype(F32), g_all)
        big_g_t = big_g.T
        last = 0 if d == 1 else c - 1
        for h in range(nh):
            col = d * nh + h
            q.append(qkv[:, h * dk:(h + 1) * dk])
            k.append(qkv[:, (nh + h) * dk:(nh + h + 1) * dk])
            v.append(qkv[:, (2 * nh + h) * dk:(2 * nh + h + 1) * dk])
            beta.append(beta_all[:, 2 * nh + col:2 * nh + col + 1])
            g_col.append(big_g[:, col:col + 1])
            g_last.append(big_g[last:last + 1, col:col + 1])
            g_row = big_g_t[col:col + 1, :]
            gamma.append(jnp.where(incl, jnp.exp(jnp.where(incl, g_col[-1] - g_row, 0.0)), 0.0))
            strict.append(strict_d)
    n = len(q)
    c = q[0].shape[0]
    e_g = [jnp.exp(g) for g in g_col]
    kq = [_dot_nt(jnp.concatenate([k[i], q[i]], axis=0), k[i]) for i in range(n)]
    a_mat = [jnp.where(strict[i], beta[i] * kq[i][:c] * gamma[i], 0.0) for i in range(n)]
    qk = [kq[i][c:] * gamma[i] for i in range(n)]
    t_inv = _unit_lower_inverses([-a for a in a_mat])
    uw = [_dot(t_inv[i], jnp.concatenate([beta[i] * v[i], (beta[i] * e_g[i]) * k[i]], axis=1))
          for i in range(n)]
    s = [s_ref[i // nh, i % nh] for i in range(n)]
    ws = [_dot(jnp.concatenate([uw[i][:, dk:], q[i] * e_g[i]], axis=0), s[i]) for i in range(n)]
    v_new = [uw[i][:, :dk] - ws[i][:c] for i in range(n)]
    o = [ws[i][c:] + _dot(qk[i], v_new[i]) for i in range(n)]
    s_new = [s[i] * jnp.exp(g_last[i]) + _dot_tn(k[i] * jnp.exp(g_last[i] - g_col[i]), v_new[i])
             for i in range(n)]
    for i in range(n):
        s_ref[i // nh, i % nh] = s_new[i]
    for bi in range(nb):
        base = 2 * nh * bi
        of_ref[bi] = jnp.concatenate(o[base:base + nh], axis=-1)
        ob_ref[bi] = jnp.concatenate(o[base + nh:base + 2 * nh], axis=-1)


BATCH_PER_STEP = 2


def _rev_chunk_map(n_ctx_chunks, n_chunks):
    def cb(s):
        return jnp.where(s < n_ctx_chunks, n_ctx_chunks - 1 - s, n_chunks + n_ctx_chunks - 1 - s)
    return cb


def gdn_chunk(qkv, p_gdn, prm, n_ctx):
    bsz, t, ncol = qkv.shape
    nv = GDN_HEADS * GDN_DK
    n_chunks = t // CHUNK
    cb = _rev_chunk_map(n_ctx // CHUNK, n_chunks)
    gate_tile = p_gdn.shape[2] // LANES - 1
    nb = BATCH_PER_STEP if bsz % BATCH_PER_STEP == 0 else 1
    return pl.pallas_call(
        _gdn_chunk_kernel,
        grid=(bsz // nb, n_chunks),
        in_specs=[pl.BlockSpec((nb, CHUNK, ncol), lambda b, s: (b, s, 0)),
                  pl.BlockSpec((nb, CHUNK, ncol), lambda b, s: (b, cb(s), 0)),
                  pl.BlockSpec((nb, CHUNK, LANES), lambda b, s: (b, s, gate_tile)),
                  pl.BlockSpec((nb, CHUNK, LANES), lambda b, s: (b, cb(s), gate_tile)),
                  pl.BlockSpec((8, LANES), lambda b, s: (0, 0))],
        out_specs=[pl.BlockSpec((nb, CHUNK, nv), lambda b, s: (b, s, 0)),
                   pl.BlockSpec((nb, CHUNK, nv), lambda b, s: (b, cb(s), 0))],
        out_shape=[jax.ShapeDtypeStruct((bsz, t, nv), F32)] * 2,
        scratch_shapes=[pltpu.VMEM((2 * nb, GDN_HEADS, GDN_DK, GDN_DK), F32)],
        compiler_params=_cparams(("parallel", "arbitrary")),
        name="gdn_chunk",
    )(qkv, qkv, p_gdn, p_gdn, prm)


def _rwkv_shift_kernel(p_ref, mu_ref, o_ref, *, n_ctx, n_cols):
    j = pl.program_id(1)
    x = p_ref[...]
    t_len = x.shape[0]
    t = lax.broadcasted_iota(jnp.int32, x.shape, 0)
    ch = lax.broadcasted_iota(jnp.int32, x.shape, 1) + j * LANES
    is_ctx = t < n_ctx
    col = (t - n_ctx) % GRID_W
    prev1 = pltpu.roll(x, 1, 0)
    next1 = pltpu.roll(x, t_len - 1, 0)
    up = pltpu.roll(x, GRID_W, 0)
    down = pltpu.roll(x, t_len - GRID_W, 0)
    quarter = n_cols // 4
    half = n_cols // 2
    ctx_sh = jnp.where(ch < half,
                       jnp.where(t >= 1, prev1, 0.0),
                       jnp.where(t < n_ctx - 1, next1, 0.0))
    lat_sh = jnp.where(ch < quarter, jnp.where(col >= 1, prev1, 0.0),
                       jnp.where(ch < 2 * quarter, jnp.where(col < GRID_W - 1, next1, 0.0),
                                 jnp.where(ch < 3 * quarter,
                                           jnp.where(t >= n_ctx + GRID_W, up, 0.0),
                                           jnp.where(t < t_len - GRID_W, down, 0.0))))
    sh = jnp.where(is_ctx, ctx_sh, lat_sh)
    o_ref[...] = x + mu_ref[...] * (sh - x)


def rwkv_shift(p_rwkv, mu, n_ctx):
    bsz, t, ncol = p_rwkv.shape
    return pl.pallas_call(
        functools.partial(_rwkv_shift_kernel, n_ctx=n_ctx, n_cols=ncol),
        grid=(bsz, ncol // LANES),
        in_specs=[pl.BlockSpec((None, t, LANES), lambda b, j: (b, 0, j)),
                  pl.BlockSpec((1, LANES), lambda b, j: (0, j))],
        out_specs=pl.BlockSpec((None, t, LANES), lambda b, j: (b, 0, j)),
        out_shape=jax.ShapeDtypeStruct((bsz, t, ncol), F32),
        compiler_params=_cparams(("parallel", "parallel")),
        name="rwkv_shift",
    )(p_rwkv, mu.reshape(1, ncol))


def _rwkv_gates(xs, d, w0_ref, w2_ref, a0_ref, a2_ref):
    w_ = RWKV_W
    xw = xs[:, 3 * w_ + d * LORA:3 * w_ + (d + 1) * LORA]
    xa = xs[:, 3 * w_ + 2 * LORA + d * LORA:3 * w_ + 2 * LORA + (d + 1) * LORA]
    w_pre = w0_ref[d:d + 1, :] + _dot(jnp.tanh(xw), w2_ref[d])
    log_w = -math.exp(-0.5) * _sigmoid(w_pre)
    a = _sigmoid(a0_ref[d:d + 1, :] + _dot(xa, a2_ref[d]))
    return log_w, a


def _rwkv_chunk_kernel(xf_ref, xb_ref, vec_ref, w0_ref, w2_ref, a0_ref, a2_ref, hsum_ref,
                       yf_ref, yb_ref, p_ref):
    @pl.when(pl.program_id(1) == 0)
    def _():
        p_ref[...] = jnp.zeros_like(p_ref)

    w_, n, nh = RWKV_W, RWKV_HD, RWKV_HEADS
    nb = xf_ref.shape[0]
    k_k, k_a = vec_ref[0:1, :], vec_ref[1:2, :]
    lhs, rhs_t, tail, vals, w_last, p0, strict, incl = [], [], [], [], [], [], [], []
    for bi, d in [(bi, d) for bi in range(nb) for d in range(2)]:
        xs = (xf_ref, xb_ref)[d][bi]
        c = xs.shape[0]
        incl_d, strict_d = _order_masks(c, d == 1)
        r = xs[:, 0:w_]
        k = xs[:, w_:2 * w_]
        v = xs[:, 2 * w_:3 * w_]
        log_w, a = _rwkv_gates(xs, d, w0_ref, w2_ref, a0_ref, a2_ref)
        kk = k * k_k
        kk = kk * lax.rsqrt(_dot_x01(kk * kk, hsum_ref[...]) + L2_EPS)
        k_dir = k * (1.0 + (a - 1.0) * k_a)
        alpha = -(kk * a)
        big_g = _dot_01x(incl_d.astype(F32), log_w)
        last = 0 if d == 1 else c - 1
        g_last = big_g[last:last + 1, :]
        e_neg = jnp.exp(-big_g)
        e_tail = jnp.exp(g_last - big_g)
        r_t = r * jnp.exp(big_g)
        b_t = kk * jnp.exp(big_g - log_w)
        a_t = alpha * e_neg
        k_t = k_dir * e_neg
        a_h = alpha * e_tail
        k_h = k_dir * e_tail
        w_l = jnp.exp(g_last)
        for h in range(nh):
            sl = slice(h * n, (h + 1) * n)
            lhs.append(jnp.concatenate([b_t[:, sl], r_t[:, sl]], axis=0))
            rhs_t.append(jnp.concatenate([a_t[:, sl], k_t[:, sl]], axis=0))
            tail.append(jnp.concatenate([a_h[:, sl], k_h[:, sl]], axis=0))
            vals.append(v[:, sl])
            w_last.append(w_l[:, sl])
            p0.append(p_ref[2 * bi + d, h])
            strict.append(strict_d)
            incl.append(incl_d)
    m = len(lhs)
    c = vals[0].shape[0]
    quad = [_dot_nt(lhs[i], rhs_t[i]) for i in range(m)]
    a_ba = [jnp.where(strict[i], quad[i][:c, :c], 0.0) for i in range(m)]
    a_k = [jnp.concatenate([jnp.where(strict[i], quad[i][:c, c:], 0.0),
                            jnp.where(incl[i], quad[i][c:, c:], 0.0)], axis=0) for i in range(m)]
    a_ra = [jnp.where(incl[i], quad[i][c:, :c], 0.0) for i in range(m)]
    from_state = [_dot_nt(lhs[i], p0[i]) for i in range(m)]
    from_vals = [_dot(a_k[i], vals[i]) for i in range(m)]
    t_inv = _unit_lower_inverses(a_ba)
    u = [_dot(t_inv[i], from_state[i][:c] + from_vals[i][:c]) for i in range(m)]
    y = [from_state[i][c:] + _dot(a_ra[i], u[i]) + from_vals[i][c:] for i in range(m)]
    p_new = [p0[i] * w_last[i] + _dot_tn(jnp.concatenate([u[i], vals[i]], axis=0), tail[i])
             for i in range(m)]
    for i in range(m):
        p_ref[i // nh, i % nh] = p_new[i]
    for bi in range(nb):
        base = 2 * nh * bi
        yf_ref[bi] = jnp.concatenate(y[base:base + nh], axis=-1)
        yb_ref[bi] = jnp.concatenate(y[base + nh:base + 2 * nh], axis=-1)


def _head_sum_matrix(width, group):
    i = jnp.arange(width)
    return (i[:, None] // group == i[None, :] // group).astype(F32)


def rwkv_chunk(xs, vecs, w0, w2, a0, a2, n_ctx):
    bsz, t, ncol = xs.shape
    n_chunks = t // CHUNK
    cb = _rev_chunk_map(n_ctx // CHUNK, n_chunks)
    hsum = _head_sum_matrix(RWKV_W, RWKV_HD)
    full = lambda *shape: pl.BlockSpec(shape, lambda b, s: (0,) * len(shape))
    nb = BATCH_PER_STEP if bsz % BATCH_PER_STEP == 0 else 1
    return pl.pallas_call(
        _rwkv_chunk_kernel,
        grid=(bsz // nb, n_chunks),
        in_specs=[pl.BlockSpec((nb, CHUNK, ncol), lambda b, s: (b, s, 0)),
                  pl.BlockSpec((nb, CHUNK, ncol), lambda b, s: (b, cb(s), 0)),
                  full(8, RWKV_W), full(2, RWKV_W), full(2, LORA, RWKV_W),
                  full(2, RWKV_W), full(2, LORA, RWKV_W), full(RWKV_W, RWKV_W)],
        out_specs=[pl.BlockSpec((nb, CHUNK, RWKV_W), lambda b, s: (b, s, 0)),
                   pl.BlockSpec((nb, CHUNK, RWKV_W), lambda b, s: (b, cb(s), 0))],
        out_shape=[jax.ShapeDtypeStruct((bsz, t, RWKV_W), F32)] * 2,
        scratch_shapes=[pltpu.VMEM((2 * nb, RWKV_HEADS, RWKV_HD, RWKV_HD), F32)],
        compiler_params=_cparams(("parallel", "arbitrary")),
        name="rwkv_chunk",
    )(xs, xs, vecs, w0, w2, a0, a2, hsum)


def _mix_out_kernel(x_ref, gt_ref, of_ref, ob_ref, z_ref, yf_ref, yb_ref, xs_ref,
                    gnw_ref, vec_ref, a0_ref, a2_ref, g2_ref, hsum_ref, wo_ref, o_ref):
    nh, dv = GDN_HEADS, GDN_DK
    w_ = RWKV_W
    o = of_ref[...] + ob_ref[...]
    z = z_ref[...]
    parts = []
    for h in range(nh):
        oh = o[:, h * dv:(h + 1) * dv]
        oh = oh * lax.rsqrt(jnp.mean(oh * oh, axis=-1, keepdims=True) + NORM_EPS)
        parts.append(oh * gnw_ref[...] * _silu(z[:, h * dv:(h + 1) * dv]))
    xs = xs_ref[...]
    r = xs[:, 0:w_]
    k = xs[:, w_:2 * w_]
    v = xs[:, 2 * w_:3 * w_]
    xg = xs[:, 3 * w_ + 4 * LORA:]
    k_a, r_k = vec_ref[1:2, :], vec_ref[2:3, :]
    gn_w, gn_b = vec_ref[3:4, :], vec_ref[4:5, :]
    hsum = hsum_ref[...]
    y = yf_ref[...] + yb_ref[...]
    mean = _dot_x01(y, hsum) * (1.0 / RWKV_HD)
    yc = y - mean
    var = _dot_x01(yc * yc, hsum) * (1.0 / RWKV_HD)
    yn = yc * lax.rsqrt(var + RWKV_GN_EPS) * gn_w + gn_b
    rk = jnp.zeros_like(r)
    for d in range(2):
        xa = xs[:, 3 * w_ + 2 * LORA + d * LORA:3 * w_ + 2 * LORA + (d + 1) * LORA]
        a = _sigmoid(a0_ref[d:d + 1, :] + _dot(xa, a2_ref[d]))
        rk = rk + r * (k * (1.0 + (a - 1.0) * k_a)) * r_k
    bonus = _dot_x01(rk, hsum) * v
    gate = _dot(_sigmoid(xg), g2_ref[...])
    parts.append((yn + bonus) * gate)
    mixed = jnp.concatenate(parts, axis=-1).astype(BF16)
    o_ref[...] = x_ref[...] + gt_ref[...] * jnp.dot(mixed, wo_ref[...], preferred_element_type=F32)


def mix_out(x, gt1, o_f, o_b, p_gdn, y_f, y_b, xs, gnw, vecs, a0, a2, g2, w_out, n_ctx, tm):
    bsz, seq, d = x.shape
    off = n_ctx // tm
    nv = GDN_HEADS * GDN_DK
    ncol = xs.shape[2]
    hsum = _head_sum_matrix(RWKV_W, RWKV_HD)
    lat = lambda width, cblk=0: pl.BlockSpec((None, tm, width), lambda b, i: (b, i + off, cblk))
    full = lambda *shape: pl.BlockSpec(shape, lambda b, i: (0,) * len(shape))
    return pl.pallas_call(
        _mix_out_kernel,
        grid=(bsz, seq // tm),
        in_specs=[pl.BlockSpec((None, tm, d), lambda b, i: (b, i, 0)),
                  pl.BlockSpec((None, 1, d), lambda b, i: (b, 0, 0)),
                  lat(nv), lat(nv), lat(nv, 3), lat(RWKV_W), lat(RWKV_W), lat(ncol),
                  full(1, GDN_DK), full(8, RWKV_W), full(2, RWKV_W), full(2, LORA, RWKV_W),
                  full(2 * LORA, RWKV_W), full(RWKV_W, RWKV_W), full(nv + RWKV_W, d)],
        out_specs=pl.BlockSpec((None, tm, d), lambda b, i: (b, i, 0)),
        out_shape=jax.ShapeDtypeStruct((bsz, seq, d), F32),
        compiler_params=_cparams(("parallel", "parallel")),
        name="mix_out",
    )(x, gt1, o_f, o_b, p_gdn, y_f, y_b, xs, gnw, vecs, a0, a2, g2, hsum, w_out)


def _extract_top(s, riota, n_rows, vals_ref, idx_ref, k):
    for i in range(k):
        m = jnp.max(s, axis=0, keepdims=True)
        am = jnp.min(jnp.where(s == m, riota, float(n_rows)), axis=0, keepdims=True)
        vals_ref[i:i + 1, :] = m
        idx_ref[i:i + 1, :] = am
        s = jnp.where(riota == am, NEG_INF, s)


def _candidate_blocks(k):
    split = math.isqrt(k)
    blocks = [(False, i, 0, k // (i + 1)) for i in range(split)]
    j = 0
    while k // (j + 1) > split:
        blocks.append((True, j, split, k // (j + 1)))
        j += 1
    return blocks


def _candidate_rows(k):
    return sum(-(-(hi - lo) // 8) * 8 for _, _, lo, hi in _candidate_blocks(k))


def _top_list_rows(k):
    need = max(lo + -(-(hi - lo) // 8) * 8 for _, _, lo, hi in _candidate_blocks(k))
    return -(-need // 8) * 8


def _peer_route_kernel(x_ref, g_ref, sh_ref, sc_ref, wq_ref, keys_ref,
                       h2_ref, idx_ref, gate_ref,
                       q_scr, tv_scr, ti_scr, cs_scr, ci_scr, cf_scr, bv_scr, bi_scr):
    h = pl.program_id(1)
    dq = 2 * LANES
    kk = PEER_TOPK

    @pl.when(h == 0)
    def _():
        x = x_ref[...]
        y = x * lax.rsqrt(jnp.mean(x * x, axis=-1, keepdims=True) + NORM_EPS) * g_ref[...]
        h2 = y * (1.0 + sc_ref[...]) + sh_ref[...]
        h2_ref[...] = h2
        hb = h2.astype(BF16)
        for hh in range(PEER_HEADS):
            q_scr[hh] = jnp.dot(hb, wq_ref[:, hh * dq:(hh + 1) * dq], preferred_element_type=F32)

    q = q_scr[h]
    tm = q.shape[0]
    riota = lax.broadcasted_iota(jnp.int32, (PEER_NKEYS, tm), 0).astype(F32)
    for p in range(2):
        s_t = _dot_nt(keys_ref[p], q[:, p * LANES:(p + 1) * LANES])
        _extract_top(s_t, riota, PEER_NKEYS, tv_scr.at[p], ti_scr.at[p], kk)
    tv_scr[:, kk:, :] = jnp.zeros_like(tv_scr[:, kk:, :])
    ti_scr[:, kk:, :] = jnp.zeros_like(ti_scr[:, kk:, :])
    row = 0
    for fixed_j, fixed, lo, hi in _candidate_blocks(kk):
        n_pad = -(-(hi - lo) // 8) * 8
        r = lax.broadcasted_iota(jnp.int32, (n_pad, tm), 0).astype(F32)
        run, one = (0, 1) if fixed_j else (1, 0)
        vals = tv_scr[run, lo:lo + n_pad, :] + tv_scr[one, fixed:fixed + 1, :]
        run_i, one_i = ti_scr[run, lo:lo + n_pad, :], ti_scr[one, fixed:fixed + 1, :]
        if fixed_j:
            experts = run_i * float(PEER_NKEYS) + one_i
            flat = (r + float(lo)) * float(kk) + float(fixed)
        else:
            experts = one_i * float(PEER_NKEYS) + run_i
            flat = float(fixed * kk + lo) + r
        cs_scr[row:row + n_pad, :] = jnp.where(r < float(hi - lo), vals, NEG_INF)
        ci_scr[row:row + n_pad, :] = experts
        cf_scr[row:row + n_pad, :] = flat
        row += n_pad
    cand = cs_scr[...]
    cand_i = ci_scr[...]
    ciota = cf_scr[...]
    for i in range(kk):
        m = jnp.max(cand, axis=0, keepdims=True)
        pos = jnp.min(jnp.where(cand == m, ciota, float(kk * kk)), axis=0, keepdims=True)
        hit = ciota == pos
        bv_scr[i:i + 1, :] = m
        bi_scr[i:i + 1, :] = jnp.sum(jnp.where(hit, cand_i, 0.0), axis=0, keepdims=True)
        cand = jnp.where(hit, NEG_INF, cand)
    best = bv_scr[...]
    e = jnp.exp(best - best[0:1, :])
    gate_ref[...] = e / jnp.sum(e, axis=0, keepdims=True)
    idx_ref[...] = bi_scr[...].astype(jnp.int32)


def peer_route(x1, norm_g, sh2, sc2, wq, sub_keys, tm):
    bsz, seq, d = x1.shape
    ntok = bsz * seq
    tiles_per_b = seq // tm
    kk = PEER_TOPK
    nq = wq.shape[1]
    return pl.pallas_call(
        _peer_route_kernel,
        grid=(ntok // tm, PEER_HEADS),
        in_specs=[pl.BlockSpec((tm, d), lambda i, h: (i, 0)),
                  pl.BlockSpec((1, d), lambda i, h: (0, 0)),
                  pl.BlockSpec((None, 1, d), lambda i, h: (i // tiles_per_b, 0, 0)),
                  pl.BlockSpec((None, 1, d), lambda i, h: (i // tiles_per_b, 0, 0)),
                  pl.BlockSpec((d, nq), lambda i, h: (0, 0)),
                  pl.BlockSpec((None, 2, PEER_NKEYS, LANES), lambda i, h: (h, 0, 0, 0))],
        out_specs=[pl.BlockSpec((tm, d), lambda i, h: (i, 0)),
                   pl.BlockSpec((kk, tm), lambda i, h: (h, i)),
                   pl.BlockSpec((kk, tm), lambda i, h: (h, i))],
        out_shape=[jax.ShapeDtypeStruct((ntok, d), F32),
                   jax.ShapeDtypeStruct((PEER_HEADS * kk, ntok), jnp.int32),
                   jax.ShapeDtypeStruct((PEER_HEADS * kk, ntok), F32)],
        scratch_shapes=[pltpu.VMEM((PEER_HEADS, tm, 2 * LANES), F32),
                        pltpu.VMEM((2, _top_list_rows(kk), tm), F32),
                        pltpu.VMEM((2, _top_list_rows(kk), tm), F32),
                        pltpu.VMEM((_candidate_rows(kk), tm), F32),
                        pltpu.VMEM((_candidate_rows(kk), tm), F32),
                        pltpu.VMEM((_candidate_rows(kk), tm), F32),
                        pltpu.VMEM((kk, tm), F32), pltpu.VMEM((kk, tm), F32)],
        compiler_params=_cparams(("parallel", "arbitrary")),
        name="peer_route",
    )(x1.reshape(ntok, d), norm_g.reshape(1, d), sh2, sc2, wq, sub_keys)


GROUP = 8
N_GSLOTS = 4
AHEAD = 2
N_SLOTS = GROUP * N_GSLOTS
ROW_TILE = 8
DMA_QUEUES = 2


def _peer_expert_kernel(idx_ref, h2_ref, gate_ref, x1_ref, gt_ref, fg_ref, tbl_ref,
                        o_ref, *scratch, tt, d):
    bufs, tiles, sem = scratch[:N_SLOTS], scratch[N_SLOTS:N_SLOTS + GROUP], scratch[N_SLOTS + GROUP]
    ne = PEER_HEADS * PEER_TOPK
    step = pl.program_id(0)
    last_step = pl.num_programs(0) - 1

    def issue_token(base, gslot, j):
        slot = gslot * GROUP + j
        for k in range(ne):
            e = idx_ref[base + k]
            pltpu.make_async_copy(tbl_ref.at[e], bufs[slot].at[k],
                                  sem.at[gslot]).start(priority=k % DMA_QUEUES)

    def wait_group(gslot):
        for j in range(GROUP):
            pltpu.make_async_copy(tbl_ref.at[pl.ds(0, ne)], bufs[gslot * GROUP + j],
                                  sem.at[gslot]).wait()

    def evaluate(t, slot):
        tile = tiles[slot % GROUP]
        tile[...] = bufs[slot][...].reshape(ne, 2 * d)
        hrow = h2_ref[pl.ds(t, 1), :]
        grow = gate_ref[pl.ds(t, 1), :]
        lane = lax.broadcasted_iota(jnp.int32, (ROW_TILE, ne), 1)
        sub = lax.broadcasted_iota(jnp.int32, (ROW_TILE, ne), 0)
        acc = jnp.zeros((ROW_TILE, d), F32)
        for g in range(ne // ROW_TILE):
            rows = tile[g * ROW_TILE:(g + 1) * ROW_TILE, :]
            pre = jnp.sum(rows[:, :d] * hrow, axis=1, keepdims=True)
            act = 0.5 * pre * (1.0 + lax.erf(pre * (2.0 ** -0.5)))
            gcol = jnp.sum(jnp.where(lane == sub + g * ROW_TILE, grow, 0.0), axis=1, keepdims=True)
            acc = acc + (act * gcol) * rows[:, d:]
        y = jnp.sum(acc, axis=0, keepdims=True)
        xo = x1_ref[pl.ds(t, 1), :] + gt_ref[...] * y
        xo = xo * lax.rsqrt(jnp.mean(xo * xo, axis=-1, keepdims=True) + NORM_EPS) * fg_ref[...]
        o_ref[pl.ds(t, 1), :] = xo

    @pl.when(step == 0)
    def _():
        for g in range(AHEAD):
            for j in range(GROUP):
                issue_token((g * GROUP + j) * ne, g, j)

    def ring_turn(it, carry):
        for gs in range(N_GSLOTS):
            t0 = (it * N_GSLOTS + gs) * GROUP
            wait_group(gs)
            for j in range(GROUP):
                issue_token((t0 + AHEAD * GROUP + j) * ne, (gs + AHEAD) % N_GSLOTS, j)
                evaluate(t0 + j, gs * GROUP + j)
        return carry

    lax.fori_loop(0, tt // N_SLOTS, ring_turn, 0)

    @pl.when(step == last_step)
    def _():
        for g in range(AHEAD):
            wait_group(g)


def peer_expert(idx_tok, h2, gate_tok, x1, gt2, final_g, table, seq, tt):
    ntok, d = h2.shape
    ne = PEER_HEADS * PEER_TOPK
    tiles_per_b = seq // tt
    n_steps = ntok // tt
    assert tt % N_SLOTS == 0
    idx_steps = idx_tok.reshape(n_steps, tt * ne)
    tail = jnp.roll(idx_steps[:, :AHEAD * GROUP * ne], -1, axis=0)
    win = (tt + AHEAD * GROUP) * ne
    idx_win = jnp.concatenate([idx_steps, tail], axis=1).reshape(n_steps * win)
    return pl.pallas_call(
        functools.partial(_peer_expert_kernel, tt=tt, d=d),
        grid=(n_steps,),
        in_specs=[pl.BlockSpec((win,), lambda i: (i,), memory_space=pltpu.SMEM),
                  pl.BlockSpec((tt, d), lambda i: (i, 0)),
                  pl.BlockSpec((tt, ne), lambda i: (i, 0)),
                  pl.BlockSpec((tt, d), lambda i: (i, 0)),
                  pl.BlockSpec((None, 1, d), lambda i: (i // tiles_per_b, 0, 0)),
                  pl.BlockSpec((1, d), lambda i: (0, 0)),
                  pl.BlockSpec(memory_space=pl.ANY)],
        out_specs=pl.BlockSpec((tt, d), lambda i: (i, 0)),
        out_shape=jax.ShapeDtypeStruct((ntok, d), F32),
        scratch_shapes=([pltpu.VMEM((ne, 1, 2 * d), F32) for _ in range(N_SLOTS)]
                        + [pltpu.VMEM((ne, 2 * d), F32) for _ in range(GROUP)]
                        + [pltpu.SemaphoreType.DMA((N_GSLOTS,))]),
        compiler_params=_cparams(("arbitrary",)),
        name="peer_expert",
    )(idx_win, h2, gate_tok, x1, gt2, final_g.reshape(1, d), table)


ROW_TILE_TOKENS = 256
ROUTE_TILE_TOKENS = 512
GATHER_STEP_TOKENS = 128


def kernel(x, c, ctx, c_ctx, ada_w, ada_b, norm1_g, w_in, gdn_conv_w, gdn_a_log, gdn_dt_bias,
           gdn_norm_w, rwkv_mu, rwkv_w0, rwkv_w2, rwkv_a0, rwkv_a2, rwkv_g2, rwkv_k_k, rwkv_k_a,
           rwkv_r_k, rwkv_gn_w, rwkv_gn_b, w_out, norm2_g, peer_w_query, peer_sub_keys,
           peer_down, peer_up, final_norm_g):
    bsz, seq, d = x.shape
    n_ctx = ctx.shape[1]
    assert ada_w.shape[0] == 1, "single-layer block"
    assert n_ctx % ROW_TILE_TOKENS == 0 and seq % ROUTE_TILE_TOKENS == 0 and seq % GRID_W == 0
    nh = GDN_HEADS
    n_qkvz = 4 * nh * GDN_DK
    gdn_cols = n_qkvz + 4 * nh

    cond = jnp.concatenate([c, c_ctx[None, :], jnp.zeros((16 - bsz - 1, d), F32)], axis=0)
    mods = ada_mod(cond, ada_w[0], ada_b[0])
    m_lat = mods[:bsz].reshape(bsz, N_MOD, 1, d)
    m_ctx = jnp.broadcast_to(mods[bsz].reshape(1, N_MOD, 1, d), (bsz, N_MOD, 1, d))
    shift1 = jnp.stack([m_ctx[:, 0], m_lat[:, 0]], axis=1)
    scale1 = jnp.stack([m_ctx[:, 1], m_lat[:, 1]], axis=1)
    gt1, sh2, sc2, gt2 = m_lat[:, 2], m_lat[:, 3], m_lat[:, 4], m_lat[:, 5]

    wl = w_in[0]
    w_gdn = jnp.concatenate([wl[:, :gdn_cols], jnp.zeros((d, LANES - 4 * nh), F32)], axis=1)
    w_rwkv = wl[:, gdn_cols:]
    xin = jnp.concatenate([ctx, x], axis=1)
    p_gdn, p_rwkv = in_proj(xin, norm1_g[0], shift1, scale1, w_gdn.astype(BF16),
                            w_rwkv.astype(BF16), n_ctx, ROW_TILE_TOKENS)

    qkv = gdn_prep(p_gdn, gdn_conv_w[0], n_ctx)
    prm = jnp.zeros((8, LANES), F32)
    prm = prm.at[0, :2 * nh].set(gdn_a_log[0].reshape(-1))
    prm = prm.at[1, :2 * nh].set(gdn_dt_bias[0].reshape(-1))
    o_f, o_b = gdn_chunk(qkv, p_gdn, prm, n_ctx)

    xs = rwkv_shift(p_rwkv, rwkv_mu[0], n_ctx)
    vecs = jnp.zeros((8, RWKV_W), F32)
    for i, vec in enumerate((rwkv_k_k, rwkv_k_a, rwkv_r_k, rwkv_gn_w, rwkv_gn_b)):
        vecs = vecs.at[i].set(vec[0])
    y_f, y_b = rwkv_chunk(xs, vecs, rwkv_w0[0], rwkv_w2[0], rwkv_a0[0], rwkv_a2[0], n_ctx)

    x1 = mix_out(x, gt1, o_f, o_b, p_gdn, y_f, y_b, xs, gdn_norm_w[0].reshape(1, GDN_DK), vecs,
                 rwkv_a0[0], rwkv_a2[0], rwkv_g2[0], w_out[0].astype(BF16), n_ctx, ROW_TILE_TOKENS)

    h2, idx_t, gate_t = peer_route(x1, norm2_g[0], sh2, sc2, peer_w_query[0].astype(BF16),
                                   peer_sub_keys[0], ROUTE_TILE_TOKENS)
    table = jnp.concatenate([peer_down[0], peer_up[0]], axis=1)[:, None, :]
    out = peer_expert(idx_t.T, h2, gate_t.T, x1.reshape(bsz * seq, d), gt2,
                      final_norm_g, table, seq, GATHER_STEP_TOKENS)
    return out.reshape(bsz, seq, d)
```

```python
import functools
import math

import jax
import jax.numpy as jnp
from jax import lax
from jax.experimental import pallas as pl
from jax.experimental.pallas import tpu as pltpu

F32 = jnp.float32
BF16 = jnp.bfloat16

GRID_W = 64
GDN_HEADS = 4
GDN_DK = 128
GDN_CONV = 5
CHUNK = 64
RWKV_HEADS = 8
RWKV_HD = 64
RWKV_W = RWKV_HEADS * RWKV_HD
LORA = 64
PEER_HEADS = 8
PEER_NKEYS = 128
PEER_TOPK = 16
N_MOD = 6

NORM_EPS = 1e-6
L2_EPS = 1e-6
RWKV_GN_EPS = 64e-5

LANES = 128
VMEM_LIMIT = 56 * 1024 * 1024
NEG_INF = float("-inf")


def _cparams(sem):
    return pltpu.CompilerParams(dimension_semantics=sem, vmem_limit_bytes=VMEM_LIMIT)


def _dot(a, b):
    return jnp.dot(a.astype(BF16), b.astype(BF16), preferred_element_type=F32)


def _dot_nt(a, b):
    return lax.dot_general(a.astype(BF16), b.astype(BF16), (((1,), (1,)), ((), ())),
                           preferred_element_type=F32)


def _dot_tn(a, b):
    return lax.dot_general(a.astype(BF16), b.astype(BF16), (((0,), (0,)), ((), ())),
                           preferred_element_type=F32)


def _dot_hi(a, b):
    return jnp.dot(a, b, preferred_element_type=F32, precision=lax.Precision.HIGHEST)


def _split3(x):
    hi = x.astype(BF16)
    rest = x - hi.astype(F32)
    mid = rest.astype(BF16)
    lo = (rest - mid.astype(F32)).astype(BF16)
    return hi, mid, lo


def _dot_x01(a, m01):
    m = m01.astype(BF16)
    return sum(jnp.dot(p, m, preferred_element_type=F32) for p in _split3(a))


def _dot_01x(m01, b):
    m = m01.astype(BF16)
    return sum(jnp.dot(m, p, preferred_element_type=F32) for p in _split3(b))


def _sigmoid(x):
    return 1.0 / (1.0 + jnp.exp(-x))


def _silu(x):
    return x * _sigmoid(x)


def _softplus(x):
    return jnp.maximum(x, 0.0) + jnp.log(1.0 + jnp.exp(-jnp.abs(x)))


def _unit_lower_inverses(xs):
    c = xs[0].shape[0]
    eye = (lax.broadcasted_iota(jnp.int32, (c, c), 0) ==
           lax.broadcasted_iota(jnp.int32, (c, c), 1)).astype(F32)
    prods = [eye + x for x in xs]
    pows = [_dot(x, x) for x in xs]
    for _ in range(int(math.log2(c)) - 2):
        both = [_dot(jnp.concatenate([p, x], axis=0), x) for p, x in zip(prods, pows)]
        prods = [p + b[:c] for p, b in zip(prods, both)]
        pows = [b[c:] for b in both]
    return [p + _dot(p, x) for p, x in zip(prods, pows)]


def _order_masks(c, rev):
    r = lax.broadcasted_iota(jnp.int32, (c, c), 0)
    s = lax.broadcasted_iota(jnp.int32, (c, c), 1)
    if rev:
        return s >= r, s > r
    return s <= r, s < r


def _ada_kernel(c_ref, w_ref, b_ref, o_ref):
    o_ref[...] = _dot_hi(_silu(c_ref[...]), w_ref[...]) + b_ref[...]


def ada_mod(cond, w, b):
    n, d = cond.shape
    cols = w.shape[1]
    tn = 1024
    return pl.pallas_call(
        _ada_kernel,
        grid=(cols // tn,),
        in_specs=[pl.BlockSpec((n, d), lambda j: (0, 0)),
                  pl.BlockSpec((d, tn), lambda j: (0, j)),
                  pl.BlockSpec((1, tn), lambda j: (0, j))],
        out_specs=pl.BlockSpec((n, tn), lambda j: (0, j)),
        out_shape=jax.ShapeDtypeStruct((n, cols), F32),
        compiler_params=_cparams(("arbitrary",)),
        name="ada_mod",
    )(cond, w, b.reshape(1, cols))


def _in_proj_kernel(x_ref, g_ref, sh_ref, sc_ref, wg_ref, wr_ref, pg_ref, pr_ref):
    x = x_ref[...]
    y = x * lax.rsqrt(jnp.mean(x * x, axis=-1, keepdims=True) + NORM_EPS) * g_ref[...]
    h = (y * (1.0 + sc_ref[...]) + sh_ref[...]).astype(BF16)
    pg_ref[...] = jnp.dot(h, wg_ref[...], preferred_element_type=F32)
    pr_ref[...] = jnp.dot(h, wr_ref[...], preferred_element_type=F32)


def in_proj(xin, norm_g, shift, scale, w_gdn, w_rwkv, n_ctx, tm):
    bsz, t, d = xin.shape
    cg, cr = w_gdn.shape[1], w_rwkv.shape[1]
    nctx_tiles = n_ctx // tm

    def mod_map(b, i):
        return (b, jnp.where(i < nctx_tiles, 0, 1), 0, 0)

    return pl.pallas_call(
        _in_proj_kernel,
        grid=(bsz, t // tm),
        in_specs=[pl.BlockSpec((None, tm, d), lambda b, i: (b, i, 0)),
                  pl.BlockSpec((1, d), lambda b, i: (0, 0)),
                  pl.BlockSpec((None, None, 1, d), mod_map),
                  pl.BlockSpec((None, None, 1, d), mod_map),
                  pl.BlockSpec((d, cg), lambda b, i: (0, 0)),
                  pl.BlockSpec((d, cr), lambda b, i: (0, 0))],
        out_specs=[pl.BlockSpec((None, tm, cg), lambda b, i: (b, i, 0)),
                   pl.BlockSpec((None, tm, cr), lambda b, i: (b, i, 0))],
        out_shape=[jax.ShapeDtypeStruct((bsz, t, cg), F32),
                   jax.ShapeDtypeStruct((bsz, t, cr), F32)],
        compiler_params=_cparams(("parallel", "parallel")),
        name="in_proj",
    )(xin, norm_g.reshape(1, d), shift, scale, w_gdn, w_rwkv)


def _gdn_prep_kernel(p_ref, w_ref, o_ref, *, n_ctx):
    j = pl.program_id(1)
    x = p_ref[...]
    t_len = x.shape[0]
    w = w_ref[...]
    t = lax.broadcasted_iota(jnp.int32, x.shape, 0)
    is_ctx = t < n_ctx
    lo = jnp.where(is_ctx, 0, n_ctx)
    hi = jnp.where(is_ctx, n_ctx, t_len)
    pad = (GDN_CONV - 1) // 2
    acc = x * w[pad:pad + 1, :]
    for s in range(-pad, pad + 1):
        if s == 0:
            continue
        xs = pltpu.roll(x, (-s) % t_len, 0)
        valid = (t + s >= lo) & (t + s < hi)
        acc = acc + jnp.where(valid, xs, 0.0) * w[s + pad:s + pad + 1, :]
    y = _silu(acc)
    inv = lax.rsqrt(jnp.sum(y * y, axis=-1, keepdims=True) + L2_EPS)
    fac = jnp.where(j < GDN_HEADS, inv * (GDN_DK ** -0.5), jnp.where(j < 2 * GDN_HEADS, inv, 1.0))
    o_ref[...] = y * fac


def gdn_prep(p_gdn, conv_w, n_ctx):
    bsz, t, _ = p_gdn.shape
    ncol = conv_w.shape[1]
    return pl.pallas_call(
        functools.partial(_gdn_prep_kernel, n_ctx=n_ctx),
        grid=(bsz, ncol // LANES),
        in_specs=[pl.BlockSpec((None, t, LANES), lambda b, j: (b, 0, j)),
                  pl.BlockSpec((GDN_CONV, LANES), lambda b, j: (0, j))],
        out_specs=pl.BlockSpec((None, t, LANES), lambda b, j: (b, 0, j)),
        out_shape=jax.ShapeDtypeStruct((bsz, t, ncol), F32),
        compiler_params=_cparams(("parallel", "parallel")),
        name="gdn_prep",
    )(p_gdn, conv_w)


def _gdn_chunk_kernel(qf_ref, qb_ref, gf_ref, gb_ref, prm_ref, of_ref, ob_ref, s_ref):
    @pl.when(pl.program_id(1) == 0)
    def _():
        s_ref[...] = jnp.zeros_like(s_ref)

    nh, dk = GDN_HEADS, GDN_DK
    nb = qf_ref.shape[0]
    prm = prm_ref[...]
    a_log, dt_bias = prm[0:1, :], prm[1:2, :]
    q, k, v, beta, g_col, g_last, gamma, strict = [], [], [], [], [], [], [], []
    for bi, d in [(bi, d) for bi in range(nb) for d in range(2)]:
        qkv = (qf_ref, qb_ref)[d][bi]
        gates = (gf_ref, gb_ref)[d][bi]
        c = qkv.shape[0]
        incl, strict_d = _order_masks(c, d == 1)
        g_all = -jnp.exp(a_log) * _softplus(gates + dt_bias)
        beta_all = _sigmoid(gates)
        big_g = _dot_01x(incl.astype(F32), g_all)
        big_g_t = big_g.T
        last = 0 if d == 1 else c - 1
        for h in range(nh):
            col = d * nh + h
            q.append(qkv[:, h * dk:(h + 1) * dk])
            k.append(qkv[:, (nh + h) * dk:(nh + h + 1) * dk])
            v.append(qkv[:, (2 * nh + h) * dk:(2 * nh + h + 1) * dk])
            beta.append(beta_all[:, 2 * nh + col:2 * nh + col + 1])
            g_col.append(big_g[:, col:col + 1])
            g_last.append(big_g[last:last + 1, col:col + 1])
            g_row = big_g_t[col:col + 1, :]
            gamma.append(jnp.where(incl, jnp.exp(jnp.where(incl, g_col[-1] - g_row, 0.0)), 0.0))
            strict.append(strict_d)
    n = len(q)
    c = q[0].shape[0]
    e_g = [jnp.exp(g) for g in g_col]
    kq = [_dot_nt(jnp.concatenate([k[i], q[i]], axis=0), k[i]) for i in range(n)]
    a_mat = [jnp.where(strict[i], beta[i] * kq[i][:c] * gamma[i], 0.0) for i in range(n)]
    qk = [kq[i][c:] * gamma[i] for i in range(n)]
    t_inv = _unit_lower_inverses([-a for a in a_mat])
    uw = [_dot(t_inv[i], jnp.concatenate([beta[i] * v[i], (beta[i] * e_g[i]) * k[i]], axis=1))
          for i in range(n)]
    s = [s_ref[i // nh, i % nh] for i in range(n)]
    ws = [_dot(jnp.concatenate([uw[i][:, dk:], q[i] * e_g[i]], axis=0), s[i]) for i in range(n)]
    v_new = [uw[i][:, :dk] - ws[i][:c] for i in range(n)]
    o = [ws[i][c:] + _dot(qk[i], v_new[i]) for i in range(n)]
    s_new = [s[i] * jnp.exp(g_last[i]) + _dot_tn(k[i] * jnp.exp(g_last[i] - g_col[i]), v_new[i])
             for i in range(n)]
    for i in range(n):
        s_ref[i // nh, i % nh] = s_new[i]
    for bi in range(nb):
        base = 2 * nh * bi
        of_ref[bi] = jnp.concatenate(o[base:base + nh], axis=-1)
        ob_ref[bi] = jnp.concatenate(o[base + nh:base + 2 * nh], axis=-1)


GDN_BATCH_PER_STEP = 4
RWKV_BATCH_PER_STEP = 2


def _rev_chunk_map(n_ctx_chunks, n_chunks):
    def cb(s):
        return jnp.where(s < n_ctx_chunks, n_ctx_chunks - 1 - s, n_chunks + n_ctx_chunks - 1 - s)
    return cb


def gdn_chunk(qkv, p_gdn, prm, n_ctx):
    bsz, t, ncol = qkv.shape
    nv = GDN_HEADS * GDN_DK
    n_chunks = t // CHUNK
    cb = _rev_chunk_map(n_ctx // CHUNK, n_chunks)
    gate_tile = p_gdn.shape[2] // LANES - 1
    nb = math.gcd(bsz, GDN_BATCH_PER_STEP)
    return pl.pallas_call(
        _gdn_chunk_kernel,
        grid=(bsz // nb, n_chunks),
        in_specs=[pl.BlockSpec((nb, CHUNK, ncol), lambda b, s: (b, s, 0)),
                  pl.BlockSpec((nb, CHUNK, ncol), lambda b, s: (b, cb(s), 0)),
                  pl.BlockSpec((nb, CHUNK, LANES), lambda b, s: (b, s, gate_tile)),
                  pl.BlockSpec((nb, CHUNK, LANES), lambda b, s: (b, cb(s), gate_tile)),
                  pl.BlockSpec((8, LANES), lambda b, s: (0, 0))],
        out_specs=[pl.BlockSpec((nb, CHUNK, nv), lambda b, s: (b, s, 0)),
                   pl.BlockSpec((nb, CHUNK, nv), lambda b, s: (b, cb(s), 0))],
        out_shape=[jax.ShapeDtypeStruct((bsz, t, nv), F32)] * 2,
        scratch_shapes=[pltpu.VMEM((2 * nb, GDN_HEADS, GDN_DK, GDN_DK), F32)],
        compiler_params=_cparams(("parallel", "arbitrary")),
        name="gdn_chunk",
    )(qkv, qkv, p_gdn, p_gdn, prm)


def _rwkv_shift_kernel(p_ref, mu_ref, o_ref, *, n_ctx, n_cols):
    j = pl.program_id(1)
    x = p_ref[...]
    t_len = x.shape[0]
    t = lax.broadcasted_iota(jnp.int32, x.shape, 0)
    ch = lax.broadcasted_iota(jnp.int32, x.shape, 1) + j * LANES
    is_ctx = t < n_ctx
    col = (t - n_ctx) % GRID_W
    prev1 = pltpu.roll(x, 1, 0)
    next1 = pltpu.roll(x, t_len - 1, 0)
    up = pltpu.roll(x, GRID_W, 0)
    down = pltpu.roll(x, t_len - GRID_W, 0)
    quarter = n_cols // 4
    half = n_cols // 2
    ctx_sh = jnp.where(ch < half,
                       jnp.where(t >= 1, prev1, 0.0),
                       jnp.where(t < n_ctx - 1, next1, 0.0))
    lat_sh = jnp.where(ch < quarter, jnp.where(col >= 1, prev1, 0.0),
                       jnp.where(ch < 2 * quarter, jnp.where(col < GRID_W - 1, next1, 0.0),
                                 jnp.where(ch < 3 * quarter,
                                           jnp.where(t >= n_ctx + GRID_W, up, 0.0),
                                           jnp.where(t < t_len - GRID_W, down, 0.0))))
    sh = jnp.where(is_ctx, ctx_sh, lat_sh)
    o_ref[...] = x + mu_ref[...] * (sh - x)


def rwkv_shift(p_rwkv, mu, n_ctx):
    bsz, t, ncol = p_rwkv.shape
    return pl.pallas_call(
        functools.partial(_rwkv_shift_kernel, n_ctx=n_ctx, n_cols=ncol),
        grid=(bsz, ncol // LANES),
        in_specs=[pl.BlockSpec((None, t, LANES), lambda b, j: (b, 0, j)),
                  pl.BlockSpec((1, LANES), lambda b, j: (0, j))],
        out_specs=pl.BlockSpec((None, t, LANES), lambda b, j: (b, 0, j)),
        out_shape=jax.ShapeDtypeStruct((bsz, t, ncol), F32),
        compiler_params=_cparams(("parallel", "parallel")),
        name="rwkv_shift",
    )(p_rwkv, mu.reshape(1, ncol))


def _rwkv_gates(xs, d, w0_ref, w2_ref, a0_ref, a2_ref):
    w_ = RWKV_W
    xw = xs[:, 3 * w_ + d * LORA:3 * w_ + (d + 1) * LORA]
    xa = xs[:, 3 * w_ + 2 * LORA + d * LORA:3 * w_ + 2 * LORA + (d + 1) * LORA]
    w_pre = w0_ref[d:d + 1, :] + _dot(jnp.tanh(xw), w2_ref[d])
    log_w = -math.exp(-0.5) * _sigmoid(w_pre)
    a = _sigmoid(a0_ref[d:d + 1, :] + _dot(xa, a2_ref[d]))
    return log_w, a


def _rwkv_chunk_kernel(xf_ref, xb_ref, vec_ref, w0_ref, w2_ref, a0_ref, a2_ref, hsum_ref,
                       yf_ref, yb_ref, p_ref):
    @pl.when(pl.program_id(1) == 0)
    def _():
        p_ref[...] = jnp.zeros_like(p_ref)

    w_, n, nh = RWKV_W, RWKV_HD, RWKV_HEADS
    nb = xf_ref.shape[0]
    k_k, k_a = vec_ref[0:1, :], vec_ref[1:2, :]
    lhs, rhs_t, tail, vals, w_last, p0, strict, incl = [], [], [], [], [], [], [], []
    for bi, d in [(bi, d) for bi in range(nb) for d in range(2)]:
        xs = (xf_ref, xb_ref)[d][bi]
        c = xs.shape[0]
        incl_d, strict_d = _order_masks(c, d == 1)
        r = xs[:, 0:w_]
        k = xs[:, w_:2 * w_]
        v = xs[:, 2 * w_:3 * w_]
        log_w, a = _rwkv_gates(xs, d, w0_ref, w2_ref, a0_ref, a2_ref)
        kk = k * k_k
        kk = kk * lax.rsqrt(_dot_x01(kk * kk, hsum_ref[...]) + L2_EPS)
        k_dir = k * (1.0 + (a - 1.0) * k_a)
        alpha = -(kk * a)
        big_g = _dot_01x(incl_d.astype(F32), log_w)
        last = 0 if d == 1 else c - 1
        g_last = big_g[last:last + 1, :]
        e_neg = jnp.exp(-big_g)
        e_tail = jnp.exp(g_last - big_g)
        r_t = r * jnp.exp(big_g)
        b_t = kk * jnp.exp(big_g - log_w)
        a_t = alpha * e_neg
        k_t = k_dir * e_neg
        a_h = alpha * e_tail
        k_h = k_dir * e_tail
        w_l = jnp.exp(g_last)
        for h in range(nh):
            sl = slice(h * n, (h + 1) * n)
            lhs.append(jnp.concatenate([b_t[:, sl], r_t[:, sl]], axis=0))
            rhs_t.append(jnp.concatenate([a_t[:, sl], k_t[:, sl]], axis=0))
            tail.append(jnp.concatenate([a_h[:, sl], k_h[:, sl]], axis=0))
            vals.append(v[:, sl])
            w_last.append(w_l[:, sl])
            p0.append(p_ref[2 * bi + d, h])
            strict.append(strict_d)
            incl.append(incl_d)
    m = len(lhs)
    c = vals[0].shape[0]
    quad = [_dot_nt(lhs[i], rhs_t[i]) for i in range(m)]
    a_ba = [jnp.where(strict[i], quad[i][:c, :c], 0.0) for i in range(m)]
    a_k = [jnp.concatenate([jnp.where(strict[i], quad[i][:c, c:], 0.0),
                            jnp.where(incl[i], quad[i][c:, c:], 0.0)], axis=0) for i in range(m)]
    a_ra = [jnp.where(incl[i], quad[i][c:, :c], 0.0) for i in range(m)]
    from_state = [_dot_nt(lhs[i], p0[i]) for i in range(m)]
    from_vals = [_dot(a_k[i], vals[i]) for i in range(m)]
    t_inv = _unit_lower_inverses(a_ba)
    u = [_dot(t_inv[i], from_state[i][:c] + from_vals[i][:c]) for i in range(m)]
    y = [from_state[i][c:] + _dot(a_ra[i], u[i]) + from_vals[i][c:] for i in range(m)]
    p_new = [p0[i] * w_last[i] + _dot_tn(jnp.concatenate([u[i], vals[i]], axis=0), tail[i])
             for i in range(m)]
    for i in range(m):
        p_ref[i // nh, i % nh] = p_new[i]
    for bi in range(nb):
        base = 2 * nh * bi
        yf_ref[bi] = jnp.concatenate(y[base:base + nh], axis=-1)
        yb_ref[bi] = jnp.concatenate(y[base + nh:base + 2 * nh], axis=-1)


def _head_sum_matrix(width, group):
    i = jnp.arange(width)
    return (i[:, None] // group == i[None, :] // group).astype(F32)


def rwkv_chunk(xs, vecs, w0, w2, a0, a2, n_ctx):
    bsz, t, ncol = xs.shape
    n_chunks = t // CHUNK
    cb = _rev_chunk_map(n_ctx // CHUNK, n_chunks)
    hsum = _head_sum_matrix(RWKV_W, RWKV_HD)
    full = lambda *shape: pl.BlockSpec(shape, lambda b, s: (0,) * len(shape))
    nb = math.gcd(bsz, RWKV_BATCH_PER_STEP)
    return pl.pallas_call(
        _rwkv_chunk_kernel,
        grid=(bsz // nb, n_chunks),
        in_specs=[pl.BlockSpec((nb, CHUNK, ncol), lambda b, s: (b, s, 0)),
                  pl.BlockSpec((nb, CHUNK, ncol), lambda b, s: (b, cb(s), 0)),
                  full(8, RWKV_W), full(2, RWKV_W), full(2, LORA, RWKV_W),
                  full(2, RWKV_W), full(2, LORA, RWKV_W), full(RWKV_W, RWKV_W)],
        out_specs=[pl.BlockSpec((nb, CHUNK, RWKV_W), lambda b, s: (b, s, 0)),
                   pl.BlockSpec((nb, CHUNK, RWKV_W), lambda b, s: (b, cb(s), 0))],
        out_shape=[jax.ShapeDtypeStruct((bsz, t, RWKV_W), F32)] * 2,
        scratch_shapes=[pltpu.VMEM((2 * nb, RWKV_HEADS, RWKV_HD, RWKV_HD), F32)],
        compiler_params=_cparams(("parallel", "arbitrary")),
        name="rwkv_chunk",
    )(xs, xs, vecs, w0, w2, a0, a2, hsum)


def _mix_out_kernel(x_ref, gt_ref, of_ref, ob_ref, z_ref, yf_ref, yb_ref, xs_ref,
                    gnw_ref, vec_ref, a0_ref, a2_ref, g2_ref, hsum_ref, wo_ref, o_ref):
    nh, dv = GDN_HEADS, GDN_DK
    w_ = RWKV_W
    o = of_ref[...] + ob_ref[...]
    z = z_ref[...]
    parts = []
    for h in range(nh):
        oh = o[:, h * dv:(h + 1) * dv]
        oh = oh * lax.rsqrt(jnp.mean(oh * oh, axis=-1, keepdims=True) + NORM_EPS)
        parts.append(oh * gnw_ref[...] * _silu(z[:, h * dv:(h + 1) * dv]))
    xs = xs_ref[...]
    r = xs[:, 0:w_]
    k = xs[:, w_:2 * w_]
    v = xs[:, 2 * w_:3 * w_]
    xg = xs[:, 3 * w_ + 4 * LORA:]
    k_a, r_k = vec_ref[1:2, :], vec_ref[2:3, :]
    gn_w, gn_b = vec_ref[3:4, :], vec_ref[4:5, :]
    hsum = hsum_ref[...]
    y = yf_ref[...] + yb_ref[...]
    mean = _dot_x01(y, hsum) * (1.0 / RWKV_HD)
    yc = y - mean
    var = _dot_x01(yc * yc, hsum) * (1.0 / RWKV_HD)
    yn = yc * lax.rsqrt(var + RWKV_GN_EPS) * gn_w + gn_b
    rk = jnp.zeros_like(r)
    for d in range(2):
        xa = xs[:, 3 * w_ + 2 * LORA + d * LORA:3 * w_ + 2 * LORA + (d + 1) * LORA]
        a = _sigmoid(a0_ref[d:d + 1, :] + _dot(xa, a2_ref[d]))
        rk = rk + r * (k * (1.0 + (a - 1.0) * k_a)) * r_k
    bonus = _dot_x01(rk, hsum) * v
    gate = _dot(_sigmoid(xg), g2_ref[...])
    parts.append((yn + bonus) * gate)
    mixed = jnp.concatenate(parts, axis=-1).astype(BF16)
    o_ref[...] = x_ref[...] + gt_ref[...] * jnp.dot(mixed, wo_ref[...], preferred_element_type=F32)


def mix_out(x, gt1, o_f, o_b, p_gdn, y_f, y_b, xs, gnw, vecs, a0, a2, g2, w_out, n_ctx, tm):
    bsz, seq, d = x.shape
    off = n_ctx // tm
    nv = GDN_HEADS * GDN_DK
    ncol = xs.shape[2]
    hsum = _head_sum_matrix(RWKV_W, RWKV_HD)
    lat = lambda width, cblk=0: pl.BlockSpec((None, tm, width), lambda b, i: (b, i + off, cblk))
    full = lambda *shape: pl.BlockSpec(shape, lambda b, i: (0,) * len(shape))
    return pl.pallas_call(
        _mix_out_kernel,
        grid=(bsz, seq // tm),
        in_specs=[pl.BlockSpec((None, tm, d), lambda b, i: (b, i, 0)),
                  pl.BlockSpec((None, 1, d), lambda b, i: (b, 0, 0)),
                  lat(nv), lat(nv), lat(nv, 3), lat(RWKV_W), lat(RWKV_W), lat(ncol),
                  full(1, GDN_DK), full(8, RWKV_W), full(2, RWKV_W), full(2, LORA, RWKV_W),
                  full(2 * LORA, RWKV_W), full(RWKV_W, RWKV_W), full(nv + RWKV_W, d)],
        out_specs=pl.BlockSpec((None, tm, d), lambda b, i: (b, i, 0)),
        out_shape=jax.ShapeDtypeStruct((bsz, seq, d), F32),
        compiler_params=_cparams(("parallel", "parallel")),
        name="mix_out",
    )(x, gt1, o_f, o_b, p_gdn, y_f, y_b, xs, gnw, vecs, a0, a2, g2, hsum, w_out)


def _extract_top(s, riota, n_rows, vals_ref, idx_ref, k):
    for i in range(k):
        m = jnp.max(s, axis=0, keepdims=True)
        am = jnp.min(jnp.where(s == m, riota, float(n_rows)), axis=0, keepdims=True)
        vals_ref[i:i + 1, :] = m
        idx_ref[i:i + 1, :] = am
        s = jnp.where(riota == am, NEG_INF, s)


def _candidate_blocks(k):
    split = math.isqrt(k)
    blocks = [(False, i, 0, k // (i + 1)) for i in range(split)]
    j = 0
    while k // (j + 1) > split:
        blocks.append((True, j, split, k // (j + 1)))
        j += 1
    return blocks


def _candidate_rows(k):
    return sum(-(-(hi - lo) // 8) * 8 for _, _, lo, hi in _candidate_blocks(k))


def _top_list_rows(k):
    need = max(lo + -(-(hi - lo) // 8) * 8 for _, _, lo, hi in _candidate_blocks(k))
    return -(-need // 8) * 8


def _peer_route_kernel(x_ref, g_ref, sh_ref, sc_ref, wq_ref, keys_ref,
                       h2_ref, idx_ref, gate_ref,
                       q_scr, tv_scr, ti_scr, cs_scr, ci_scr, cf_scr, bv_scr, bi_scr):
    h = pl.program_id(1)
    dq = 2 * LANES
    kk = PEER_TOPK

    @pl.when(h == 0)
    def _():
        x = x_ref[...]
        y = x * lax.rsqrt(jnp.mean(x * x, axis=-1, keepdims=True) + NORM_EPS) * g_ref[...]
        h2 = y * (1.0 + sc_ref[...]) + sh_ref[...]
        h2_ref[...] = h2
        hb = h2.astype(BF16)
        for hh in range(PEER_HEADS):
            q_scr[hh] = jnp.dot(hb, wq_ref[:, hh * dq:(hh + 1) * dq], preferred_element_type=F32)

    q = q_scr[h]
    tm = q.shape[0]
    riota = lax.broadcasted_iota(jnp.int32, (PEER_NKEYS, tm), 0).astype(F32)
    for p in range(2):
        s_t = _dot_nt(keys_ref[p], q[:, p * LANES:(p + 1) * LANES])
        _extract_top(s_t, riota, PEER_NKEYS, tv_scr.at[p], ti_scr.at[p], kk)
    tv_scr[:, kk:, :] = jnp.zeros_like(tv_scr[:, kk:, :])
    ti_scr[:, kk:, :] = jnp.zeros_like(ti_scr[:, kk:, :])
    row = 0
    for fixed_j, fixed, lo, hi in _candidate_blocks(kk):
        n_pad = -(-(hi - lo) // 8) * 8
        r = lax.broadcasted_iota(jnp.int32, (n_pad, tm), 0).astype(F32)
        run, one = (0, 1) if fixed_j else (1, 0)
        vals = tv_scr[run, lo:lo + n_pad, :] + tv_scr[one, fixed:fixed + 1, :]
        run_i, one_i = ti_scr[run, lo:lo + n_pad, :], ti_scr[one, fixed:fixed + 1, :]
        if fixed_j:
            experts = run_i * float(PEER_NKEYS) + one_i
            flat = (r + float(lo)) * float(kk) + float(fixed)
        else:
            experts = one_i * float(PEER_NKEYS) + run_i
            flat = float(fixed * kk + lo) + r
        cs_scr[row:row + n_pad, :] = jnp.where(r < float(hi - lo), vals, NEG_INF)
        ci_scr[row:row + n_pad, :] = experts
        cf_scr[row:row + n_pad, :] = flat
        row += n_pad
    cand = cs_scr[...]
    cand_i = ci_scr[...]
    ciota = cf_scr[...]
    for i in range(kk):
        m = jnp.max(cand, axis=0, keepdims=True)
        pos = jnp.min(jnp.where(cand == m, ciota, float(kk * kk)), axis=0, keepdims=True)
        hit = ciota == pos
        bv_scr[i:i + 1, :] = m
        bi_scr[i:i + 1, :] = jnp.sum(jnp.where(hit, cand_i, 0.0), axis=0, keepdims=True)
        cand = jnp.where(hit, NEG_INF, cand)
    best = bv_scr[...]
    e = jnp.exp(best - best[0:1, :])
    gate_ref[...] = e / jnp.sum(e, axis=0, keepdims=True)
    idx_ref[...] = bi_scr[...].astype(jnp.int32)


def peer_route(x1, norm_g, sh2, sc2, wq, sub_keys, tm):
    bsz, seq, d = x1.shape
    ntok = bsz * seq
    tiles_per_b = seq // tm
    kk = PEER_TOPK
    nq = wq.shape[1]
    return pl.pallas_call(
        _peer_route_kernel,
        grid=(ntok // tm, PEER_HEADS),
        in_specs=[pl.BlockSpec((tm, d), lambda i, h: (i, 0)),
                  pl.BlockSpec((1, d), lambda i, h: (0, 0)),
                  pl.BlockSpec((None, 1, d), lambda i, h: (i // tiles_per_b, 0, 0)),
                  pl.BlockSpec((None, 1, d), lambda i, h: (i // tiles_per_b, 0, 0)),
                  pl.BlockSpec((d, nq), lambda i, h: (0, 0)),
                  pl.BlockSpec((None, 2, PEER_NKEYS, LANES), lambda i, h: (h, 0, 0, 0))],
        out_specs=[pl.BlockSpec((tm, d), lambda i, h: (i, 0)),
                   pl.BlockSpec((kk, tm), lambda i, h: (h, i)),
                   pl.BlockSpec((kk, tm), lambda i, h: (h, i))],
        out_shape=[jax.ShapeDtypeStruct((ntok, d), F32),
                   jax.ShapeDtypeStruct((PEER_HEADS * kk, ntok), jnp.int32),
                   jax.ShapeDtypeStruct((PEER_HEADS * kk, ntok), F32)],
        scratch_shapes=[pltpu.VMEM((PEER_HEADS, tm, 2 * LANES), F32),
                        pltpu.VMEM((2, _top_list_rows(kk), tm), F32),
                        pltpu.VMEM((2, _top_list_rows(kk), tm), F32),
                        pltpu.VMEM((_candidate_rows(kk), tm), F32),
                        pltpu.VMEM((_candidate_rows(kk), tm), F32),
                        pltpu.VMEM((_candidate_rows(kk), tm), F32),
                        pltpu.VMEM((kk, tm), F32), pltpu.VMEM((kk, tm), F32)],
        compiler_params=_cparams(("parallel", "arbitrary")),
        name="peer_route",
    )(x1.reshape(ntok, d), norm_g.reshape(1, d), sh2, sc2, wq, sub_keys)


GROUP = 8
N_GSLOTS = 4
AHEAD = 2
N_SLOTS = GROUP * N_GSLOTS
ROW_TILE = 8
DMA_QUEUES = 2


def _peer_expert_kernel(idx_ref, h2_ref, gate_ref, x1_ref, gt_ref, fg_ref, tbl_ref,
                        o_ref, *scratch, tt, d):
    bufs, tiles, sem = scratch[:N_SLOTS], scratch[N_SLOTS:N_SLOTS + GROUP], scratch[N_SLOTS + GROUP]
    ne = PEER_HEADS * PEER_TOPK
    step = pl.program_id(0)
    last_step = pl.num_programs(0) - 1

    def issue_token(base, gslot, j):
        slot = gslot * GROUP + j
        for k in range(ne):
            e = idx_ref[base + k]
            pltpu.make_async_copy(tbl_ref.at[e], bufs[slot].at[k],
                                  sem.at[gslot]).start(priority=k % DMA_QUEUES)

    def wait_group(gslot):
        for j in range(GROUP):
            pltpu.make_async_copy(tbl_ref.at[pl.ds(0, ne)], bufs[gslot * GROUP + j],
                                  sem.at[gslot]).wait()

    def evaluate(t, slot):
        tile = tiles[slot % GROUP]
        tile[...] = bufs[slot][...].reshape(ne, 2 * d)
        hrow = h2_ref[pl.ds(t, 1), :]
        grow = gate_ref[pl.ds(t, 1), :]
        lane = lax.broadcasted_iota(jnp.int32, (ROW_TILE, ne), 1)
        sub = lax.broadcasted_iota(jnp.int32, (ROW_TILE, ne), 0)
        acc = jnp.zeros((ROW_TILE, d), F32)
        for g in range(ne // ROW_TILE):
            rows = tile[g * ROW_TILE:(g + 1) * ROW_TILE, :]
            pre = jnp.sum(rows[:, :d] * hrow, axis=1, keepdims=True)
            act = 0.5 * pre * (1.0 + lax.erf(pre * (2.0 ** -0.5)))
            gcol = jnp.sum(jnp.where(lane == sub + g * ROW_TILE, grow, 0.0), axis=1, keepdims=True)
            acc = acc + (act * gcol) * rows[:, d:]
        y = jnp.sum(acc, axis=0, keepdims=True)
        xo = x1_ref[pl.ds(t, 1), :] + gt_ref[...] * y
        xo = xo * lax.rsqrt(jnp.mean(xo * xo, axis=-1, keepdims=True) + NORM_EPS) * fg_ref[...]
        o_ref[pl.ds(t, 1), :] = xo

    @pl.when(step == 0)
    def _():
        for g in range(AHEAD):
            for j in range(GROUP):
                issue_token((g * GROUP + j) * ne, g, j)

    def ring_turn(it, carry):
        for gs in range(N_GSLOTS):
            t0 = (it * N_GSLOTS + gs) * GROUP
            wait_group(gs)
            for j in range(GROUP):
                issue_token((t0 + AHEAD * GROUP + j) * ne, (gs + AHEAD) % N_GSLOTS, j)
                evaluate(t0 + j, gs * GROUP + j)
        return carry

    lax.fori_loop(0, tt // N_SLOTS, ring_turn, 0)

    @pl.when(step == last_step)
    def _():
        for g in range(AHEAD):
            wait_group(g)


def peer_expert(idx_tok, h2, gate_tok, x1, gt2, final_g, table, seq, tt):
    ntok, d = h2.shape
    ne = PEER_HEADS * PEER_TOPK
    tiles_per_b = seq // tt
    n_steps = ntok // tt
    assert tt % N_SLOTS == 0
    idx_steps = idx_tok.reshape(n_steps, tt * ne)
    tail = jnp.roll(idx_steps[:, :AHEAD * GROUP * ne], -1, axis=0)
    win = (tt + AHEAD * GROUP) * ne
    idx_win = jnp.concatenate([idx_steps, tail], axis=1).reshape(n_steps * win)
    return pl.pallas_call(
        functools.partial(_peer_expert_kernel, tt=tt, d=d),
        grid=(n_steps,),
        in_specs=[pl.BlockSpec((win,), lambda i: (i,), memory_space=pltpu.SMEM),
                  pl.BlockSpec((tt, d), lambda i: (i, 0)),
                  pl.BlockSpec((tt, ne), lambda i: (i, 0)),
                  pl.BlockSpec((tt, d), lambda i: (i, 0)),
                  pl.BlockSpec((None, 1, d), lambda i: (i // tiles_per_b, 0, 0)),
                  pl.BlockSpec((1, d), lambda i: (0, 0)),
                  pl.BlockSpec(memory_space=pl.ANY)],
        out_specs=pl.BlockSpec((tt, d), lambda i: (i, 0)),
        out_shape=jax.ShapeDtypeStruct((ntok, d), F32),
        scratch_shapes=([pltpu.VMEM((ne, 1, 2 * d), F32) for _ in range(N_SLOTS)]
                        + [pltpu.VMEM((ne, 2 * d), F32) for _ in range(GROUP)]
                        + [pltpu.SemaphoreType.DMA((N_GSLOTS,))]),
        compiler_params=_cparams(("arbitrary",)),
        name="peer_expert",
    )(idx_win, h2, gate_tok, x1, gt2, final_g.reshape(1, d), table)


ROW_TILE_TOKENS = 256
ROUTE_TILE_TOKENS = 1024
GATHER_STEP_TOKENS = 128


def kernel(x, c, ctx, c_ctx, ada_w, ada_b, norm1_g, w_in, gdn_conv_w, gdn_a_log, gdn_dt_bias,
           gdn_norm_w, rwkv_mu, rwkv_w0, rwkv_w2, rwkv_a0, rwkv_a2, rwkv_g2, rwkv_k_k, rwkv_k_a,
           rwkv_r_k, rwkv_gn_w, rwkv_gn_b, w_out, norm2_g, peer_w_query, peer_sub_keys,
           peer_down, peer_up, final_norm_g):
    bsz, seq, d = x.shape
    n_ctx = ctx.shape[1]
    assert ada_w.shape[0] == 1, "single-layer block"
    assert n_ctx % ROW_TILE_TOKENS == 0 and seq % ROUTE_TILE_TOKENS == 0 and seq % GRID_W == 0
    nh = GDN_HEADS
    n_qkvz = 4 * nh * GDN_DK
    gdn_cols = n_qkvz + 4 * nh

    cond = jnp.concatenate([c, c_ctx[None, :], jnp.zeros((16 - bsz - 1, d), F32)], axis=0)
    mods = ada_mod(cond, ada_w[0], ada_b[0])
    m_lat = mods[:bsz].reshape(bsz, N_MOD, 1, d)
    m_ctx = jnp.broadcast_to(mods[bsz].reshape(1, N_MOD, 1, d), (bsz, N_MOD, 1, d))
    shift1 = jnp.stack([m_ctx[:, 0], m_lat[:, 0]], axis=1)
    scale1 = jnp.stack([m_ctx[:, 1], m_lat[:, 1]], axis=1)
    gt1, sh2, sc2, gt2 = m_lat[:, 2], m_lat[:, 3], m_lat[:, 4], m_lat[:, 5]

    wl = w_in[0]
    w_gdn = jnp.concatenate([wl[:, :gdn_cols], jnp.zeros((d, LANES - 4 * nh), F32)], axis=1)
    w_rwkv = wl[:, gdn_cols:]
    xin = jnp.concatenate([ctx, x], axis=1)
    p_gdn, p_rwkv = in_proj(xin, norm1_g[0], shift1, scale1, w_gdn.astype(BF16),
                            w_rwkv.astype(BF16), n_ctx, ROW_TILE_TOKENS)

    qkv = gdn_prep(p_gdn, gdn_conv_w[0], n_ctx)
    prm = jnp.zeros((8, LANES), F32)
    prm = prm.at[0, :2 * nh].set(gdn_a_log[0].reshape(-1))
    prm = prm.at[1, :2 * nh].set(gdn_dt_bias[0].reshape(-1))
    o_f, o_b = gdn_chunk(qkv, p_gdn, prm, n_ctx)

    xs = rwkv_shift(p_rwkv, rwkv_mu[0], n_ctx)
    vecs = jnp.zeros((8, RWKV_W), F32)
    for i, vec in enumerate((rwkv_k_k, rwkv_k_a, rwkv_r_k, rwkv_gn_w, rwkv_gn_b)):
        vecs = vecs.at[i].set(vec[0])
    y_f, y_b = rwkv_chunk(xs, vecs, rwkv_w0[0], rwkv_w2[0], rwkv_a0[0], rwkv_a2[0], n_ctx)

    x1 = mix_out(x, gt1, o_f, o_b, p_gdn, y_f, y_b, xs, gdn_norm_w[0].reshape(1, GDN_DK), vecs,
                 rwkv_a0[0], rwkv_a2[0], rwkv_g2[0], w_out[0].astype(BF16), n_ctx, ROW_TILE_TOKENS)

    h2, idx_t, gate_t = peer_route(x1, norm2_g[0], sh2, sc2, peer_w_query[0].astype(BF16),
                                   peer_sub_keys[0], ROUTE_TILE_TOKENS)
    table = jnp.concatenate([peer_down[0], peer_up[0]], axis=1)[:, None, :]
    out = peer_expert(idx_t.T, h2, gate_t.T, x1.reshape(bsz * seq, d), gt2,
                      final_norm_g, table, seq, GATHER_STEP_TOKENS)
    return out.reshape(bsz, seq, d)
```

```python
import functools
import math

import jax
import jax.numpy as jnp
from jax import lax
from jax.experimental import pallas as pl
from jax.experimental.pallas import tpu as pltpu

F32 = jnp.float32
BF16 = jnp.bfloat16

GRID_W = 64
GDN_HEADS = 4
GDN_DK = 128
GDN_CONV = 5
CHUNK = 64
RWKV_HEADS = 8
RWKV_HD = 64
RWKV_W = RWKV_HEADS * RWKV_HD
LORA = 64
PEER_HEADS = 8
PEER_NKEYS = 128
PEER_TOPK = 16
N_MOD = 6

NORM_EPS = 1e-6
L2_EPS = 1e-6
RWKV_GN_EPS = 64e-5

LANES = 128
VMEM_LIMIT = 56 * 1024 * 1024
NEG_INF = float("-inf")


def _cparams(sem):
    return pltpu.CompilerParams(dimension_semantics=sem, vmem_limit_bytes=VMEM_LIMIT)


def _dot(a, b):
    return jnp.dot(a.astype(BF16), b.astype(BF16), preferred_element_type=F32)


def _dot_nt(a, b):
    return lax.dot_general(a.astype(BF16), b.astype(BF16), (((1,), (1,)), ((), ())),
                           preferred_element_type=F32)


def _dot_tn(a, b):
    return lax.dot_general(a.astype(BF16), b.astype(BF16), (((0,), (0,)), ((), ())),
                           preferred_element_type=F32)


def _dot_hi(a, b):
    return jnp.dot(a, b, preferred_element_type=F32, precision=lax.Precision.HIGHEST)


def _split3(x):
    hi = x.astype(BF16)
    rest = x - hi.astype(F32)
    mid = rest.astype(BF16)
    lo = (rest - mid.astype(F32)).astype(BF16)
    return hi, mid, lo


def _dot_x01(a, m01):
    m = m01.astype(BF16)
    return sum(jnp.dot(p, m, preferred_element_type=F32) for p in _split3(a))


def _dot_01x(m01, b):
    m = m01.astype(BF16)
    return sum(jnp.dot(m, p, preferred_element_type=F32) for p in _split3(b))


def _sigmoid(x):
    return 1.0 / (1.0 + jnp.exp(-x))


def _silu(x):
    return x * _sigmoid(x)


def _softplus(x):
    return jnp.maximum(x, 0.0) + jnp.log(1.0 + jnp.exp(-jnp.abs(x)))


def _unit_lower_inverses(xs):
    c = xs[0].shape[0]
    eye = (lax.broadcasted_iota(jnp.int32, (c, c), 0) ==
           lax.broadcasted_iota(jnp.int32, (c, c), 1)).astype(F32)
    prods = [eye + x for x in xs]
    pows = [_dot(x, x) for x in xs]
    for _ in range(int(math.log2(c)) - 2):
        both = [_dot(jnp.concatenate([p, x], axis=0), x) for p, x in zip(prods, pows)]
        prods = [p + b[:c] for p, b in zip(prods, both)]
        pows = [b[c:] for b in both]
    return [p + _dot(p, x) for p, x in zip(prods, pows)]


def _order_masks(c, rev):
    r = lax.broadcasted_iota(jnp.int32, (c, c), 0)
    s = lax.broadcasted_iota(jnp.int32, (c, c), 1)
    if rev:
        return s >= r, s > r
    return s <= r, s < r


def _ada_kernel(c_ref, w_ref, b_ref, o_ref):
    o_ref[...] = _dot_hi(_silu(c_ref[...]), w_ref[...]) + b_ref[...]


def ada_mod(cond, w, b):
    n, d = cond.shape
    cols = w.shape[1]
    tn = 1024
    return pl.pallas_call(
        _ada_kernel,
        grid=(cols // tn,),
        in_specs=[pl.BlockSpec((n, d), lambda j: (0, 0)),
                  pl.BlockSpec((d, tn), lambda j: (0, j)),
                  pl.BlockSpec((1, tn), lambda j: (0, j))],
        out_specs=pl.BlockSpec((n, tn), lambda j: (0, j)),
        out_shape=jax.ShapeDtypeStruct((n, cols), F32),
        compiler_params=_cparams(("arbitrary",)),
        name="ada_mod",
    )(cond, w, b.reshape(1, cols))


def _in_proj_kernel(ctx_ref, x_ref, g_ref, sh_ref, sc_ref, wg_ref, wr_ref, pg_ref, pr_ref,
                    *, nctx_tiles):
    x = jnp.where(pl.program_id(1) < nctx_tiles, ctx_ref[...], x_ref[...])
    y = x * lax.rsqrt(jnp.mean(x * x, axis=-1, keepdims=True) + NORM_EPS) * g_ref[...]
    h = (y * (1.0 + sc_ref[...]) + sh_ref[...]).astype(BF16)
    pg_ref[...] = jnp.dot(h, wg_ref[...], preferred_element_type=F32)
    pr_ref[...] = jnp.dot(h, wr_ref[...], preferred_element_type=F32)


def in_proj(ctx, x, norm_g, shift, scale, w_gdn, w_rwkv, tm):
    bsz, n_ctx, d = ctx.shape
    t = n_ctx + x.shape[1]
    cg, cr = w_gdn.shape[1], w_rwkv.shape[1]
    nctx_tiles = n_ctx // tm

    def mod_map(b, i):
        return (b, jnp.where(i < nctx_tiles, 0, 1), 0, 0)

    return pl.pallas_call(
        functools.partial(_in_proj_kernel, nctx_tiles=nctx_tiles),
        grid=(bsz, t // tm),
        in_specs=[pl.BlockSpec((None, tm, d), lambda b, i: (b, jnp.minimum(i, nctx_tiles - 1), 0)),
                  pl.BlockSpec((None, tm, d), lambda b, i: (b, jnp.maximum(i - nctx_tiles, 0), 0)),
                  pl.BlockSpec((1, d), lambda b, i: (0, 0)),
                  pl.BlockSpec((None, None, 1, d), mod_map),
                  pl.BlockSpec((None, None, 1, d), mod_map),
                  pl.BlockSpec((d, cg), lambda b, i: (0, 0)),
                  pl.BlockSpec((d, cr), lambda b, i: (0, 0))],
        out_specs=[pl.BlockSpec((None, tm, cg), lambda b, i: (b, i, 0)),
                   pl.BlockSpec((None, tm, cr), lambda b, i: (b, i, 0))],
        out_shape=[jax.ShapeDtypeStruct((bsz, t, cg), F32),
                   jax.ShapeDtypeStruct((bsz, t, cr), F32)],
        compiler_params=_cparams(("parallel", "parallel")),
        name="in_proj",
    )(ctx, x, norm_g.reshape(1, d), shift, scale, w_gdn, w_rwkv)


def _gdn_prep_kernel(p_ref, w_ref, o_ref, *, n_ctx):
    j = pl.program_id(1)
    x = p_ref[...]
    t_len = x.shape[0]
    w = w_ref[...]
    t = lax.broadcasted_iota(jnp.int32, x.shape, 0)
    is_ctx = t < n_ctx
    lo = jnp.where(is_ctx, 0, n_ctx)
    hi = jnp.where(is_ctx, n_ctx, t_len)
    pad = (GDN_CONV - 1) // 2
    acc = x * w[pad:pad + 1, :]
    for s in range(-pad, pad + 1):
        if s == 0:
            continue
        xs = pltpu.roll(x, (-s) % t_len, 0)
        valid = (t + s >= lo) & (t + s < hi)
        acc = acc + jnp.where(valid, xs, 0.0) * w[s + pad:s + pad + 1, :]
    y = _silu(acc)
    inv = lax.rsqrt(jnp.sum(y * y, axis=-1, keepdims=True) + L2_EPS)
    fac = jnp.where(j < GDN_HEADS, inv * (GDN_DK ** -0.5), jnp.where(j < 2 * GDN_HEADS, inv, 1.0))
    o_ref[...] = y * fac


def gdn_prep(p_gdn, conv_w, n_ctx):
    bsz, t, _ = p_gdn.shape
    ncol = conv_w.shape[1]
    return pl.pallas_call(
        functools.partial(_gdn_prep_kernel, n_ctx=n_ctx),
        grid=(bsz, ncol // LANES),
        in_specs=[pl.BlockSpec((None, t, LANES), lambda b, j: (b, 0, j)),
                  pl.BlockSpec((GDN_CONV, LANES), lambda b, j: (0, j))],
        out_specs=pl.BlockSpec((None, t, LANES), lambda b, j: (b, 0, j)),
        out_shape=jax.ShapeDtypeStruct((bsz, t, ncol), F32),
        compiler_params=_cparams(("parallel", "parallel")),
        name="gdn_prep",
    )(p_gdn, conv_w)


def _gdn_chunk_kernel(qf_ref, qb_ref, gf_ref, gb_ref, prm_ref, of_ref, ob_ref, s_ref):
    @pl.when(pl.program_id(1) == 0)
    def _():
        s_ref[...] = jnp.zeros_like(s_ref)

    nh, dk = GDN_HEADS, GDN_DK
    nb = qf_ref.shape[0]
    prm = prm_ref[...]
    a_log, dt_bias = prm[0:1, :], prm[1:2, :]
    q, k, v, beta, g_col, g_last, gamma, strict = [], [], [], [], [], [], [], []
    for bi, d in [(bi, d) for bi in range(nb) for d in range(2)]:
        qkv = (qf_ref, qb_ref)[d][bi]
        gates = (gf_ref, gb_ref)[d][bi]
        c = qkv.shape[0]
        incl, strict_d = _order_masks(c, d == 1)
        g_all = -jnp.exp(a_log) * _softplus(gates + dt_bias)
        beta_all = _sigmoid(gates)
        big_g = _dot_01x(incl.astype(F32), g_all)
        big_g_t = big_g.T
        last = 0 if d == 1 else c - 1
        for h in range(nh):
            col = d * nh + h
            q.append(qkv[:, h * dk:(h + 1) * dk])
            k.append(qkv[:, (nh + h) * dk:(nh + h + 1) * dk])
            v.append(qkv[:, (2 * nh + h) * dk:(2 * nh + h + 1) * dk])
            beta.append(beta_all[:, 2 * nh + col:2 * nh + col + 1])
            g_col.append(big_g[:, col:col + 1])
            g_last.append(big_g[last:last + 1, col:col + 1])
            g_row = big_g_t[col:col + 1, :]
            gamma.append(jnp.where(incl, jnp.exp(jnp.where(incl, g_col[-1] - g_row, 0.0)), 0.0))
            strict.append(strict_d)
    n = len(q)
    c = q[0].shape[0]
    e_g = [jnp.exp(g) for g in g_col]
    kq = [_dot_nt(jnp.concatenate([k[i], q[i]], axis=0), k[i]) for i in range(n)]
    a_mat = [jnp.where(strict[i], beta[i] * kq[i][:c] * gamma[i], 0.0) for i in range(n)]
    qk = [kq[i][c:] * gamma[i] for i in range(n)]
    t_inv = _unit_lower_inverses([-a for a in a_mat])
    uw = [_dot(t_inv[i], jnp.concatenate([beta[i] * v[i], (beta[i] * e_g[i]) * k[i]], axis=1))
          for i in range(n)]
    s = [s_ref[i // nh, i % nh] for i in range(n)]
    ws = [_dot(jnp.concatenate([uw[i][:, dk:], q[i] * e_g[i]], axis=0), s[i]) for i in range(n)]
    v_new = [uw[i][:, :dk] - ws[i][:c] for i in range(n)]
    o = [ws[i][c:] + _dot(qk[i], v_new[i]) for i in range(n)]
    s_new = [s[i] * jnp.exp(g_last[i]) + _dot_tn(k[i] * jnp.exp(g_last[i] - g_col[i]), v_new[i])
             for i in range(n)]
    for i in range(n):
        s_ref[i // nh, i % nh] = s_new[i]
    for bi in range(nb):
        base = 2 * nh * bi
        of_ref[bi] = jnp.concatenate(o[base:base + nh], axis=-1)
        ob_ref[bi] = jnp.concatenate(o[base + nh:base + 2 * nh], axis=-1)


GDN_BATCH_PER_STEP = 4
RWKV_BATCH_PER_STEP = 2


def _rev_chunk_map(n_ctx_chunks, n_chunks):
    def cb(s):
        return jnp.where(s < n_ctx_chunks, n_ctx_chunks - 1 - s, n_chunks + n_ctx_chunks - 1 - s)
    return cb


def gdn_chunk(qkv, p_gdn, prm, n_ctx):
    bsz, t, ncol = qkv.shape
    nv = GDN_HEADS * GDN_DK
    n_chunks = t // CHUNK
    cb = _rev_chunk_map(n_ctx // CHUNK, n_chunks)
    gate_tile = p_gdn.shape[2] // LANES - 1
    nb = math.gcd(bsz, GDN_BATCH_PER_STEP)
    return pl.pallas_call(
        _gdn_chunk_kernel,
        grid=(bsz // nb, n_chunks),
        in_specs=[pl.BlockSpec((nb, CHUNK, ncol), lambda b, s: (b, s, 0)),
                  pl.BlockSpec((nb, CHUNK, ncol), lambda b, s: (b, cb(s), 0)),
                  pl.BlockSpec((nb, CHUNK, LANES), lambda b, s: (b, s, gate_tile)),
                  pl.BlockSpec((nb, CHUNK, LANES), lambda b, s: (b, cb(s), gate_tile)),
                  pl.BlockSpec((8, LANES), lambda b, s: (0, 0))],
        out_specs=[pl.BlockSpec((nb, CHUNK, nv), lambda b, s: (b, s, 0)),
                   pl.BlockSpec((nb, CHUNK, nv), lambda b, s: (b, cb(s), 0))],
        out_shape=[jax.ShapeDtypeStruct((bsz, t, nv), F32)] * 2,
        scratch_shapes=[pltpu.VMEM((2 * nb, GDN_HEADS, GDN_DK, GDN_DK), F32)],
        compiler_params=_cparams(("parallel", "arbitrary")),
        name="gdn_chunk",
    )(qkv, qkv, p_gdn, p_gdn, prm)


def _rwkv_shift_kernel(p_ref, mu_ref, o_ref, *, n_ctx, n_cols):
    j = pl.program_id(1)
    x = p_ref[...]
    t_len = x.shape[0]
    t = lax.broadcasted_iota(jnp.int32, x.shape, 0)
    ch = lax.broadcasted_iota(jnp.int32, x.shape, 1) + j * LANES
    is_ctx = t < n_ctx
    col = (t - n_ctx) % GRID_W
    prev1 = pltpu.roll(x, 1, 0)
    next1 = pltpu.roll(x, t_len - 1, 0)
    up = pltpu.roll(x, GRID_W, 0)
    down = pltpu.roll(x, t_len - GRID_W, 0)
    quarter = n_cols // 4
    half = n_cols // 2
    ctx_sh = jnp.where(ch < half,
                       jnp.where(t >= 1, prev1, 0.0),
                       jnp.where(t < n_ctx - 1, next1, 0.0))
    lat_sh = jnp.where(ch < quarter, jnp.where(col >= 1, prev1, 0.0),
                       jnp.where(ch < 2 * quarter, jnp.where(col < GRID_W - 1, next1, 0.0),
                                 jnp.where(ch < 3 * quarter,
                                           jnp.where(t >= n_ctx + GRID_W, up, 0.0),
                                           jnp.where(t < t_len - GRID_W, down, 0.0))))
    sh = jnp.where(is_ctx, ctx_sh, lat_sh)
    o_ref[...] = x + mu_ref[...] * (sh - x)


def rwkv_shift(p_rwkv, mu, n_ctx):
    bsz, t, ncol = p_rwkv.shape
    return pl.pallas_call(
        functools.partial(_rwkv_shift_kernel, n_ctx=n_ctx, n_cols=ncol),
        grid=(bsz, ncol // LANES),
        in_specs=[pl.BlockSpec((None, t, LANES), lambda b, j: (b, 0, j)),
                  pl.BlockSpec((1, LANES), lambda b, j: (0, j))],
        out_specs=pl.BlockSpec((None, t, LANES), lambda b, j: (b, 0, j)),
        out_shape=jax.ShapeDtypeStruct((bsz, t, ncol), F32),
        compiler_params=_cparams(("parallel", "parallel")),
        name="rwkv_shift",
    )(p_rwkv, mu.reshape(1, ncol))


def _rwkv_gates(xs, d, w0_ref, w2_ref, a0_ref, a2_ref):
    w_ = RWKV_W
    xw = xs[:, 3 * w_ + d * LORA:3 * w_ + (d + 1) * LORA]
    xa = xs[:, 3 * w_ + 2 * LORA + d * LORA:3 * w_ + 2 * LORA + (d + 1) * LORA]
    w_pre = w0_ref[d:d + 1, :] + _dot(jnp.tanh(xw), w2_ref[d])
    log_w = -math.exp(-0.5) * _sigmoid(w_pre)
    a = _sigmoid(a0_ref[d:d + 1, :] + _dot(xa, a2_ref[d]))
    return log_w, a


def _rwkv_chunk_kernel(xf_ref, xb_ref, vec_ref, w0_ref, w2_ref, a0_ref, a2_ref, hsum_ref,
                       yf_ref, yb_ref, p_ref):
    @pl.when(pl.program_id(1) == 0)
    def _():
        p_ref[...] = jnp.zeros_like(p_ref)

    w_, n, nh = RWKV_W, RWKV_HD, RWKV_HEADS
    nb = xf_ref.shape[0]
    k_k, k_a = vec_ref[0:1, :], vec_ref[1:2, :]
    lhs, rhs_t, tail, vals, w_last, p0, strict, incl = [], [], [], [], [], [], [], []
    for bi, d in [(bi, d) for bi in range(nb) for d in range(2)]:
        xs = (xf_ref, xb_ref)[d][bi]
        c = xs.shape[0]
        incl_d, strict_d = _order_masks(c, d == 1)
        r = xs[:, 0:w_]
        k = xs[:, w_:2 * w_]
        v = xs[:, 2 * w_:3 * w_]
        log_w, a = _rwkv_gates(xs, d, w0_ref, w2_ref, a0_ref, a2_ref)
        kk = k * k_k
        kk = kk * lax.rsqrt(_dot_x01(kk * kk, hsum_ref[...]) + L2_EPS)
        k_dir = k * (1.0 + (a - 1.0) * k_a)
        alpha = -(kk * a)
        big_g = _dot_01x(incl_d.astype(F32), log_w)
        last = 0 if d == 1 else c - 1
        g_last = big_g[last:last + 1, :]
        e_neg = jnp.exp(-big_g)
        e_tail = jnp.exp(g_last - big_g)
        r_t = r * jnp.exp(big_g)
        b_t = kk * jnp.exp(big_g - log_w)
        a_t = alpha * e_neg
        k_t = k_dir * e_neg
        a_h = alpha * e_tail
        k_h = k_dir * e_tail
        w_l = jnp.exp(g_last)
        for h in range(nh):
            sl = slice(h * n, (h + 1) * n)
            lhs.append(jnp.concatenate([b_t[:, sl], r_t[:, sl]], axis=0))
            rhs_t.append(jnp.concatenate([a_t[:, sl], k_t[:, sl]], axis=0))
            tail.append(jnp.concatenate([a_h[:, sl], k_h[:, sl]], axis=0))
            vals.append(v[:, sl])
            w_last.append(w_l[:, sl])
            p0.append(p_ref[2 * bi + d, h])
            strict.append(strict_d)
            incl.append(incl_d)
    m = len(lhs)
    c = vals[0].shape[0]
    quad = [_dot_nt(lhs[i], rhs_t[i]) for i in range(m)]
    a_ba = [jnp.where(strict[i], quad[i][:c, :c], 0.0) for i in range(m)]
    a_k = [jnp.concatenate([jnp.where(strict[i], quad[i][:c, c:], 0.0),
                            jnp.where(incl[i], quad[i][c:, c:], 0.0)], axis=0) for i in range(m)]
    a_ra = [jnp.where(incl[i], quad[i][c:, :c], 0.0) for i in range(m)]
    from_state = [_dot_nt(lhs[i], p0[i]) for i in range(m)]
    from_vals = [_dot(a_k[i], vals[i]) for i in range(m)]
    t_inv = _unit_lower_inverses(a_ba)
    u = [_dot(t_inv[i], from_state[i][:c] + from_vals[i][:c]) for i in range(m)]
    y = [from_state[i][c:] + _dot(a_ra[i], u[i]) + from_vals[i][c:] for i in range(m)]
    p_new = [p0[i] * w_last[i] + _dot_tn(jnp.concatenate([u[i], vals[i]], axis=0), tail[i])
             for i in range(m)]
    for i in range(m):
        p_ref[i // nh, i % nh] = p_new[i]
    for bi in range(nb):
        base = 2 * nh * bi
        yf_ref[bi] = jnp.concatenate(y[base:base + nh], axis=-1)
        yb_ref[bi] = jnp.concatenate(y[base + nh:base + 2 * nh], axis=-1)


def _head_sum_matrix(width, group):
    i = jnp.arange(width)
    return (i[:, None] // group == i[None, :] // group).astype(F32)


def rwkv_chunk(xs, vecs, w0, w2, a0, a2, n_ctx):
    bsz, t, ncol = xs.shape
    n_chunks = t // CHUNK
    cb = _rev_chunk_map(n_ctx // CHUNK, n_chunks)
    hsum = _head_sum_matrix(RWKV_W, RWKV_HD)
    full = lambda *shape: pl.BlockSpec(shape, lambda b, s: (0,) * len(shape))
    nb = math.gcd(bsz, RWKV_BATCH_PER_STEP)
    return pl.pallas_call(
        _rwkv_chunk_kernel,
        grid=(bsz // nb, n_chunks),
        in_specs=[pl.BlockSpec((nb, CHUNK, ncol), lambda b, s: (b, s, 0)),
                  pl.BlockSpec((nb, CHUNK, ncol), lambda b, s: (b, cb(s), 0)),
                  full(8, RWKV_W), full(2, RWKV_W), full(2, LORA, RWKV_W),
                  full(2, RWKV_W), full(2, LORA, RWKV_W), full(RWKV_W, RWKV_W)],
        out_specs=[pl.BlockSpec((nb, CHUNK, RWKV_W), lambda b, s: (b, s, 0)),
                   pl.BlockSpec((nb, CHUNK, RWKV_W), lambda b, s: (b, cb(s), 0))],
        out_shape=[jax.ShapeDtypeStruct((bsz, t, RWKV_W), F32)] * 2,
        scratch_shapes=[pltpu.VMEM((2 * nb, RWKV_HEADS, RWKV_HD, RWKV_HD), F32)],
        compiler_params=_cparams(("parallel", "arbitrary")),
        name="rwkv_chunk",
    )(xs, xs, vecs, w0, w2, a0, a2, hsum)


def _mix_out_kernel(x_ref, gt_ref, of_ref, ob_ref, z_ref, yf_ref, yb_ref, xs_ref,
                    gnw_ref, vec_ref, a0_ref, a2_ref, g2_ref, hsum_ref, wo_ref, o_ref):
    nh, dv = GDN_HEADS, GDN_DK
    w_ = RWKV_W
    o = of_ref[...] + ob_ref[...]
    z = z_ref[...]
    parts = []
    for h in range(nh):
        oh = o[:, h * dv:(h + 1) * dv]
        oh = oh * lax.rsqrt(jnp.mean(oh * oh, axis=-1, keepdims=True) + NORM_EPS)
        parts.append(oh * gnw_ref[...] * _silu(z[:, h * dv:(h + 1) * dv]))
    xs = xs_ref[...]
    r = xs[:, 0:w_]
    k = xs[:, w_:2 * w_]
    v = xs[:, 2 * w_:3 * w_]
    xg = xs[:, 3 * w_ + 4 * LORA:]
    k_a, r_k = vec_ref[1:2, :], vec_ref[2:3, :]
    gn_w, gn_b = vec_ref[3:4, :], vec_ref[4:5, :]
    hsum = hsum_ref[...]
    y = yf_ref[...] + yb_ref[...]
    mean = _dot_x01(y, hsum) * (1.0 / RWKV_HD)
    yc = y - mean
    var = _dot_x01(yc * yc, hsum) * (1.0 / RWKV_HD)
    yn = yc * lax.rsqrt(var + RWKV_GN_EPS) * gn_w + gn_b
    rk = jnp.zeros_like(r)
    for d in range(2):
        xa = xs[:, 3 * w_ + 2 * LORA + d * LORA:3 * w_ + 2 * LORA + (d + 1) * LORA]
        a = _sigmoid(a0_ref[d:d + 1, :] + _dot(xa, a2_ref[d]))
        rk = rk + r * (k * (1.0 + (a - 1.0) * k_a)) * r_k
    bonus = _dot_x01(rk, hsum) * v
    gate = _dot(_sigmoid(xg), g2_ref[...])
    parts.append((yn + bonus) * gate)
    mixed = jnp.concatenate(parts, axis=-1).astype(BF16)
    o_ref[...] = x_ref[...] + gt_ref[...] * jnp.dot(mixed, wo_ref[...], preferred_element_type=F32)


def mix_out(x, gt1, o_f, o_b, p_gdn, y_f, y_b, xs, gnw, vecs, a0, a2, g2, w_out, n_ctx, tm):
    bsz, seq, d = x.shape
    off = n_ctx // tm
    nv = GDN_HEADS * GDN_DK
    ncol = xs.shape[2]
    hsum = _head_sum_matrix(RWKV_W, RWKV_HD)
    lat = lambda width, cblk=0: pl.BlockSpec((None, tm, width), lambda b, i: (b, i + off, cblk))
    full = lambda *shape: pl.BlockSpec(shape, lambda b, i: (0,) * len(shape))
    return pl.pallas_call(
        _mix_out_kernel,
        grid=(bsz, seq // tm),
        in_specs=[pl.BlockSpec((None, tm, d), lambda b, i: (b, i, 0)),
                  pl.BlockSpec((None, 1, d), lambda b, i: (b, 0, 0)),
                  lat(nv), lat(nv), lat(nv, 3), lat(RWKV_W), lat(RWKV_W), lat(ncol),
                  full(1, GDN_DK), full(8, RWKV_W), full(2, RWKV_W), full(2, LORA, RWKV_W),
                  full(2 * LORA, RWKV_W), full(RWKV_W, RWKV_W), full(nv + RWKV_W, d)],
        out_specs=pl.BlockSpec((None, tm, d), lambda b, i: (b, i, 0)),
        out_shape=jax.ShapeDtypeStruct((bsz, seq, d), F32),
        compiler_params=_cparams(("parallel", "parallel")),
        name="mix_out",
    )(x, gt1, o_f, o_b, p_gdn, y_f, y_b, xs, gnw, vecs, a0, a2, g2, hsum, w_out)


def _extract_top(s, riota, n_rows, vals_ref, idx_ref, k):
    for i in range(k):
        m = jnp.max(s, axis=0, keepdims=True)
        am = jnp.min(jnp.where(s == m, riota, float(n_rows)), axis=0, keepdims=True)
        vals_ref[i:i + 1, :] = m
        idx_ref[i:i + 1, :] = am
        s = jnp.where(riota == am, NEG_INF, s)


def _candidate_blocks(k):
    split = math.isqrt(k)
    blocks = [(False, i, 0, k // (i + 1)) for i in range(split)]
    j = 0
    while k // (j + 1) > split:
        blocks.append((True, j, split, k // (j + 1)))
        j += 1
    return blocks


def _candidate_rows(k):
    return sum(-(-(hi - lo) // 8) * 8 for _, _, lo, hi in _candidate_blocks(k))


def _top_list_rows(k):
    need = max(lo + -(-(hi - lo) // 8) * 8 for _, _, lo, hi in _candidate_blocks(k))
    return -(-need // 8) * 8


def _peer_route_kernel(x_ref, g_ref, sh_ref, sc_ref, wq_ref, keys_ref,
                       h2_ref, idx_ref, gate_ref,
                       q_scr, tv_scr, ti_scr, cs_scr, ci_scr, cf_scr, bv_scr, bi_scr):
    h = pl.program_id(1)
    dq = 2 * LANES
    kk = PEER_TOPK

    @pl.when(h == 0)
    def _():
        x = x_ref[...]
        y = x * lax.rsqrt(jnp.mean(x * x, axis=-1, keepdims=True) + NORM_EPS) * g_ref[...]
        h2 = y * (1.0 + sc_ref[...]) + sh_ref[...]
        h2_ref[...] = h2
        hb = h2.astype(BF16)
        for hh in range(PEER_HEADS):
            q_scr[hh] = jnp.dot(hb, wq_ref[:, hh * dq:(hh + 1) * dq], preferred_element_type=F32)

    q = q_scr[h]
    tm = q.shape[0]
    riota = lax.broadcasted_iota(jnp.int32, (PEER_NKEYS, tm), 0).astype(F32)
    for p in range(2):
        s_t = _dot_nt(keys_ref[p], q[:, p * LANES:(p + 1) * LANES])
        _extract_top(s_t, riota, PEER_NKEYS, tv_scr.at[p], ti_scr.at[p], kk)
    tv_scr[:, kk:, :] = jnp.zeros_like(tv_scr[:, kk:, :])
    ti_scr[:, kk:, :] = jnp.zeros_like(ti_scr[:, kk:, :])
    row = 0
    for fixed_j, fixed, lo, hi in _candidate_blocks(kk):
        n_pad = -(-(hi - lo) // 8) * 8
        r = lax.broadcasted_iota(jnp.int32, (n_pad, tm), 0).astype(F32)
        run, one = (0, 1) if fixed_j else (1, 0)
        vals = tv_scr[run, lo:lo + n_pad, :] + tv_scr[one, fixed:fixed + 1, :]
        run_i, one_i = ti_scr[run, lo:lo + n_pad, :], ti_scr[one, fixed:fixed + 1, :]
        if fixed_j:
            experts = run_i * float(PEER_NKEYS) + one_i
            flat = (r + float(lo)) * float(kk) + float(fixed)
        else:
            experts = one_i * float(PEER_NKEYS) + run_i
            flat = float(fixed * kk + lo) + r
        cs_scr[row:row + n_pad, :] = jnp.where(r < float(hi - lo), vals, NEG_INF)
        ci_scr[row:row + n_pad, :] = experts
        cf_scr[row:row + n_pad, :] = flat
        row += n_pad
    cand = cs_scr[...]
    cand_i = ci_scr[...]
    ciota = cf_scr[...]
    for i in range(kk):
        m = jnp.max(cand, axis=0, keepdims=True)
        pos = jnp.min(jnp.where(cand == m, ciota, float(kk * kk)), axis=0, keepdims=True)
        hit = ciota == pos
        bv_scr[i:i + 1, :] = m
        bi_scr[i:i + 1, :] = jnp.sum(jnp.where(hit, cand_i, 0.0), axis=0, keepdims=True)
        cand = jnp.where(hit, NEG_INF, cand)
    best = bv_scr[...]
    e = jnp.exp(best - best[0:1, :])
    gate_ref[...] = e / jnp.sum(e, axis=0, keepdims=True)
    idx_ref[...] = bi_scr[...].astype(jnp.int32)


def peer_route(x1, norm_g, sh2, sc2, wq, sub_keys, tm):
    bsz, seq, d = x1.shape
    ntok = bsz * seq
    tiles_per_b = seq // tm
    kk = PEER_TOPK
    nq = wq.shape[1]
    return pl.pallas_call(
        _peer_route_kernel,
        grid=(ntok // tm, PEER_HEADS),
        in_specs=[pl.BlockSpec((tm, d), lambda i, h: (i, 0)),
                  pl.BlockSpec((1, d), lambda i, h: (0, 0)),
                  pl.BlockSpec((None, 1, d), lambda i, h: (i // tiles_per_b, 0, 0)),
                  pl.BlockSpec((None, 1, d), lambda i, h: (i // tiles_per_b, 0, 0)),
                  pl.BlockSpec((d, nq), lambda i, h: (0, 0)),
                  pl.BlockSpec((None, 2, PEER_NKEYS, LANES), lambda i, h: (h, 0, 0, 0))],
        out_specs=[pl.BlockSpec((tm, d), lambda i, h: (i, 0)),
                   pl.BlockSpec((kk, tm), lambda i, h: (h, i)),
                   pl.BlockSpec((kk, tm), lambda i, h: (h, i))],
        out_shape=[jax.ShapeDtypeStruct((ntok, d), F32),
                   jax.ShapeDtypeStruct((PEER_HEADS * kk, ntok), jnp.int32),
                   jax.ShapeDtypeStruct((PEER_HEADS * kk, ntok), F32)],
        scratch_shapes=[pltpu.VMEM((PEER_HEADS, tm, 2 * LANES), F32),
                        pltpu.VMEM((2, _top_list_rows(kk), tm), F32),
                        pltpu.VMEM((2, _top_list_rows(kk), tm), F32),
                        pltpu.VMEM((_candidate_rows(kk), tm), F32),
                        pltpu.VMEM((_candidate_rows(kk), tm), F32),
                        pltpu.VMEM((_candidate_rows(kk), tm), F32),
                        pltpu.VMEM((kk, tm), F32), pltpu.VMEM((kk, tm), F32)],
        compiler_params=_cparams(("parallel", "arbitrary")),
        name="peer_route",
    )(x1.reshape(ntok, d), norm_g.reshape(1, d), sh2, sc2, wq, sub_keys)


GROUP = 8
N_GSLOTS = 4
AHEAD = 2
N_SLOTS = GROUP * N_GSLOTS
ROW_TILE = 8
DMA_QUEUES = 2


def _peer_expert_kernel(idx_ref, h2_ref, gate_ref, x1_ref, gt_ref, fg_ref, tbl_ref,
                        o_ref, *scratch, tt, d):
    bufs, tiles, sem = scratch[:N_SLOTS], scratch[N_SLOTS:N_SLOTS + GROUP], scratch[N_SLOTS + GROUP]
    ne = PEER_HEADS * PEER_TOPK
    step = pl.program_id(0)
    last_step = pl.num_programs(0) - 1

    def issue_token(base, gslot, j):
        slot = gslot * GROUP + j
        for k in range(ne):
            e = idx_ref[base + k]
            pltpu.make_async_copy(tbl_ref.at[e], bufs[slot].at[k],
                                  sem.at[gslot]).start(priority=k % DMA_QUEUES)

    def wait_group(gslot):
        for j in range(GROUP):
            pltpu.make_async_copy(tbl_ref.at[pl.ds(0, ne)], bufs[gslot * GROUP + j],
                                  sem.at[gslot]).wait()

    def evaluate(t, slot):
        tile = tiles[slot % GROUP]
        tile[...] = bufs[slot][...].reshape(ne, 2 * d)
        hrow = h2_ref[pl.ds(t, 1), :]
        grow = gate_ref[pl.ds(t, 1), :]
        lane = lax.broadcasted_iota(jnp.int32, (ROW_TILE, ne), 1)
        sub = lax.broadcasted_iota(jnp.int32, (ROW_TILE, ne), 0)
        acc = jnp.zeros((ROW_TILE, d), F32)
        for g in range(ne // ROW_TILE):
            rows = tile[g * ROW_TILE:(g + 1) * ROW_TILE, :]
            pre = jnp.sum(rows[:, :d] * hrow, axis=1, keepdims=True)
            act = 0.5 * pre * (1.0 + lax.erf(pre * (2.0 ** -0.5)))
            gcol = jnp.sum(jnp.where(lane == sub + g * ROW_TILE, grow, 0.0), axis=1, keepdims=True)
            acc = acc + (act * gcol) * rows[:, d:]
        y = jnp.sum(acc, axis=0, keepdims=True)
        xo = x1_ref[pl.ds(t, 1), :] + gt_ref[...] * y
        xo = xo * lax.rsqrt(jnp.mean(xo * xo, axis=-1, keepdims=True) + NORM_EPS) * fg_ref[...]
        o_ref[pl.ds(t, 1), :] = xo

    @pl.when(step == 0)
    def _():
        for g in range(AHEAD):
            for j in range(GROUP):
                issue_token((g * GROUP + j) * ne, g, j)

    def ring_turn(it, carry):
        for gs in range(N_GSLOTS):
            t0 = (it * N_GSLOTS + gs) * GROUP
            wait_group(gs)
            for j in range(GROUP):
                issue_token((t0 + AHEAD * GROUP + j) * ne, (gs + AHEAD) % N_GSLOTS, j)
                evaluate(t0 + j, gs * GROUP + j)
        return carry

    lax.fori_loop(0, tt // N_SLOTS, ring_turn, 0)

    @pl.when(step == last_step)
    def _():
        for g in range(AHEAD):
            wait_group(g)


def peer_expert(idx_tok, h2, gate_tok, x1, gt2, final_g, table, seq, tt):
    ntok, d = h2.shape
    ne = PEER_HEADS * PEER_TOPK
    tiles_per_b = seq // tt
    n_steps = ntok // tt
    assert tt % N_SLOTS == 0
    idx_steps = idx_tok.reshape(n_steps, tt * ne)
    tail = jnp.roll(idx_steps[:, :AHEAD * GROUP * ne], -1, axis=0)
    win = (tt + AHEAD * GROUP) * ne
    idx_win = jnp.concatenate([idx_steps, tail], axis=1).reshape(n_steps * win)
    return pl.pallas_call(
        functools.partial(_peer_expert_kernel, tt=tt, d=d),
        grid=(n_steps,),
        in_specs=[pl.BlockSpec((win,), lambda i: (i,), memory_space=pltpu.SMEM),
                  pl.BlockSpec((tt, d), lambda i: (i, 0)),
                  pl.BlockSpec((tt, ne), lambda i: (i, 0)),
                  pl.BlockSpec((tt, d), lambda i: (i, 0)),
                  pl.BlockSpec((None, 1, d), lambda i: (i // tiles_per_b, 0, 0)),
                  pl.BlockSpec((1, d), lambda i: (0, 0)),
                  pl.BlockSpec(memory_space=pl.ANY)],
        out_specs=pl.BlockSpec((tt, d), lambda i: (i, 0)),
        out_shape=jax.ShapeDtypeStruct((ntok, d), F32),
        scratch_shapes=([pltpu.VMEM((ne, 1, 2 * d), F32) for _ in range(N_SLOTS)]
                        + [pltpu.VMEM((ne, 2 * d), F32) for _ in range(GROUP)]
                        + [pltpu.SemaphoreType.DMA((N_GSLOTS,))]),
        compiler_params=_cparams(("arbitrary",)),
        name="peer_expert",
    )(idx_win, h2, gate_tok, x1, gt2, final_g.reshape(1, d), table)


ROW_TILE_TOKENS = 256
ROUTE_TILE_TOKENS = 1024
GATHER_STEP_TOKENS = 128


def kernel(x, c, ctx, c_ctx, ada_w, ada_b, norm1_g, w_in, gdn_conv_w, gdn_a_log, gdn_dt_bias,
           gdn_norm_w, rwkv_mu, rwkv_w0, rwkv_w2, rwkv_a0, rwkv_a2, rwkv_g2, rwkv_k_k, rwkv_k_a,
           rwkv_r_k, rwkv_gn_w, rwkv_gn_b, w_out, norm2_g, peer_w_query, peer_sub_keys,
           peer_down, peer_up, final_norm_g):
    bsz, seq, d = x.shape
    n_ctx = ctx.shape[1]
    assert ada_w.shape[0] == 1, "single-layer block"
    assert n_ctx % ROW_TILE_TOKENS == 0 and seq % ROUTE_TILE_TOKENS == 0 and seq % GRID_W == 0
    nh = GDN_HEADS
    n_qkvz = 4 * nh * GDN_DK
    gdn_cols = n_qkvz + 4 * nh

    cond = jnp.concatenate([c, c_ctx[None, :], jnp.zeros((16 - bsz - 1, d), F32)], axis=0)
    mods = ada_mod(cond, ada_w[0], ada_b[0])
    m_lat = mods[:bsz].reshape(bsz, N_MOD, 1, d)
    m_ctx = jnp.broadcast_to(mods[bsz].reshape(1, N_MOD, 1, d), (bsz, N_MOD, 1, d))
    shift1 = jnp.stack([m_ctx[:, 0], m_lat[:, 0]], axis=1)
    scale1 = jnp.stack([m_ctx[:, 1], m_lat[:, 1]], axis=1)
    gt1, sh2, sc2, gt2 = m_lat[:, 2], m_lat[:, 3], m_lat[:, 4], m_lat[:, 5]

    wl = w_in[0]
    w_gdn = jnp.concatenate([wl[:, :gdn_cols], jnp.zeros((d, LANES - 4 * nh), F32)], axis=1)
    w_rwkv = wl[:, gdn_cols:]
    p_gdn, p_rwkv = in_proj(ctx, x, norm1_g[0], shift1, scale1, w_gdn.astype(BF16),
                            w_rwkv.astype(BF16), ROW_TILE_TOKENS)

    qkv = gdn_prep(p_gdn, gdn_conv_w[0], n_ctx)
    prm = jnp.zeros((8, LANES), F32)
    prm = prm.at[0, :2 * nh].set(gdn_a_log[0].reshape(-1))
    prm = prm.at[1, :2 * nh].set(gdn_dt_bias[0].reshape(-1))
    o_f, o_b = gdn_chunk(qkv, p_gdn, prm, n_ctx)

    xs = rwkv_shift(p_rwkv, rwkv_mu[0], n_ctx)
    vecs = jnp.zeros((8, RWKV_W), F32)
    for i, vec in enumerate((rwkv_k_k, rwkv_k_a, rwkv_r_k, rwkv_gn_w, rwkv_gn_b)):
        vecs = vecs.at[i].set(vec[0])
    y_f, y_b = rwkv_chunk(xs, vecs, rwkv_w0[0], rwkv_w2[0], rwkv_a0[0], rwkv_a2[0], n_ctx)

    x1 = mix_out(x, gt1, o_f, o_b, p_gdn, y_f, y_b, xs, gdn_norm_w[0].reshape(1, GDN_DK), vecs,
                 rwkv_a0[0], rwkv_a2[0], rwkv_g2[0], w_out[0].astype(BF16), n_ctx, ROW_TILE_TOKENS)

    h2, idx_t, gate_t = peer_route(x1, norm2_g[0], sh2, sc2, peer_w_query[0].astype(BF16),
                                   peer_sub_keys[0], ROUTE_TILE_TOKENS)
    table = jnp.concatenate([peer_down[0], peer_up[0]], axis=1)[:, None, :]
    out = peer_expert(idx_t.T, h2, gate_t.T, x1.reshape(bsz * seq, d), gt2,
                      final_norm_g, table, seq, GATHER_STEP_TOKENS)
    return out.reshape(bsz, seq, d)
```

```python
import functools
import math

import jax
import jax.numpy as jnp
from jax import lax
from jax.experimental import pallas as pl
from jax.experimental.pallas import tpu as pltpu

F32 = jnp.float32
BF16 = jnp.bfloat16

GRID_W = 64
GDN_HEADS = 4
GDN_DK = 128
GDN_CONV = 5
CHUNK = 64
RWKV_HEADS = 8
RWKV_HD = 64
RWKV_W = RWKV_HEADS * RWKV_HD
LORA = 64
PEER_HEADS = 8
PEER_NKEYS = 128
PEER_TOPK = 16
N_MOD = 6

NORM_EPS = 1e-6
L2_EPS = 1e-6
RWKV_GN_EPS = 64e-5

LANES = 128
VMEM_LIMIT = 56 * 1024 * 1024
NEG_INF = float("-inf")


def _cparams(sem):
    return pltpu.CompilerParams(dimension_semantics=sem, vmem_limit_bytes=VMEM_LIMIT)


def _dot(a, b):
    return jnp.dot(a.astype(BF16), b.astype(BF16), preferred_element_type=F32)


def _dot_nt(a, b):
    return lax.dot_general(a.astype(BF16), b.astype(BF16), (((1,), (1,)), ((), ())),
                           preferred_element_type=F32)


def _dot_tn(a, b):
    return lax.dot_general(a.astype(BF16), b.astype(BF16), (((0,), (0,)), ((), ())),
                           preferred_element_type=F32)


def _dot_hi(a, b):
    return jnp.dot(a, b, preferred_element_type=F32, precision=lax.Precision.HIGHEST)


def _split3(x):
    hi = x.astype(BF16)
    rest = x - hi.astype(F32)
    mid = rest.astype(BF16)
    lo = (rest - mid.astype(F32)).astype(BF16)
    return hi, mid, lo


def _dot_x01(a, m01):
    m = m01.astype(BF16)
    return sum(jnp.dot(p, m, preferred_element_type=F32) for p in _split3(a))


def _dot_01x(m01, b):
    m = m01.astype(BF16)
    return sum(jnp.dot(m, p, preferred_element_type=F32) for p in _split3(b))


def _sigmoid(x):
    return 1.0 / (1.0 + jnp.exp(-x))


def _silu(x):
    return x * _sigmoid(x)


def _softplus(x):
    return jnp.maximum(x, 0.0) + jnp.log(1.0 + jnp.exp(-jnp.abs(x)))


def _unit_lower_inverses(xs):
    c = xs[0].shape[0]
    eye = (lax.broadcasted_iota(jnp.int32, (c, c), 0) ==
           lax.broadcasted_iota(jnp.int32, (c, c), 1)).astype(F32)
    prods = [eye + x for x in xs]
    pows = [_dot(x, x) for x in xs]
    for _ in range(int(math.log2(c)) - 2):
        both = [_dot(jnp.concatenate([p, x], axis=0), x) for p, x in zip(prods, pows)]
        prods = [p + b[:c] for p, b in zip(prods, both)]
        pows = [b[c:] for b in both]
    return [p + _dot(p, x) for p, x in zip(prods, pows)]


def _order_masks(c, rev):
    r = lax.broadcasted_iota(jnp.int32, (c, c), 0)
    s = lax.broadcasted_iota(jnp.int32, (c, c), 1)
    if rev:
        return s >= r, s > r
    return s <= r, s < r


def _ada_kernel(c_ref, w_ref, b_ref, o_ref):
    o_ref[...] = _dot_hi(_silu(c_ref[...]), w_ref[...]) + b_ref[...]


def ada_mod(cond, w, b):
    n, d = cond.shape
    cols = w.shape[1]
    tn = 1024
    return pl.pallas_call(
        _ada_kernel,
        grid=(cols // tn,),
        in_specs=[pl.BlockSpec((n, d), lambda j: (0, 0)),
                  pl.BlockSpec((d, tn), lambda j: (0, j)),
                  pl.BlockSpec((1, tn), lambda j: (0, j))],
        out_specs=pl.BlockSpec((n, tn), lambda j: (0, j)),
        out_shape=jax.ShapeDtypeStruct((n, cols), F32),
        compiler_params=_cparams(("arbitrary",)),
        name="ada_mod",
    )(cond, w, b.reshape(1, cols))


def _in_proj_kernel(ctx_ref, x_ref, g_ref, sh_ref, sc_ref, wg_ref, wr_ref, pg_ref, pr_ref,
                    *, nctx_tiles):
    x = jnp.where(pl.program_id(1) < nctx_tiles, ctx_ref[...], x_ref[...])
    y = x * lax.rsqrt(jnp.mean(x * x, axis=-1, keepdims=True) + NORM_EPS) * g_ref[...]
    h = (y * (1.0 + sc_ref[...]) + sh_ref[...]).astype(BF16)
    pg_ref[...] = jnp.dot(h, wg_ref[...], preferred_element_type=F32)
    pr_ref[...] = jnp.dot(h, wr_ref[...], preferred_element_type=F32)


def in_proj(ctx, x, norm_g, shift, scale, w_gdn, w_rwkv, tm):
    bsz, n_ctx, d = ctx.shape
    t = n_ctx + x.shape[1]
    cg, cr = w_gdn.shape[1], w_rwkv.shape[1]
    nctx_tiles = n_ctx // tm

    def mod_map(b, i):
        return (b, jnp.where(i < nctx_tiles, 0, 1), 0, 0)

    return pl.pallas_call(
        functools.partial(_in_proj_kernel, nctx_tiles=nctx_tiles),
        grid=(bsz, t // tm),
        in_specs=[pl.BlockSpec((None, tm, d), lambda b, i: (b, jnp.minimum(i, nctx_tiles - 1), 0)),
                  pl.BlockSpec((None, tm, d), lambda b, i: (b, jnp.maximum(i - nctx_tiles, 0), 0)),
                  pl.BlockSpec((1, d), lambda b, i: (0, 0)),
                  pl.BlockSpec((None, None, 1, d), mod_map),
                  pl.BlockSpec((None, None, 1, d), mod_map),
                  pl.BlockSpec((d, cg), lambda b, i: (0, 0)),
                  pl.BlockSpec((d, cr), lambda b, i: (0, 0))],
        out_specs=[pl.BlockSpec((None, tm, cg), lambda b, i: (b, i, 0)),
                   pl.BlockSpec((None, tm, cr), lambda b, i: (b, i, 0))],
        out_shape=[jax.ShapeDtypeStruct((bsz, t, cg), F32),
                   jax.ShapeDtypeStruct((bsz, t, cr), F32)],
        compiler_params=_cparams(("parallel", "parallel")),
        name="in_proj",
    )(ctx, x, norm_g.reshape(1, d), shift, scale, w_gdn, w_rwkv)


def _gdn_prep_kernel(p_ref, w_ref, o_ref, *, n_ctx):
    j = pl.program_id(1)
    x = p_ref[...]
    t_len = x.shape[0]
    w = w_ref[...]
    t = lax.broadcasted_iota(jnp.int32, x.shape, 0)
    is_ctx = t < n_ctx
    lo = jnp.where(is_ctx, 0, n_ctx)
    hi = jnp.where(is_ctx, n_ctx, t_len)
    pad = (GDN_CONV - 1) // 2
    acc = x * w[pad:pad + 1, :]
    for s in range(-pad, pad + 1):
        if s == 0:
            continue
        xs = pltpu.roll(x, (-s) % t_len, 0)
        valid = (t + s >= lo) & (t + s < hi)
        acc = acc + jnp.where(valid, xs, 0.0) * w[s + pad:s + pad + 1, :]
    y = _silu(acc)
    inv = lax.rsqrt(jnp.sum(y * y, axis=-1, keepdims=True) + L2_EPS)
    fac = jnp.where(j < GDN_HEADS, inv * (GDN_DK ** -0.5), jnp.where(j < 2 * GDN_HEADS, inv, 1.0))
    o_ref[...] = y * fac


def gdn_prep(p_gdn, conv_w, n_ctx):
    bsz, t, _ = p_gdn.shape
    ncol = conv_w.shape[1]
    return pl.pallas_call(
        functools.partial(_gdn_prep_kernel, n_ctx=n_ctx),
        grid=(bsz, ncol // LANES),
        in_specs=[pl.BlockSpec((None, t, LANES), lambda b, j: (b, 0, j)),
                  pl.BlockSpec((GDN_CONV, LANES), lambda b, j: (0, j))],
        out_specs=pl.BlockSpec((None, t, LANES), lambda b, j: (b, 0, j)),
        out_shape=jax.ShapeDtypeStruct((bsz, t, ncol), F32),
        compiler_params=_cparams(("parallel", "parallel")),
        name="gdn_prep",
    )(p_gdn, conv_w)


def _gdn_chunk_kernel(qf_ref, qb_ref, gf_ref, gb_ref, prm_ref, of_ref, ob_ref, s_ref):
    @pl.when(pl.program_id(1) == 0)
    def _():
        s_ref[...] = jnp.zeros_like(s_ref)

    nh, dk = GDN_HEADS, GDN_DK
    nb = qf_ref.shape[0]
    prm = prm_ref[...]
    a_log, dt_bias = prm[0:1, :], prm[1:2, :]
    q, k, v, beta, g_col, g_last, gamma, strict = [], [], [], [], [], [], [], []
    for bi, d in [(bi, d) for bi in range(nb) for d in range(2)]:
        qkv = (qf_ref, qb_ref)[d][bi]
        gates = (gf_ref, gb_ref)[d][bi]
        c = qkv.shape[0]
        incl, strict_d = _order_masks(c, d == 1)
        g_all = -jnp.exp(a_log) * _softplus(gates + dt_bias)
        beta_all = _sigmoid(gates)
        big_g = _dot_01x(incl.astype(F32), g_all)
        big_g_t = big_g.T
        last = 0 if d == 1 else c - 1
        for h in range(nh):
            col = d * nh + h
            q.append(qkv[:, h * dk:(h + 1) * dk])
            k.append(qkv[:, (nh + h) * dk:(nh + h + 1) * dk])
            v.append(qkv[:, (2 * nh + h) * dk:(2 * nh + h + 1) * dk])
            beta.append(beta_all[:, 2 * nh + col:2 * nh + col + 1])
            g_col.append(big_g[:, col:col + 1])
            g_last.append(big_g[last:last + 1, col:col + 1])
            g_row = big_g_t[col:col + 1, :]
            gamma.append(jnp.where(incl, jnp.exp(jnp.where(incl, g_col[-1] - g_row, 0.0)), 0.0))
            strict.append(strict_d)
    n = len(q)
    c = q[0].shape[0]
    e_g = [jnp.exp(g) for g in g_col]
    kq = [_dot_nt(jnp.concatenate([k[i], q[i]], axis=0), k[i]) for i in range(n)]
    a_mat = [jnp.where(strict[i], beta[i] * kq[i][:c] * gamma[i], 0.0) for i in range(n)]
    qk = [kq[i][c:] * gamma[i] for i in range(n)]
    t_inv = _unit_lower_inverses([-a for a in a_mat])
    uw = [_dot(t_inv[i], jnp.concatenate([beta[i] * v[i], (beta[i] * e_g[i]) * k[i]], axis=1))
          for i in range(n)]
    s = [s_ref[i // nh, i % nh] for i in range(n)]
    ws = [_dot(jnp.concatenate([uw[i][:, dk:], q[i] * e_g[i]], axis=0), s[i]) for i in range(n)]
    v_new = [uw[i][:, :dk] - ws[i][:c] for i in range(n)]
    o = [ws[i][c:] + _dot(qk[i], v_new[i]) for i in range(n)]
    s_new = [s[i] * jnp.exp(g_last[i]) + _dot_tn(k[i] * jnp.exp(g_last[i] - g_col[i]), v_new[i])
             for i in range(n)]
    for i in range(n):
        s_ref[i // nh, i % nh] = s_new[i]
    for bi in range(nb):
        base = 2 * nh * bi
        of_ref[bi] = jnp.concatenate(o[base:base + nh], axis=-1)
        ob_ref[bi] = jnp.concatenate(o[base + nh:base + 2 * nh], axis=-1)


GDN_BATCH_PER_STEP = 4
RWKV_BATCH_PER_STEP = 2


def _rev_chunk_map(n_ctx_chunks, n_chunks):
    def cb(s):
        return jnp.where(s < n_ctx_chunks, n_ctx_chunks - 1 - s, n_chunks + n_ctx_chunks - 1 - s)
    return cb


def gdn_chunk(qkv, p_gdn, prm, n_ctx):
    bsz, t, ncol = qkv.shape
    nv = GDN_HEADS * GDN_DK
    n_chunks = t // CHUNK
    cb = _rev_chunk_map(n_ctx // CHUNK, n_chunks)
    gate_tile = p_gdn.shape[2] // LANES - 1
    nb = math.gcd(bsz, GDN_BATCH_PER_STEP)
    return pl.pallas_call(
        _gdn_chunk_kernel,
        grid=(bsz // nb, n_chunks),
        in_specs=[pl.BlockSpec((nb, CHUNK, ncol), lambda b, s: (b, s, 0)),
                  pl.BlockSpec((nb, CHUNK, ncol), lambda b, s: (b, cb(s), 0)),
                  pl.BlockSpec((nb, CHUNK, LANES), lambda b, s: (b, s, gate_tile)),
                  pl.BlockSpec((nb, CHUNK, LANES), lambda b, s: (b, cb(s), gate_tile)),
                  pl.BlockSpec((8, LANES), lambda b, s: (0, 0))],
        out_specs=[pl.BlockSpec((nb, CHUNK, nv), lambda b, s: (b, s, 0)),
                   pl.BlockSpec((nb, CHUNK, nv), lambda b, s: (b, cb(s), 0))],
        out_shape=[jax.ShapeDtypeStruct((bsz, t, nv), F32)] * 2,
        scratch_shapes=[pltpu.VMEM((2 * nb, GDN_HEADS, GDN_DK, GDN_DK), F32)],
        compiler_params=_cparams(("parallel", "arbitrary")),
        name="gdn_chunk",
    )(qkv, qkv, p_gdn, p_gdn, prm)


def _rwkv_shift_kernel(p_ref, mu_ref, o_ref, *, n_ctx, n_cols):
    j = pl.program_id(1)
    x = p_ref[...]
    t_len = x.shape[0]
    t = lax.broadcasted_iota(jnp.int32, x.shape, 0)
    ch = lax.broadcasted_iota(jnp.int32, x.shape, 1) + j * LANES
    is_ctx = t < n_ctx
    col = (t - n_ctx) % GRID_W
    prev1 = pltpu.roll(x, 1, 0)
    next1 = pltpu.roll(x, t_len - 1, 0)
    up = pltpu.roll(x, GRID_W, 0)
    down = pltpu.roll(x, t_len - GRID_W, 0)
    quarter = n_cols // 4
    half = n_cols // 2
    ctx_sh = jnp.where(ch < half,
                       jnp.where(t >= 1, prev1, 0.0),
                       jnp.where(t < n_ctx - 1, next1, 0.0))
    lat_sh = jnp.where(ch < quarter, jnp.where(col >= 1, prev1, 0.0),
                       jnp.where(ch < 2 * quarter, jnp.where(col < GRID_W - 1, next1, 0.0),
                                 jnp.where(ch < 3 * quarter,
                                           jnp.where(t >= n_ctx + GRID_W, up, 0.0),
                                           jnp.where(t < t_len - GRID_W, down, 0.0))))
    sh = jnp.where(is_ctx, ctx_sh, lat_sh)
    o_ref[...] = x + mu_ref[...] * (sh - x)


def rwkv_shift(p_rwkv, mu, n_ctx):
    bsz, t, ncol = p_rwkv.shape
    return pl.pallas_call(
        functools.partial(_rwkv_shift_kernel, n_ctx=n_ctx, n_cols=ncol),
        grid=(bsz, ncol // LANES),
        in_specs=[pl.BlockSpec((None, t, LANES), lambda b, j: (b, 0, j)),
                  pl.BlockSpec((1, LANES), lambda b, j: (0, j))],
        out_specs=pl.BlockSpec((None, t, LANES), lambda b, j: (b, 0, j)),
        out_shape=jax.ShapeDtypeStruct((bsz, t, ncol), F32),
        compiler_params=_cparams(("parallel", "parallel")),
        name="rwkv_shift",
    )(p_rwkv, mu.reshape(1, ncol))


def _rwkv_gates(xs, d, w0_ref, w2_ref, a0_ref, a2_ref):
    w_ = RWKV_W
    xw = xs[:, 3 * w_ + d * LORA:3 * w_ + (d + 1) * LORA]
    xa = xs[:, 3 * w_ + 2 * LORA + d * LORA:3 * w_ + 2 * LORA + (d + 1) * LORA]
    w_pre = w0_ref[d:d + 1, :] + _dot(jnp.tanh(xw), w2_ref[d])
    log_w = -math.exp(-0.5) * _sigmoid(w_pre)
    a = _sigmoid(a0_ref[d:d + 1, :] + _dot(xa, a2_ref[d]))
    return log_w, a


def _rwkv_chunk_kernel(xf_ref, xb_ref, vec_ref, w0_ref, w2_ref, a0_ref, a2_ref, hsum_ref,
                       yf_ref, yb_ref, p_ref):
    @pl.when(pl.program_id(1) == 0)
    def _():
        p_ref[...] = jnp.zeros_like(p_ref)

    w_, n, nh = RWKV_W, RWKV_HD, RWKV_HEADS
    nb = xf_ref.shape[0]
    k_k, k_a = vec_ref[0:1, :], vec_ref[1:2, :]
    lhs, rhs_t, tail, vals, w_last, p0, strict, incl = [], [], [], [], [], [], [], []
    for bi, d in [(bi, d) for bi in range(nb) for d in range(2)]:
        xs = (xf_ref, xb_ref)[d][bi]
        c = xs.shape[0]
        incl_d, strict_d = _order_masks(c, d == 1)
        r = xs[:, 0:w_]
        k = xs[:, w_:2 * w_]
        v = xs[:, 2 * w_:3 * w_]
        log_w, a = _rwkv_gates(xs, d, w0_ref, w2_ref, a0_ref, a2_ref)
        kk = k * k_k
        kk = kk * lax.rsqrt(_dot_x01(kk * kk, hsum_ref[...]) + L2_EPS)
        k_dir = k * (1.0 + (a - 1.0) * k_a)
        alpha = -(kk * a)
        big_g = _dot_01x(incl_d.astype(F32), log_w)
        last = 0 if d == 1 else c - 1
        g_last = big_g[last:last + 1, :]
        e_neg = jnp.exp(-big_g)
        e_tail = jnp.exp(g_last - big_g)
        r_t = r * jnp.exp(big_g)
        b_t = kk * jnp.exp(big_g - log_w)
        a_t = alpha * e_neg
        k_t = k_dir * e_neg
        a_h = alpha * e_tail
        k_h = k_dir * e_tail
        w_l = jnp.exp(g_last)
        for h in range(nh):
            sl = slice(h * n, (h + 1) * n)
            lhs.append(jnp.concatenate([b_t[:, sl], r_t[:, sl]], axis=0))
            rhs_t.append(jnp.concatenate([a_t[:, sl], k_t[:, sl]], axis=0))
            tail.append(jnp.concatenate([a_h[:, sl], k_h[:, sl]], axis=0))
            vals.append(v[:, sl])
            w_last.append(w_l[:, sl])
            p0.append(p_ref[2 * bi + d, h])
            strict.append(strict_d)
            incl.append(incl_d)
    m = len(lhs)
    c = vals[0].shape[0]
    quad = [_dot_nt(lhs[i], rhs_t[i]) for i in range(m)]
    a_ba = [jnp.where(strict[i], quad[i][:c, :c], 0.0) for i in range(m)]
    a_k = [jnp.concatenate([jnp.where(strict[i], quad[i][:c, c:], 0.0),
                            jnp.where(incl[i], quad[i][c:, c:], 0.0)], axis=0) for i in range(m)]
    a_ra = [jnp.where(incl[i], quad[i][c:, :c], 0.0) for i in range(m)]
    from_state = [_dot_nt(lhs[i], p0[i]) for i in range(m)]
    from_vals = [_dot(a_k[i], vals[i]) for i in range(m)]
    t_inv = _unit_lower_inverses(a_ba)
    u = [_dot(t_inv[i], from_state[i][:c] + from_vals[i][:c]) for i in range(m)]
    y = [from_state[i][c:] + _dot(a_ra[i], u[i]) + from_vals[i][c:] for i in range(m)]
    p_new = [p0[i] * w_last[i] + _dot_tn(jnp.concatenate([u[i], vals[i]], axis=0), tail[i])
             for i in range(m)]
    for i in range(m):
        p_ref[i // nh, i % nh] = p_new[i]
    for bi in range(nb):
        base = 2 * nh * bi
        yf_ref[bi] = jnp.concatenate(y[base:base + nh], axis=-1)
        yb_ref[bi] = jnp.concatenate(y[base + nh:base + 2 * nh], axis=-1)


def _head_sum_matrix(width, group):
    i = jnp.arange(width)
    return (i[:, None] // group == i[None, :] // group).astype(F32)


def rwkv_chunk(xs, vecs, w0, w2, a0, a2, n_ctx):
    bsz, t, ncol = xs.shape
    n_chunks = t // CHUNK
    cb = _rev_chunk_map(n_ctx // CHUNK, n_chunks)
    hsum = _head_sum_matrix(RWKV_W, RWKV_HD)
    full = lambda *shape: pl.BlockSpec(shape, lambda b, s: (0,) * len(shape))
    nb = math.gcd(bsz, RWKV_BATCH_PER_STEP)
    return pl.pallas_call(
        _rwkv_chunk_kernel,
        grid=(bsz // nb, n_chunks),
        in_specs=[pl.BlockSpec((nb, CHUNK, ncol), lambda b, s: (b, s, 0)),
                  pl.BlockSpec((nb, CHUNK, ncol), lambda b, s: (b, cb(s), 0)),
                  full(8, RWKV_W), full(2, RWKV_W), full(2, LORA, RWKV_W),
                  full(2, RWKV_W), full(2, LORA, RWKV_W), full(RWKV_W, RWKV_W)],
        out_specs=[pl.BlockSpec((nb, CHUNK, RWKV_W), lambda b, s: (b, s, 0)),
                   pl.BlockSpec((nb, CHUNK, RWKV_W), lambda b, s: (b, cb(s), 0))],
        out_shape=[jax.ShapeDtypeStruct((bsz, t, RWKV_W), F32)] * 2,
        scratch_shapes=[pltpu.VMEM((2 * nb, RWKV_HEADS, RWKV_HD, RWKV_HD), F32)],
        compiler_params=_cparams(("parallel", "arbitrary")),
        name="rwkv_chunk",
    )(xs, xs, vecs, w0, w2, a0, a2, hsum)


def _mix_out_kernel(x_ref, gt_ref, of_ref, ob_ref, z_ref, yf_ref, yb_ref, xs_ref,
                    gnw_ref, vec_ref, a0_ref, a2_ref, g2_ref, hsum_ref, wo_ref, o_ref):
    nh, dv = GDN_HEADS, GDN_DK
    w_ = RWKV_W
    o = of_ref[...] + ob_ref[...]
    z = z_ref[...]
    parts = []
    for h in range(nh):
        oh = o[:, h * dv:(h + 1) * dv]
        oh = oh * lax.rsqrt(jnp.mean(oh * oh, axis=-1, keepdims=True) + NORM_EPS)
        parts.append(oh * gnw_ref[...] * _silu(z[:, h * dv:(h + 1) * dv]))
    xs = xs_ref[...]
    r = xs[:, 0:w_]
    k = xs[:, w_:2 * w_]
    v = xs[:, 2 * w_:3 * w_]
    xg = xs[:, 3 * w_ + 4 * LORA:]
    k_a, r_k = vec_ref[1:2, :], vec_ref[2:3, :]
    gn_w, gn_b = vec_ref[3:4, :], vec_ref[4:5, :]
    hsum = hsum_ref[...]
    y = yf_ref[...] + yb_ref[...]
    mean = _dot_x01(y, hsum) * (1.0 / RWKV_HD)
    yc = y - mean
    var = _dot_x01(yc * yc, hsum) * (1.0 / RWKV_HD)
    yn = yc * lax.rsqrt(var + RWKV_GN_EPS) * gn_w + gn_b
    rk = jnp.zeros_like(r)
    for d in range(2):
        xa = xs[:, 3 * w_ + 2 * LORA + d * LORA:3 * w_ + 2 * LORA + (d + 1) * LORA]
        a = _sigmoid(a0_ref[d:d + 1, :] + _dot(xa, a2_ref[d]))
        rk = rk + r * (k * (1.0 + (a - 1.0) * k_a)) * r_k
    bonus = _dot_x01(rk, hsum) * v
    gate = _dot(_sigmoid(xg), g2_ref[...])
    parts.append((yn + bonus) * gate)
    mixed = jnp.concatenate(parts, axis=-1).astype(BF16)
    o_ref[...] = x_ref[...] + gt_ref[...] * jnp.dot(mixed, wo_ref[...], preferred_element_type=F32)


def mix_out(x, gt1, o_f, o_b, p_gdn, y_f, y_b, xs, gnw, vecs, a0, a2, g2, w_out, n_ctx, tm):
    bsz, seq, d = x.shape
    off = n_ctx // tm
    nv = GDN_HEADS * GDN_DK
    ncol = xs.shape[2]
    hsum = _head_sum_matrix(RWKV_W, RWKV_HD)
    lat = lambda width, cblk=0: pl.BlockSpec((None, tm, width), lambda b, i: (b, i + off, cblk))
    full = lambda *shape: pl.BlockSpec(shape, lambda b, i: (0,) * len(shape))
    return pl.pallas_call(
        _mix_out_kernel,
        grid=(bsz, seq // tm),
        in_specs=[pl.BlockSpec((None, tm, d), lambda b, i: (b, i, 0)),
                  pl.BlockSpec((None, 1, d), lambda b, i: (b, 0, 0)),
                  lat(nv), lat(nv), lat(nv, 3), lat(RWKV_W), lat(RWKV_W), lat(ncol),
                  full(1, GDN_DK), full(8, RWKV_W), full(2, RWKV_W), full(2, LORA, RWKV_W),
                  full(2 * LORA, RWKV_W), full(RWKV_W, RWKV_W), full(nv + RWKV_W, d)],
        out_specs=pl.BlockSpec((None, tm, d), lambda b, i: (b, i, 0)),
        out_shape=jax.ShapeDtypeStruct((bsz, seq, d), F32),
        compiler_params=_cparams(("parallel", "parallel")),
        name="mix_out",
    )(x, gt1, o_f, o_b, p_gdn, y_f, y_b, xs, gnw, vecs, a0, a2, g2, hsum, w_out)


def _extract_top(s, riota, n_rows, vals_ref, idx_ref, k):
    for i in range(k):
        m = jnp.max(s, axis=0, keepdims=True)
        am = jnp.min(jnp.where(s == m, riota, float(n_rows)), axis=0, keepdims=True)
        vals_ref[i:i + 1, :] = m
        idx_ref[i:i + 1, :] = am
        s = jnp.where(riota == am, NEG_INF, s)


def _candidate_blocks(k):
    split = math.isqrt(k)
    blocks = [(False, i, 0, k // (i + 1)) for i in range(split)]
    j = 0
    while k // (j + 1) > split:
        blocks.append((True, j, split, k // (j + 1)))
        j += 1
    return blocks


def _candidate_rows(k):
    return sum(-(-(hi - lo) // 8) * 8 for _, _, lo, hi in _candidate_blocks(k))


def _top_list_rows(k):
    need = max(lo + -(-(hi - lo) // 8) * 8 for _, _, lo, hi in _candidate_blocks(k))
    return -(-need // 8) * 8


def _peer_route_kernel(x_ref, g_ref, sh_ref, sc_ref, wq_ref, keys_ref,
                       h2_ref, idx_ref, gate_ref,
                       q_scr, tv_scr, ti_scr, cs_scr, ci_scr, cf_scr, bv_scr, bi_scr):
    h = pl.program_id(1)
    dq = 2 * LANES
    kk = PEER_TOPK

    @pl.when(h == 0)
    def _():
        x = x_ref[...]
        y = x * lax.rsqrt(jnp.mean(x * x, axis=-1, keepdims=True) + NORM_EPS) * g_ref[...]
        h2 = y * (1.0 + sc_ref[...]) + sh_ref[...]
        h2_ref[...] = h2
        hb = h2.astype(BF16)
        for hh in range(PEER_HEADS):
            q_scr[hh] = jnp.dot(hb, wq_ref[:, hh * dq:(hh + 1) * dq], preferred_element_type=F32)

    q = q_scr[h]
    tm = q.shape[0]
    riota = lax.broadcasted_iota(jnp.int32, (PEER_NKEYS, tm), 0).astype(F32)
    for p in range(2):
        s_t = _dot_nt(keys_ref[p], q[:, p * LANES:(p + 1) * LANES])
        _extract_top(s_t, riota, PEER_NKEYS, tv_scr.at[p], ti_scr.at[p], kk)
    tv_scr[:, kk:, :] = jnp.zeros_like(tv_scr[:, kk:, :])
    ti_scr[:, kk:, :] = jnp.zeros_like(ti_scr[:, kk:, :])
    row = 0
    for fixed_j, fixed, lo, hi in _candidate_blocks(kk):
        n_pad = -(-(hi - lo) // 8) * 8
        r = lax.broadcasted_iota(jnp.int32, (n_pad, tm), 0).astype(F32)
        run, one = (0, 1) if fixed_j else (1, 0)
        vals = tv_scr[run, lo:lo + n_pad, :] + tv_scr[one, fixed:fixed + 1, :]
        run_i, one_i = ti_scr[run, lo:lo + n_pad, :], ti_scr[one, fixed:fixed + 1, :]
        if fixed_j:
            experts = run_i * float(PEER_NKEYS) + one_i
            flat = (r + float(lo)) * float(kk) + float(fixed)
        else:
            experts = one_i * float(PEER_NKEYS) + run_i
            flat = float(fixed * kk + lo) + r
        cs_scr[row:row + n_pad, :] = jnp.where(r < float(hi - lo), vals, NEG_INF)
        ci_scr[row:row + n_pad, :] = experts
        cf_scr[row:row + n_pad, :] = flat
        row += n_pad
    cand = cs_scr[...]
    cand_i = ci_scr[...]
    ciota = cf_scr[...]
    for i in range(kk):
        m = jnp.max(cand, axis=0, keepdims=True)
        pos = jnp.min(jnp.where(cand == m, ciota, float(kk * kk)), axis=0, keepdims=True)
        hit = ciota == pos
        bv_scr[i:i + 1, :] = m
        bi_scr[i:i + 1, :] = jnp.sum(jnp.where(hit, cand_i, 0.0), axis=0, keepdims=True)
        cand = jnp.where(hit, NEG_INF, cand)
    best = bv_scr[...]
    e = jnp.exp(best - best[0:1, :])
    gate_ref[...] = e / jnp.sum(e, axis=0, keepdims=True)
    idx_ref[...] = bi_scr[...].astype(jnp.int32)


def peer_route(x1, norm_g, sh2, sc2, wq, sub_keys, tm):
    bsz, seq, d = x1.shape
    ntok = bsz * seq
    tiles_per_b = seq // tm
    kk = PEER_TOPK
    nq = wq.shape[1]
    return pl.pallas_call(
        _peer_route_kernel,
        grid=(ntok // tm, PEER_HEADS),
        in_specs=[pl.BlockSpec((tm, d), lambda i, h: (i, 0)),
                  pl.BlockSpec((1, d), lambda i, h: (0, 0)),
                  pl.BlockSpec((None, 1, d), lambda i, h: (i // tiles_per_b, 0, 0)),
                  pl.BlockSpec((None, 1, d), lambda i, h: (i // tiles_per_b, 0, 0)),
                  pl.BlockSpec((d, nq), lambda i, h: (0, 0)),
                  pl.BlockSpec((None, 2, PEER_NKEYS, LANES), lambda i, h: (h, 0, 0, 0))],
        out_specs=[pl.BlockSpec((tm, d), lambda i, h: (i, 0)),
                   pl.BlockSpec((kk, tm), lambda i, h: (h, i)),
                   pl.BlockSpec((kk, tm), lambda i, h: (h, i))],
        out_shape=[jax.ShapeDtypeStruct((ntok, d), F32),
                   jax.ShapeDtypeStruct((PEER_HEADS * kk, ntok), jnp.int32),
                   jax.ShapeDtypeStruct((PEER_HEADS * kk, ntok), F32)],
        scratch_shapes=[pltpu.VMEM((PEER_HEADS, tm, 2 * LANES), F32),
                        pltpu.VMEM((2, _top_list_rows(kk), tm), F32),
                        pltpu.VMEM((2, _top_list_rows(kk), tm), F32),
                        pltpu.VMEM((_candidate_rows(kk), tm), F32),
                        pltpu.VMEM((_candidate_rows(kk), tm), F32),
                        pltpu.VMEM((_candidate_rows(kk), tm), F32),
                        pltpu.VMEM((kk, tm), F32), pltpu.VMEM((kk, tm), F32)],
        compiler_params=_cparams(("parallel", "arbitrary")),
        name="peer_route",
    )(x1.reshape(ntok, d), norm_g.reshape(1, d), sh2, sc2, wq, sub_keys)


GROUP = 8
N_GSLOTS = 4
AHEAD = 2
N_SLOTS = GROUP * N_GSLOTS
ROW_TILE = 8
DMA_QUEUES = 2


def _peer_expert_kernel(idx_ref, h2_ref, gate_ref, x1_ref, gt_ref, fg_ref, tbl_ref,
                        o_ref, *scratch, tt, d):
    bufs, tiles, sem = scratch[:N_SLOTS], scratch[N_SLOTS:N_SLOTS + GROUP], scratch[N_SLOTS + GROUP]
    ne = PEER_HEADS * PEER_TOPK
    step = pl.program_id(0)
    last_step = pl.num_programs(0) - 1

    def issue_token(base, gslot, j):
        slot = gslot * GROUP + j
        for k in range(ne):
            e = idx_ref[base + k]
            pltpu.make_async_copy(tbl_ref.at[e], bufs[slot].at[k],
                                  sem.at[gslot]).start(priority=k % DMA_QUEUES)

    def wait_group(gslot):
        for j in range(GROUP):
            pltpu.make_async_copy(tbl_ref.at[pl.ds(0, ne)], bufs[gslot * GROUP + j],
                                  sem.at[gslot]).wait()

    def evaluate(t, slot):
        tile = tiles[slot % GROUP]
        tile[...] = bufs[slot][...].reshape(ne, 2 * d)
        hrow = h2_ref[pl.ds(t, 1), :]
        grow = gate_ref[pl.ds(t, 1), :]
        lane = lax.broadcasted_iota(jnp.int32, (ROW_TILE, ne), 1)
        sub = lax.broadcasted_iota(jnp.int32, (ROW_TILE, ne), 0)
        acc = jnp.zeros((ROW_TILE, d), F32)
        for g in range(ne // ROW_TILE):
            rows = tile[g * ROW_TILE:(g + 1) * ROW_TILE, :]
            pre = jnp.sum(rows[:, :d] * hrow, axis=1, keepdims=True)
            act = 0.5 * pre * (1.0 + lax.erf(pre * (2.0 ** -0.5)))
            gcol = jnp.sum(jnp.where(lane == sub + g * ROW_TILE, grow, 0.0), axis=1, keepdims=True)
            acc = acc + (act * gcol) * rows[:, d:]
        y = jnp.sum(acc, axis=0, keepdims=True)
        xo = x1_ref[pl.ds(t, 1), :] + gt_ref[...] * y
        xo = xo * lax.rsqrt(jnp.mean(xo * xo, axis=-1, keepdims=True) + NORM_EPS) * fg_ref[...]
        o_ref[pl.ds(t, 1), :] = xo

    @pl.when(step == 0)
    def _():
        for g in range(AHEAD):
            for j in range(GROUP):
                issue_token((g * GROUP + j) * ne, g, j)

    def ring_turn(it, carry):
        for gs in range(N_GSLOTS):
            t0 = (it * N_GSLOTS + gs) * GROUP
            wait_group(gs)
            for j in range(GROUP):
                issue_token((t0 + AHEAD * GROUP + j) * ne, (gs + AHEAD) % N_GSLOTS, j)
                evaluate(t0 + j, gs * GROUP + j)
        return carry

    lax.fori_loop(0, tt // N_SLOTS, ring_turn, 0)

    @pl.when(step == last_step)
    def _():
        for g in range(AHEAD):
            wait_group(g)


def peer_expert(idx_tok, h2, gate_tok, x1, gt2, final_g, table, seq, tt):
    ntok, d = h2.shape
    ne = PEER_HEADS * PEER_TOPK
    tiles_per_b = seq // tt
    n_steps = ntok // tt
    assert tt % N_SLOTS == 0
    idx_steps = idx_tok.reshape(n_steps, tt * ne)
    tail = jnp.roll(idx_steps[:, :AHEAD * GROUP * ne], -1, axis=0)
    win = (tt + AHEAD * GROUP) * ne
    idx_win = jnp.concatenate([idx_steps, tail], axis=1).reshape(n_steps * win)
    return pl.pallas_call(
        functools.partial(_peer_expert_kernel, tt=tt, d=d),
        grid=(n_steps,),
        in_specs=[pl.BlockSpec((win,), lambda i: (i,), memory_space=pltpu.SMEM),
                  pl.BlockSpec((tt, d), lambda i: (i, 0)),
                  pl.BlockSpec((tt, ne), lambda i: (i, 0)),
                  pl.BlockSpec((tt, d), lambda i: (i, 0)),
                  pl.BlockSpec((None, 1, d), lambda i: (i // tiles_per_b, 0, 0)),
                  pl.BlockSpec((1, d), lambda i: (0, 0)),
                  pl.BlockSpec(memory_space=pl.ANY)],
        out_specs=pl.BlockSpec((tt, d), lambda i: (i, 0)),
        out_shape=jax.ShapeDtypeStruct((ntok, d), F32),
        scratch_shapes=([pltpu.VMEM((ne, 1, 2 * d), F32) for _ in range(N_SLOTS)]
                        + [pltpu.VMEM((ne, 2 * d), F32) for _ in range(GROUP)]
                        + [pltpu.SemaphoreType.DMA((N_GSLOTS,))]),
        compiler_params=_cparams(("arbitrary",)),
        name="peer_expert",
    )(idx_win, h2, gate_tok, x1, gt2, final_g.reshape(1, d), table)


def _fuse_tables_kernel(down_ref, up_ref, o_ref):
    d = down_ref.shape[1]
    o_ref[:, 0, :d] = down_ref[...]
    o_ref[:, 0, d:] = up_ref[...]


def fuse_expert_tables(down, up, te):
    n_exp, d = down.shape
    return pl.pallas_call(
        _fuse_tables_kernel,
        grid=(n_exp // te,),
        in_specs=[pl.BlockSpec((te, d), lambda i: (i, 0)), pl.BlockSpec((te, d), lambda i: (i, 0))],
        out_specs=pl.BlockSpec((te, 1, 2 * d), lambda i: (i, 0, 0)),
        out_shape=jax.ShapeDtypeStruct((n_exp, 1, 2 * d), F32),
        compiler_params=_cparams(("parallel",)),
        name="fuse_expert_tables",
    )(down, up)


ROW_TILE_TOKENS = 256
ROUTE_TILE_TOKENS = 1024
GATHER_STEP_TOKENS = 128


def kernel(x, c, ctx, c_ctx, ada_w, ada_b, norm1_g, w_in, gdn_conv_w, gdn_a_log, gdn_dt_bias,
           gdn_norm_w, rwkv_mu, rwkv_w0, rwkv_w2, rwkv_a0, rwkv_a2, rwkv_g2, rwkv_k_k, rwkv_k_a,
           rwkv_r_k, rwkv_gn_w, rwkv_gn_b, w_out, norm2_g, peer_w_query, peer_sub_keys,
           peer_down, peer_up, final_norm_g):
    bsz, seq, d = x.shape
    n_ctx = ctx.shape[1]
    assert ada_w.shape[0] == 1, "single-layer block"
    assert n_ctx % ROW_TILE_TOKENS == 0 and seq % ROUTE_TILE_TOKENS == 0 and seq % GRID_W == 0
    nh = GDN_HEADS
    n_qkvz = 4 * nh * GDN_DK
    gdn_cols = n_qkvz + 4 * nh

    cond = jnp.concatenate([c, c_ctx[None, :], jnp.zeros((16 - bsz - 1, d), F32)], axis=0)
    mods = ada_mod(cond, ada_w[0], ada_b[0])
    m_lat = mods[:bsz].reshape(bsz, N_MOD, 1, d)
    m_ctx = jnp.broadcast_to(mods[bsz].reshape(1, N_MOD, 1, d), (bsz, N_MOD, 1, d))
    shift1 = jnp.stack([m_ctx[:, 0], m_lat[:, 0]], axis=1)
    scale1 = jnp.stack([m_ctx[:, 1], m_lat[:, 1]], axis=1)
    gt1, sh2, sc2, gt2 = m_lat[:, 2], m_lat[:, 3], m_lat[:, 4], m_lat[:, 5]

    wl = w_in[0]
    w_gdn = jnp.concatenate([wl[:, :gdn_cols], jnp.zeros((d, LANES - 4 * nh), F32)], axis=1)
    w_rwkv = wl[:, gdn_cols:]
    p_gdn, p_rwkv = in_proj(ctx, x, norm1_g[0], shift1, scale1, w_gdn.astype(BF16),
                            w_rwkv.astype(BF16), ROW_TILE_TOKENS)

    qkv = gdn_prep(p_gdn, gdn_conv_w[0], n_ctx)
    prm = jnp.zeros((8, LANES), F32)
    prm = prm.at[0, :2 * nh].set(gdn_a_log[0].reshape(-1))
    prm = prm.at[1, :2 * nh].set(gdn_dt_bias[0].reshape(-1))
    o_f, o_b = gdn_chunk(qkv, p_gdn, prm, n_ctx)

    xs = rwkv_shift(p_rwkv, rwkv_mu[0], n_ctx)
    vecs = jnp.zeros((8, RWKV_W), F32)
    for i, vec in enumerate((rwkv_k_k, rwkv_k_a, rwkv_r_k, rwkv_gn_w, rwkv_gn_b)):
        vecs = vecs.at[i].set(vec[0])
    y_f, y_b = rwkv_chunk(xs, vecs, rwkv_w0[0], rwkv_w2[0], rwkv_a0[0], rwkv_a2[0], n_ctx)

    x1 = mix_out(x, gt1, o_f, o_b, p_gdn, y_f, y_b, xs, gdn_norm_w[0].reshape(1, GDN_DK), vecs,
                 rwkv_a0[0], rwkv_a2[0], rwkv_g2[0], w_out[0].astype(BF16), n_ctx, ROW_TILE_TOKENS)

    h2, idx_t, gate_t = peer_route(x1, norm2_g[0], sh2, sc2, peer_w_query[0].astype(BF16),
                                   peer_sub_keys[0], ROUTE_TILE_TOKENS)
    table = fuse_expert_tables(peer_down[0], peer_up[0], ROW_TILE_TOKENS)
    out = peer_expert(idx_t.T, h2, gate_t.T, x1.reshape(bsz * seq, d), gt2,
                      final_norm_g, table, seq, GATHER_STEP_TOKENS)
    return out.reshape(bsz, seq, d)
```

```python
import functools
import math

import jax
import jax.numpy as jnp
from jax import lax
from jax.experimental import pallas as pl
from jax.experimental.pallas import tpu as pltpu

F32 = jnp.float32
BF16 = jnp.bfloat16

GRID_W = 64
GDN_HEADS = 4
GDN_DK = 128
GDN_CONV = 5
CHUNK = 64
RWKV_HEADS = 8
RWKV_HD = 64
RWKV_W = RWKV_HEADS * RWKV_HD
LORA = 64
PEER_HEADS = 8
PEER_NKEYS = 128
PEER_TOPK = 16
N_MOD = 6

NORM_EPS = 1e-6
L2_EPS = 1e-6
RWKV_GN_EPS = 64e-5

LANES = 128
VMEM_LIMIT = 56 * 1024 * 1024
NEG_INF = float("-inf")


def _cparams(sem):
    return pltpu.CompilerParams(dimension_semantics=sem, vmem_limit_bytes=VMEM_LIMIT)


def _dot(a, b):
    return jnp.dot(a.astype(BF16), b.astype(BF16), preferred_element_type=F32)


def _dot_nt(a, b):
    return lax.dot_general(a.astype(BF16), b.astype(BF16), (((1,), (1,)), ((), ())),
                           preferred_element_type=F32)


def _dot_tn(a, b):
    return lax.dot_general(a.astype(BF16), b.astype(BF16), (((0,), (0,)), ((), ())),
                           preferred_element_type=F32)


def _dot_hi(a, b):
    return jnp.dot(a, b, preferred_element_type=F32, precision=lax.Precision.HIGHEST)


def _split3(x):
    hi = x.astype(BF16)
    rest = x - hi.astype(F32)
    mid = rest.astype(BF16)
    lo = (rest - mid.astype(F32)).astype(BF16)
    return hi, mid, lo


def _dot_x01(a, m01):
    m = m01.astype(BF16)
    return sum(jnp.dot(p, m, preferred_element_type=F32) for p in _split3(a))


def _dot_01x(m01, b):
    m = m01.astype(BF16)
    return sum(jnp.dot(m, p, preferred_element_type=F32) for p in _split3(b))


HEAD_SUM_SLAB = 256


def _head_sums(a, m01):
    w = m01.shape[0]
    return jnp.concatenate([_dot_x01(a[:, i:i + w], m01) for i in range(0, a.shape[1], w)], axis=1)


def _sigmoid(x):
    return 1.0 / (1.0 + jnp.exp(-x))


def _silu(x):
    return x * _sigmoid(x)


def _softplus(x):
    return jnp.maximum(x, 0.0) + jnp.log(1.0 + jnp.exp(-jnp.abs(x)))


def _unit_lower_inverses(xs):
    c = xs[0].shape[0]
    eye = (lax.broadcasted_iota(jnp.int32, (c, c), 0) ==
           lax.broadcasted_iota(jnp.int32, (c, c), 1)).astype(F32)
    prods = [eye + x for x in xs]
    pows = [_dot(x, x) for x in xs]
    for _ in range(int(math.log2(c)) - 2):
        both = [_dot(jnp.concatenate([p, x], axis=0), x) for p, x in zip(prods, pows)]
        prods = [p + b[:c] for p, b in zip(prods, both)]
        pows = [b[c:] for b in both]
    return [p + _dot(p, x) for p, x in zip(prods, pows)]


def _order_masks(c, rev):
    r = lax.broadcasted_iota(jnp.int32, (c, c), 0)
    s = lax.broadcasted_iota(jnp.int32, (c, c), 1)
    if rev:
        return s >= r, s > r
    return s <= r, s < r


def _ada_kernel(c_ref, w_ref, b_ref, o_ref):
    o_ref[...] = _dot_hi(_silu(c_ref[...]), w_ref[...]) + b_ref[...]


def ada_mod(cond, w, b):
    n, d = cond.shape
    cols = w.shape[1]
    tn = 1024
    return pl.pallas_call(
        _ada_kernel,
        grid=(cols // tn,),
        in_specs=[pl.BlockSpec((n, d), lambda j: (0, 0)),
                  pl.BlockSpec((d, tn), lambda j: (0, j)),
                  pl.BlockSpec((1, tn), lambda j: (0, j))],
        out_specs=pl.BlockSpec((n, tn), lambda j: (0, j)),
        out_shape=jax.ShapeDtypeStruct((n, cols), F32),
        compiler_params=_cparams(("arbitrary",)),
        name="ada_mod",
    )(cond, w, b.reshape(1, cols))


def _in_proj_kernel(ctx_ref, x_ref, g_ref, sh_ref, sc_ref, wg_ref, wr_ref, pg_ref, pr_ref,
                    *, nctx_tiles):
    x = jnp.where(pl.program_id(1) < nctx_tiles, ctx_ref[...], x_ref[...])
    y = x * lax.rsqrt(jnp.mean(x * x, axis=-1, keepdims=True) + NORM_EPS) * g_ref[...]
    h = (y * (1.0 + sc_ref[...]) + sh_ref[...]).astype(BF16)
    pg_ref[...] = jnp.dot(h, wg_ref[...], preferred_element_type=F32)
    pr_ref[...] = jnp.dot(h, wr_ref[...], preferred_element_type=F32)


def in_proj(ctx, x, norm_g, shift, scale, w_gdn, w_rwkv, tm):
    bsz, n_ctx, d = ctx.shape
    t = n_ctx + x.shape[1]
    cg, cr = w_gdn.shape[1], w_rwkv.shape[1]
    nctx_tiles = n_ctx // tm

    def mod_map(b, i):
        return (b, jnp.where(i < nctx_tiles, 0, 1), 0, 0)

    return pl.pallas_call(
        functools.partial(_in_proj_kernel, nctx_tiles=nctx_tiles),
        grid=(bsz, t // tm),
        in_specs=[pl.BlockSpec((None, tm, d), lambda b, i: (b, jnp.minimum(i, nctx_tiles - 1), 0)),
                  pl.BlockSpec((None, tm, d), lambda b, i: (b, jnp.maximum(i - nctx_tiles, 0), 0)),
                  pl.BlockSpec((1, d), lambda b, i: (0, 0)),
                  pl.BlockSpec((None, None, 1, d), mod_map),
                  pl.BlockSpec((None, None, 1, d), mod_map),
                  pl.BlockSpec((d, cg), lambda b, i: (0, 0)),
                  pl.BlockSpec((d, cr), lambda b, i: (0, 0))],
        out_specs=[pl.BlockSpec((None, tm, cg), lambda b, i: (b, i, 0)),
                   pl.BlockSpec((None, tm, cr), lambda b, i: (b, i, 0))],
        out_shape=[jax.ShapeDtypeStruct((bsz, t, cg), F32),
                   jax.ShapeDtypeStruct((bsz, t, cr), F32)],
        compiler_params=_cparams(("parallel", "parallel")),
        name="in_proj",
    )(ctx, x, norm_g.reshape(1, d), shift, scale, w_gdn, w_rwkv)


def _gdn_prep_kernel(p_ref, w_ref, o_ref, *, n_ctx):
    j = pl.program_id(1)
    x = p_ref[...]
    t_len = x.shape[0]
    w = w_ref[...]
    t = lax.broadcasted_iota(jnp.int32, x.shape, 0)
    is_ctx = t < n_ctx
    lo = jnp.where(is_ctx, 0, n_ctx)
    hi = jnp.where(is_ctx, n_ctx, t_len)
    pad = (GDN_CONV - 1) // 2
    acc = x * w[pad:pad + 1, :]
    for s in range(-pad, pad + 1):
        if s == 0:
            continue
        xs = pltpu.roll(x, (-s) % t_len, 0)
        valid = (t + s >= lo) & (t + s < hi)
        acc = acc + jnp.where(valid, xs, 0.0) * w[s + pad:s + pad + 1, :]
    y = _silu(acc)
    inv = lax.rsqrt(jnp.sum(y * y, axis=-1, keepdims=True) + L2_EPS)
    fac = jnp.where(j < GDN_HEADS, inv * (GDN_DK ** -0.5), jnp.where(j < 2 * GDN_HEADS, inv, 1.0))
    o_ref[...] = y * fac


def gdn_prep(p_gdn, conv_w, n_ctx):
    bsz, t, _ = p_gdn.shape
    ncol = conv_w.shape[1]
    return pl.pallas_call(
        functools.partial(_gdn_prep_kernel, n_ctx=n_ctx),
        grid=(bsz, ncol // LANES),
        in_specs=[pl.BlockSpec((None, t, LANES), lambda b, j: (b, 0, j)),
                  pl.BlockSpec((GDN_CONV, LANES), lambda b, j: (0, j))],
        out_specs=pl.BlockSpec((None, t, LANES), lambda b, j: (b, 0, j)),
        out_shape=jax.ShapeDtypeStruct((bsz, t, ncol), F32),
        compiler_params=_cparams(("parallel", "parallel")),
        name="gdn_prep",
    )(p_gdn, conv_w)


def _gdn_chunk_kernel(qf_ref, qb_ref, gf_ref, gb_ref, prm_ref, of_ref, ob_ref, s_ref):
    @pl.when(pl.program_id(1) == 0)
    def _():
        s_ref[...] = jnp.zeros_like(s_ref)

    nh, dk = GDN_HEADS, GDN_DK
    nb = qf_ref.shape[0]
    prm = prm_ref[...]
    a_log, dt_bias = prm[0:1, :], prm[1:2, :]
    q, k, v, beta, g_col, g_last, gamma, strict = [], [], [], [], [], [], [], []
    for bi, d in [(bi, d) for bi in range(nb) for d in range(2)]:
        qkv = (qf_ref, qb_ref)[d][bi]
        gates = (gf_ref, gb_ref)[d][bi]
        c = qkv.shape[0]
        incl, strict_d = _order_masks(c, d == 1)
        g_all = -jnp.exp(a_log) * _softplus(gates + dt_bias)
        beta_all = _sigmoid(gates)
        big_g = _dot_01x(incl.astype(F32), g_all)
        big_g_t = big_g.T
        last = 0 if d == 1 else c - 1
        for h in range(nh):
            col = d * nh + h
            q.append(qkv[:, h * dk:(h + 1) * dk])
            k.append(qkv[:, (nh + h) * dk:(nh + h + 1) * dk])
            v.append(qkv[:, (2 * nh + h) * dk:(2 * nh + h + 1) * dk])
            beta.append(beta_all[:, 2 * nh + col:2 * nh + col + 1])
            g_col.append(big_g[:, col:col + 1])
            g_last.append(big_g[last:last + 1, col:col + 1])
            g_row = big_g_t[col:col + 1, :]
            gamma.append(jnp.where(incl, jnp.exp(jnp.where(incl, g_col[-1] - g_row, 0.0)), 0.0))
            strict.append(strict_d)
    n = len(q)
    c = q[0].shape[0]
    e_g = [jnp.exp(g) for g in g_col]
    kq = [_dot_nt(jnp.concatenate([k[i], q[i]], axis=0), k[i]) for i in range(n)]
    a_mat = [jnp.where(strict[i], beta[i] * kq[i][:c] * gamma[i], 0.0) for i in range(n)]
    qk = [kq[i][c:] * gamma[i] for i in range(n)]
    t_inv = _unit_lower_inverses([-a for a in a_mat])
    uw = [_dot(t_inv[i], jnp.concatenate([beta[i] * v[i], (beta[i] * e_g[i]) * k[i]], axis=1))
          for i in range(n)]
    s = [s_ref[i // nh, i % nh] for i in range(n)]
    ws = [_dot(jnp.concatenate([uw[i][:, dk:], q[i] * e_g[i]], axis=0), s[i]) for i in range(n)]
    v_new = [uw[i][:, :dk] - ws[i][:c] for i in range(n)]
    o = [ws[i][c:] + _dot(qk[i], v_new[i]) for i in range(n)]
    s_new = [s[i] * jnp.exp(g_last[i]) + _dot_tn(k[i] * jnp.exp(g_last[i] - g_col[i]), v_new[i])
             for i in range(n)]
    for i in range(n):
        s_ref[i // nh, i % nh] = s_new[i]
    for bi in range(nb):
        base = 2 * nh * bi
        of_ref[bi] = jnp.concatenate(o[base:base + nh], axis=-1)
        ob_ref[bi] = jnp.concatenate(o[base + nh:base + 2 * nh], axis=-1)


GDN_BATCH_PER_STEP = 4
RWKV_BATCH_PER_STEP = 2


def _rev_chunk_map(n_ctx_chunks, n_chunks):
    def cb(s):
        return jnp.where(s < n_ctx_chunks, n_ctx_chunks - 1 - s, n_chunks + n_ctx_chunks - 1 - s)
    return cb


def gdn_chunk(qkv, p_gdn, prm, n_ctx):
    bsz, t, ncol = qkv.shape
    nv = GDN_HEADS * GDN_DK
    n_chunks = t // CHUNK
    cb = _rev_chunk_map(n_ctx // CHUNK, n_chunks)
    gate_tile = p_gdn.shape[2] // LANES - 1
    nb = math.gcd(bsz, GDN_BATCH_PER_STEP)
    return pl.pallas_call(
        _gdn_chunk_kernel,
        grid=(bsz // nb, n_chunks),
        in_specs=[pl.BlockSpec((nb, CHUNK, ncol), lambda b, s: (b, s, 0)),
                  pl.BlockSpec((nb, CHUNK, ncol), lambda b, s: (b, cb(s), 0)),
                  pl.BlockSpec((nb, CHUNK, LANES), lambda b, s: (b, s, gate_tile)),
                  pl.BlockSpec((nb, CHUNK, LANES), lambda b, s: (b, cb(s), gate_tile)),
                  pl.BlockSpec((8, LANES), lambda b, s: (0, 0))],
        out_specs=[pl.BlockSpec((nb, CHUNK, nv), lambda b, s: (b, s, 0)),
                   pl.BlockSpec((nb, CHUNK, nv), lambda b, s: (b, cb(s), 0))],
        out_shape=[jax.ShapeDtypeStruct((bsz, t, nv), F32)] * 2,
        scratch_shapes=[pltpu.VMEM((2 * nb, GDN_HEADS, GDN_DK, GDN_DK), F32)],
        compiler_params=_cparams(("parallel", "arbitrary")),
        name="gdn_chunk",
    )(qkv, qkv, p_gdn, p_gdn, prm)


def _rwkv_shift_kernel(p_ref, mu_ref, o_ref, *, n_ctx, n_cols):
    j = pl.program_id(1)
    x = p_ref[...]
    t_len = x.shape[0]
    t = lax.broadcasted_iota(jnp.int32, x.shape, 0)
    ch = lax.broadcasted_iota(jnp.int32, x.shape, 1) + j * LANES
    is_ctx = t < n_ctx
    col = (t - n_ctx) % GRID_W
    prev1 = pltpu.roll(x, 1, 0)
    next1 = pltpu.roll(x, t_len - 1, 0)
    up = pltpu.roll(x, GRID_W, 0)
    down = pltpu.roll(x, t_len - GRID_W, 0)
    quarter = n_cols // 4
    half = n_cols // 2
    ctx_sh = jnp.where(ch < half,
                       jnp.where(t >= 1, prev1, 0.0),
                       jnp.where(t < n_ctx - 1, next1, 0.0))
    lat_sh = jnp.where(ch < quarter, jnp.where(col >= 1, prev1, 0.0),
                       jnp.where(ch < 2 * quarter, jnp.where(col < GRID_W - 1, next1, 0.0),
                                 jnp.where(ch < 3 * quarter,
                                           jnp.where(t >= n_ctx + GRID_W, up, 0.0),
                                           jnp.where(t < t_len - GRID_W, down, 0.0))))
    sh = jnp.where(is_ctx, ctx_sh, lat_sh)
    o_ref[...] = x + mu_ref[...] * (sh - x)


def rwkv_shift(p_rwkv, mu, n_ctx):
    bsz, t, ncol = p_rwkv.shape
    return pl.pallas_call(
        functools.partial(_rwkv_shift_kernel, n_ctx=n_ctx, n_cols=ncol),
        grid=(bsz, ncol // LANES),
        in_specs=[pl.BlockSpec((None, t, LANES), lambda b, j: (b, 0, j)),
                  pl.BlockSpec((1, LANES), lambda b, j: (0, j))],
        out_specs=pl.BlockSpec((None, t, LANES), lambda b, j: (b, 0, j)),
        out_shape=jax.ShapeDtypeStruct((bsz, t, ncol), F32),
        compiler_params=_cparams(("parallel", "parallel")),
        name="rwkv_shift",
    )(p_rwkv, mu.reshape(1, ncol))


def _rwkv_gates(xs, d, w0_ref, w2_ref, a0_ref, a2_ref):
    w_ = RWKV_W
    xw = xs[:, 3 * w_ + d * LORA:3 * w_ + (d + 1) * LORA]
    xa = xs[:, 3 * w_ + 2 * LORA + d * LORA:3 * w_ + 2 * LORA + (d + 1) * LORA]
    w_pre = w0_ref[d:d + 1, :] + _dot(jnp.tanh(xw), w2_ref[d])
    log_w = -math.exp(-0.5) * _sigmoid(w_pre)
    a = _sigmoid(a0_ref[d:d + 1, :] + _dot(xa, a2_ref[d]))
    return log_w, a


def _rwkv_chunk_kernel(xf_ref, xb_ref, vec_ref, w0_ref, w2_ref, a0_ref, a2_ref, hsum_ref,
                       yf_ref, yb_ref, p_ref):
    @pl.when(pl.program_id(1) == 0)
    def _():
        p_ref[...] = jnp.zeros_like(p_ref)

    w_, n, nh = RWKV_W, RWKV_HD, RWKV_HEADS
    nb = xf_ref.shape[0]
    k_k, k_a = vec_ref[0:1, :], vec_ref[1:2, :]
    lhs, rhs_t, tail, vals, w_last, p0, strict, incl = [], [], [], [], [], [], [], []
    for bi, d in [(bi, d) for bi in range(nb) for d in range(2)]:
        xs = (xf_ref, xb_ref)[d][bi]
        c = xs.shape[0]
        incl_d, strict_d = _order_masks(c, d == 1)
        r = xs[:, 0:w_]
        k = xs[:, w_:2 * w_]
        v = xs[:, 2 * w_:3 * w_]
        log_w, a = _rwkv_gates(xs, d, w0_ref, w2_ref, a0_ref, a2_ref)
        kk = k * k_k
        kk = kk * lax.rsqrt(_head_sums(kk * kk, hsum_ref[...]) + L2_EPS)
        k_dir = k * (1.0 + (a - 1.0) * k_a)
        alpha = -(kk * a)
        big_g = _dot_01x(incl_d.astype(F32), log_w)
        last = 0 if d == 1 else c - 1
        g_last = big_g[last:last + 1, :]
        e_neg = jnp.exp(-big_g)
        e_tail = jnp.exp(g_last - big_g)
        r_t = r * jnp.exp(big_g)
        b_t = kk * jnp.exp(big_g - log_w)
        a_t = alpha * e_neg
        k_t = k_dir * e_neg
        a_h = alpha * e_tail
        k_h = k_dir * e_tail
        w_l = jnp.exp(g_last)
        for h in range(nh):
            sl = slice(h * n, (h + 1) * n)
            lhs.append(jnp.concatenate([b_t[:, sl], r_t[:, sl]], axis=0))
            rhs_t.append(jnp.concatenate([a_t[:, sl], k_t[:, sl]], axis=0))
            tail.append(jnp.concatenate([a_h[:, sl], k_h[:, sl]], axis=0))
            vals.append(v[:, sl])
            w_last.append(w_l[:, sl])
            p0.append(p_ref[2 * bi + d, h])
            strict.append(strict_d)
            incl.append(incl_d)
    m = len(lhs)
    c = vals[0].shape[0]
    quad = [_dot_nt(lhs[i], rhs_t[i]) for i in range(m)]
    a_ba = [jnp.where(strict[i], quad[i][:c, :c], 0.0) for i in range(m)]
    a_k = [jnp.concatenate([jnp.where(strict[i], quad[i][:c, c:], 0.0),
                            jnp.where(incl[i], quad[i][c:, c:], 0.0)], axis=0) for i in range(m)]
    a_ra = [jnp.where(incl[i], quad[i][c:, :c], 0.0) for i in range(m)]
    from_state = [_dot_nt(lhs[i], p0[i]) for i in range(m)]
    from_vals = [_dot(a_k[i], vals[i]) for i in range(m)]
    t_inv = _unit_lower_inverses(a_ba)
    u = [_dot(t_inv[i], from_state[i][:c] + from_vals[i][:c]) for i in range(m)]
    y = [from_state[i][c:] + _dot(a_ra[i], u[i]) + from_vals[i][c:] for i in range(m)]
    p_new = [p0[i] * w_last[i] + _dot_tn(jnp.concatenate([u[i], vals[i]], axis=0), tail[i])
             for i in range(m)]
    for i in range(m):
        p_ref[i // nh, i % nh] = p_new[i]
    for bi in range(nb):
        base = 2 * nh * bi
        yf_ref[bi] = jnp.concatenate(y[base:base + nh], axis=-1)
        yb_ref[bi] = jnp.concatenate(y[base + nh:base + 2 * nh], axis=-1)


def _head_sum_matrix(width, group):
    i = jnp.arange(width)
    return (i[:, None] // group == i[None, :] // group).astype(F32)


def rwkv_chunk(xs, vecs, w0, w2, a0, a2, n_ctx):
    bsz, t, ncol = xs.shape
    n_chunks = t // CHUNK
    cb = _rev_chunk_map(n_ctx // CHUNK, n_chunks)
    hsum = _head_sum_matrix(HEAD_SUM_SLAB, RWKV_HD)
    full = lambda *shape: pl.BlockSpec(shape, lambda b, s: (0,) * len(shape))
    nb = math.gcd(bsz, RWKV_BATCH_PER_STEP)
    return pl.pallas_call(
        _rwkv_chunk_kernel,
        grid=(bsz // nb, n_chunks),
        in_specs=[pl.BlockSpec((nb, CHUNK, ncol), lambda b, s: (b, s, 0)),
                  pl.BlockSpec((nb, CHUNK, ncol), lambda b, s: (b, cb(s), 0)),
                  full(8, RWKV_W), full(2, RWKV_W), full(2, LORA, RWKV_W),
                  full(2, RWKV_W), full(2, LORA, RWKV_W), full(HEAD_SUM_SLAB, HEAD_SUM_SLAB)],
        out_specs=[pl.BlockSpec((nb, CHUNK, RWKV_W), lambda b, s: (b, s, 0)),
                   pl.BlockSpec((nb, CHUNK, RWKV_W), lambda b, s: (b, cb(s), 0))],
        out_shape=[jax.ShapeDtypeStruct((bsz, t, RWKV_W), F32)] * 2,
        scratch_shapes=[pltpu.VMEM((2 * nb, RWKV_HEADS, RWKV_HD, RWKV_HD), F32)],
        compiler_params=_cparams(("parallel", "arbitrary")),
        name="rwkv_chunk",
    )(xs, xs, vecs, w0, w2, a0, a2, hsum)


def _mix_out_kernel(x_ref, gt_ref, of_ref, ob_ref, z_ref, yf_ref, yb_ref, xs_ref,
                    gnw_ref, vec_ref, a0_ref, a2_ref, g2_ref, hsum_ref, wo_ref, o_ref):
    nh, dv = GDN_HEADS, GDN_DK
    w_ = RWKV_W
    o = of_ref[...] + ob_ref[...]
    z = z_ref[...]
    parts = []
    for h in range(nh):
        oh = o[:, h * dv:(h + 1) * dv]
        oh = oh * lax.rsqrt(jnp.mean(oh * oh, axis=-1, keepdims=True) + NORM_EPS)
        parts.append(oh * gnw_ref[...] * _silu(z[:, h * dv:(h + 1) * dv]))
    xs = xs_ref[...]
    r = xs[:, 0:w_]
    k = xs[:, w_:2 * w_]
    v = xs[:, 2 * w_:3 * w_]
    xg = xs[:, 3 * w_ + 4 * LORA:]
    k_a, r_k = vec_ref[1:2, :], vec_ref[2:3, :]
    gn_w, gn_b = vec_ref[3:4, :], vec_ref[4:5, :]
    hsum = hsum_ref[...]
    y = yf_ref[...] + yb_ref[...]
    mean = _head_sums(y, hsum) * (1.0 / RWKV_HD)
    yc = y - mean
    var = _head_sums(yc * yc, hsum) * (1.0 / RWKV_HD)
    yn = yc * lax.rsqrt(var + RWKV_GN_EPS) * gn_w + gn_b
    rk = jnp.zeros_like(r)
    for d in range(2):
        xa = xs[:, 3 * w_ + 2 * LORA + d * LORA:3 * w_ + 2 * LORA + (d + 1) * LORA]
        a = _sigmoid(a0_ref[d:d + 1, :] + _dot(xa, a2_ref[d]))
        rk = rk + r * (k * (1.0 + (a - 1.0) * k_a)) * r_k
    bonus = _head_sums(rk, hsum) * v
    gate = _dot(_sigmoid(xg), g2_ref[...])
    parts.append((yn + bonus) * gate)
    mixed = jnp.concatenate(parts, axis=-1).astype(BF16)
    o_ref[...] = x_ref[...] + gt_ref[...] * jnp.dot(mixed, wo_ref[...], preferred_element_type=F32)


def mix_out(x, gt1, o_f, o_b, p_gdn, y_f, y_b, xs, gnw, vecs, a0, a2, g2, w_out, n_ctx, tm):
    bsz, seq, d = x.shape
    off = n_ctx // tm
    nv = GDN_HEADS * GDN_DK
    ncol = xs.shape[2]
    hsum = _head_sum_matrix(HEAD_SUM_SLAB, RWKV_HD)
    lat = lambda width, cblk=0: pl.BlockSpec((None, tm, width), lambda b, i: (b, i + off, cblk))
    full = lambda *shape: pl.BlockSpec(shape, lambda b, i: (0,) * len(shape))
    return pl.pallas_call(
        _mix_out_kernel,
        grid=(bsz, seq // tm),
        in_specs=[pl.BlockSpec((None, tm, d), lambda b, i: (b, i, 0)),
                  pl.BlockSpec((None, 1, d), lambda b, i: (b, 0, 0)),
                  lat(nv), lat(nv), lat(nv, 3), lat(RWKV_W), lat(RWKV_W), lat(ncol),
                  full(1, GDN_DK), full(8, RWKV_W), full(2, RWKV_W), full(2, LORA, RWKV_W),
                  full(2 * LORA, RWKV_W), full(HEAD_SUM_SLAB, HEAD_SUM_SLAB), full(nv + RWKV_W, d)],
        out_specs=pl.BlockSpec((None, tm, d), lambda b, i: (b, i, 0)),
        out_shape=jax.ShapeDtypeStruct((bsz, seq, d), F32),
        compiler_params=_cparams(("parallel", "parallel")),
        name="mix_out",
    )(x, gt1, o_f, o_b, p_gdn, y_f, y_b, xs, gnw, vecs, a0, a2, g2, hsum, w_out)


def _extract_top(s, riota, n_rows, vals_ref, idx_ref, k):
    for i in range(k):
        m = jnp.max(s, axis=0, keepdims=True)
        am = jnp.min(jnp.where(s == m, riota, float(n_rows)), axis=0, keepdims=True)
        vals_ref[i:i + 1, :] = m
        idx_ref[i:i + 1, :] = am
        s = jnp.where(riota == am, NEG_INF, s)


def _candidate_blocks(k):
    split = math.isqrt(k)
    blocks = [(False, i, 0, k // (i + 1)) for i in range(split)]
    j = 0
    while k // (j + 1) > split:
        blocks.append((True, j, split, k // (j + 1)))
        j += 1
    return blocks


def _candidate_rows(k):
    return sum(-(-(hi - lo) // 8) * 8 for _, _, lo, hi in _candidate_blocks(k))


def _top_list_rows(k):
    need = max(lo + -(-(hi - lo) // 8) * 8 for _, _, lo, hi in _candidate_blocks(k))
    return -(-need // 8) * 8


def _peer_route_kernel(x_ref, g_ref, sh_ref, sc_ref, wq_ref, keys_ref,
                       h2_ref, idx_ref, gate_ref,
                       q_scr, tv_scr, ti_scr, cs_scr, ci_scr, cf_scr, bv_scr, bi_scr):
    h = pl.program_id(1)
    dq = 2 * LANES
    kk = PEER_TOPK

    @pl.when(h == 0)
    def _():
        x = x_ref[...]
        y = x * lax.rsqrt(jnp.mean(x * x, axis=-1, keepdims=True) + NORM_EPS) * g_ref[...]
        h2 = y * (1.0 + sc_ref[...]) + sh_ref[...]
        h2_ref[...] = h2
        hb = h2.astype(BF16)
        for hh in range(PEER_HEADS):
            q_scr[hh] = jnp.dot(hb, wq_ref[:, hh * dq:(hh + 1) * dq], preferred_element_type=F32)

    q = q_scr[h]
    tm = q.shape[0]
    riota = lax.broadcasted_iota(jnp.int32, (PEER_NKEYS, tm), 0).astype(F32)
    for p in range(2):
        s_t = _dot_nt(keys_ref[p], q[:, p * LANES:(p + 1) * LANES])
        _extract_top(s_t, riota, PEER_NKEYS, tv_scr.at[p], ti_scr.at[p], kk)
    tv_scr[:, kk:, :] = jnp.zeros_like(tv_scr[:, kk:, :])
    ti_scr[:, kk:, :] = jnp.zeros_like(ti_scr[:, kk:, :])
    row = 0
    for fixed_j, fixed, lo, hi in _candidate_blocks(kk):
        n_pad = -(-(hi - lo) // 8) * 8
        r = lax.broadcasted_iota(jnp.int32, (n_pad, tm), 0).astype(F32)
        run, one = (0, 1) if fixed_j else (1, 0)
        vals = tv_scr[run, lo:lo + n_pad, :] + tv_scr[one, fixed:fixed + 1, :]
        run_i, one_i = ti_scr[run, lo:lo + n_pad, :], ti_scr[one, fixed:fixed + 1, :]
        if fixed_j:
            experts = run_i * float(PEER_NKEYS) + one_i
            flat = (r + float(lo)) * float(kk) + float(fixed)
        else:
            experts = one_i * float(PEER_NKEYS) + run_i
            flat = float(fixed * kk + lo) + r
        cs_scr[row:row + n_pad, :] = jnp.where(r < float(hi - lo), vals, NEG_INF)
        ci_scr[row:row + n_pad, :] = experts
        cf_scr[row:row + n_pad, :] = flat
        row += n_pad
    cand = cs_scr[...]
    cand_i = ci_scr[...]
    ciota = cf_scr[...]
    for i in range(kk):
        m = jnp.max(cand, axis=0, keepdims=True)
        pos = jnp.min(jnp.where(cand == m, ciota, float(kk * kk)), axis=0, keepdims=True)
        hit = ciota == pos
        bv_scr[i:i + 1, :] = m
        bi_scr[i:i + 1, :] = jnp.sum(jnp.where(hit, cand_i, 0.0), axis=0, keepdims=True)
        cand = jnp.where(hit, NEG_INF, cand)
    best = bv_scr[...]
    e = jnp.exp(best - best[0:1, :])
    gate_ref[...] = e / jnp.sum(e, axis=0, keepdims=True)
    idx_ref[...] = bi_scr[...].astype(jnp.int32)


def peer_route(x1, norm_g, sh2, sc2, wq, sub_keys, tm):
    bsz, seq, d = x1.shape
    ntok = bsz * seq
    tiles_per_b = seq // tm
    kk = PEER_TOPK
    nq = wq.shape[1]
    return pl.pallas_call(
        _peer_route_kernel,
        grid=(ntok // tm, PEER_HEADS),
        in_specs=[pl.BlockSpec((tm, d), lambda i, h: (i, 0)),
                  pl.BlockSpec((1, d), lambda i, h: (0, 0)),
                  pl.BlockSpec((None, 1, d), lambda i, h: (i // tiles_per_b, 0, 0)),
                  pl.BlockSpec((None, 1, d), lambda i, h: (i // tiles_per_b, 0, 0)),
                  pl.BlockSpec((d, nq), lambda i, h: (0, 0)),
                  pl.BlockSpec((None, 2, PEER_NKEYS, LANES), lambda i, h: (h, 0, 0, 0))],
        out_specs=[pl.BlockSpec((tm, d), lambda i, h: (i, 0)),
                   pl.BlockSpec((kk, tm), lambda i, h: (h, i)),
                   pl.BlockSpec((kk, tm), lambda i, h: (h, i))],
        out_shape=[jax.ShapeDtypeStruct((ntok, d), F32),
                   jax.ShapeDtypeStruct((PEER_HEADS * kk, ntok), jnp.int32),
                   jax.ShapeDtypeStruct((PEER_HEADS * kk, ntok), F32)],
        scratch_shapes=[pltpu.VMEM((PEER_HEADS, tm, 2 * LANES), F32),
                        pltpu.VMEM((2, _top_list_rows(kk), tm), F32),
                        pltpu.VMEM((2, _top_list_rows(kk), tm), F32),
                        pltpu.VMEM((_candidate_rows(kk), tm), F32),
                        pltpu.VMEM((_candidate_rows(kk), tm), F32),
                        pltpu.VMEM((_candidate_rows(kk), tm), F32),
                        pltpu.VMEM((kk, tm), F32), pltpu.VMEM((kk, tm), F32)],
        compiler_params=_cparams(("parallel", "arbitrary")),
        name="peer_route",
    )(x1.reshape(ntok, d), norm_g.reshape(1, d), sh2, sc2, wq, sub_keys)


GROUP = 8
N_GSLOTS = 4
AHEAD = 2
N_SLOTS = GROUP * N_GSLOTS
ROW_TILE = 8
DMA_QUEUES = 2


def _peer_expert_kernel(idx_ref, h2_ref, gate_ref, x1_ref, gt_ref, fg_ref, tbl_ref,
                        o_ref, *scratch, tt, d):
    bufs, tiles, sem = scratch[:N_SLOTS], scratch[N_SLOTS:N_SLOTS + GROUP], scratch[N_SLOTS + GROUP]
    ne = PEER_HEADS * PEER_TOPK
    step = pl.program_id(0)
    last_step = pl.num_programs(0) - 1

    def issue_token(base, gslot, j):
        slot = gslot * GROUP + j
        for k in range(ne):
            e = idx_ref[base + k]
            pltpu.make_async_copy(tbl_ref.at[e], bufs[slot].at[k],
                                  sem.at[gslot]).start(priority=k % DMA_QUEUES)

    def wait_group(gslot):
        for j in range(GROUP):
            pltpu.make_async_copy(tbl_ref.at[pl.ds(0, ne)], bufs[gslot * GROUP + j],
                                  sem.at[gslot]).wait()

    def evaluate(t, slot):
        tile = tiles[slot % GROUP]
        tile[...] = bufs[slot][...].reshape(ne, 2 * d)
        hrow = h2_ref[pl.ds(t, 1), :]
        grow = gate_ref[pl.ds(t, 1), :]
        lane = lax.broadcasted_iota(jnp.int32, (ROW_TILE, ne), 1)
        sub = lax.broadcasted_iota(jnp.int32, (ROW_TILE, ne), 0)
        acc = jnp.zeros((ROW_TILE, d), F32)
        for g in range(ne // ROW_TILE):
            rows = tile[g * ROW_TILE:(g + 1) * ROW_TILE, :]
            pre = jnp.sum(rows[:, :d] * hrow, axis=1, keepdims=True)
            act = 0.5 * pre * (1.0 + lax.erf(pre * (2.0 ** -0.5)))
            gcol = jnp.sum(jnp.where(lane == sub + g * ROW_TILE, grow, 0.0), axis=1, keepdims=True)
            acc = acc + (act * gcol) * rows[:, d:]
        y = jnp.sum(acc, axis=0, keepdims=True)
        xo = x1_ref[pl.ds(t, 1), :] + gt_ref[...] * y
        xo = xo * lax.rsqrt(jnp.mean(xo * xo, axis=-1, keepdims=True) + NORM_EPS) * fg_ref[...]
        o_ref[pl.ds(t, 1), :] = xo

    @pl.when(step == 0)
    def _():
        for g in range(AHEAD):
            for j in range(GROUP):
                issue_token((g * GROUP + j) * ne, g, j)

    def ring_turn(it, carry):
        for gs in range(N_GSLOTS):
            t0 = (it * N_GSLOTS + gs) * GROUP
            wait_group(gs)
            for j in range(GROUP):
                issue_token((t0 + AHEAD * GROUP + j) * ne, (gs + AHEAD) % N_GSLOTS, j)
                evaluate(t0 + j, gs * GROUP + j)
        return carry

    lax.fori_loop(0, tt // N_SLOTS, ring_turn, 0)

    @pl.when(step == last_step)
    def _():
        for g in range(AHEAD):
            wait_group(g)


def peer_expert(idx_tok, h2, gate_tok, x1, gt2, final_g, table, seq, tt):
    ntok, d = h2.shape
    ne = PEER_HEADS * PEER_TOPK
    tiles_per_b = seq // tt
    n_steps = ntok // tt
    assert tt % N_SLOTS == 0
    idx_steps = idx_tok.reshape(n_steps, tt * ne)
    tail = jnp.roll(idx_steps[:, :AHEAD * GROUP * ne], -1, axis=0)
    win = (tt + AHEAD * GROUP) * ne
    idx_win = jnp.concatenate([idx_steps, tail], axis=1).reshape(n_steps * win)
    return pl.pallas_call(
        functools.partial(_peer_expert_kernel, tt=tt, d=d),
        grid=(n_steps,),
        in_specs=[pl.BlockSpec((win,), lambda i: (i,), memory_space=pltpu.SMEM),
                  pl.BlockSpec((tt, d), lambda i: (i, 0)),
                  pl.BlockSpec((tt, ne), lambda i: (i, 0)),
                  pl.BlockSpec((tt, d), lambda i: (i, 0)),
                  pl.BlockSpec((None, 1, d), lambda i: (i // tiles_per_b, 0, 0)),
                  pl.BlockSpec((1, d), lambda i: (0, 0)),
                  pl.BlockSpec(memory_space=pl.ANY)],
        out_specs=pl.BlockSpec((tt, d), lambda i: (i, 0)),
        out_shape=jax.ShapeDtypeStruct((ntok, d), F32),
        scratch_shapes=([pltpu.VMEM((ne, 1, 2 * d), F32) for _ in range(N_SLOTS)]
                        + [pltpu.VMEM((ne, 2 * d), F32) for _ in range(GROUP)]
                        + [pltpu.SemaphoreType.DMA((N_GSLOTS,))]),
        compiler_params=_cparams(("arbitrary",)),
        name="peer_expert",
    )(idx_win, h2, gate_tok, x1, gt2, final_g.reshape(1, d), table)


def _fuse_tables_kernel(down_ref, up_ref, o_ref):
    d = down_ref.shape[1]
    o_ref[:, 0, :d] = down_ref[...]
    o_ref[:, 0, d:] = up_ref[...]


def fuse_expert_tables(down, up, te):
    n_exp, d = down.shape
    return pl.pallas_call(
        _fuse_tables_kernel,
        grid=(n_exp // te,),
        in_specs=[pl.BlockSpec((te, d), lambda i: (i, 0)), pl.BlockSpec((te, d), lambda i: (i, 0))],
        out_specs=pl.BlockSpec((te, 1, 2 * d), lambda i: (i, 0, 0)),
        out_shape=jax.ShapeDtypeStruct((n_exp, 1, 2 * d), F32),
        compiler_params=_cparams(("parallel",)),
        name="fuse_expert_tables",
    )(down, up)


ROW_TILE_TOKENS = 256
ROUTE_TILE_TOKENS = 1024
GATHER_STEP_TOKENS = 128


def kernel(x, c, ctx, c_ctx, ada_w, ada_b, norm1_g, w_in, gdn_conv_w, gdn_a_log, gdn_dt_bias,
           gdn_norm_w, rwkv_mu, rwkv_w0, rwkv_w2, rwkv_a0, rwkv_a2, rwkv_g2, rwkv_k_k, rwkv_k_a,
           rwkv_r_k, rwkv_gn_w, rwkv_gn_b, w_out, norm2_g, peer_w_query, peer_sub_keys,
           peer_down, peer_up, final_norm_g):
    bsz, seq, d = x.shape
    n_ctx = ctx.shape[1]
    assert ada_w.shape[0] == 1, "single-layer block"
    assert n_ctx % ROW_TILE_TOKENS == 0 and seq % ROUTE_TILE_TOKENS == 0 and seq % GRID_W == 0
    nh = GDN_HEADS
    n_qkvz = 4 * nh * GDN_DK
    gdn_cols = n_qkvz + 4 * nh

    cond = jnp.concatenate([c, c_ctx[None, :], jnp.zeros((16 - bsz - 1, d), F32)], axis=0)
    mods = ada_mod(cond, ada_w[0], ada_b[0])
    m_lat = mods[:bsz].reshape(bsz, N_MOD, 1, d)
    m_ctx = jnp.broadcast_to(mods[bsz].reshape(1, N_MOD, 1, d), (bsz, N_MOD, 1, d))
    shift1 = jnp.stack([m_ctx[:, 0], m_lat[:, 0]], axis=1)
    scale1 = jnp.stack([m_ctx[:, 1], m_lat[:, 1]], axis=1)
    gt1, sh2, sc2, gt2 = m_lat[:, 2], m_lat[:, 3], m_lat[:, 4], m_lat[:, 5]

    wl = w_in[0]
    w_gdn = jnp.concatenate([wl[:, :gdn_cols], jnp.zeros((d, LANES - 4 * nh), F32)], axis=1)
    w_rwkv = wl[:, gdn_cols:]
    p_gdn, p_rwkv = in_proj(ctx, x, norm1_g[0], shift1, scale1, w_gdn.astype(BF16),
                            w_rwkv.astype(BF16), ROW_TILE_TOKENS)

    qkv = gdn_prep(p_gdn, gdn_conv_w[0], n_ctx)
    prm = jnp.zeros((8, LANES), F32)
    prm = prm.at[0, :2 * nh].set(gdn_a_log[0].reshape(-1))
    prm = prm.at[1, :2 * nh].set(gdn_dt_bias[0].reshape(-1))
    o_f, o_b = gdn_chunk(qkv, p_gdn, prm, n_ctx)

    xs = rwkv_shift(p_rwkv, rwkv_mu[0], n_ctx)
    vecs = jnp.zeros((8, RWKV_W), F32)
    for i, vec in enumerate((rwkv_k_k, rwkv_k_a, rwkv_r_k, rwkv_gn_w, rwkv_gn_b)):
        vecs = vecs.at[i].set(vec[0])
    y_f, y_b = rwkv_chunk(xs, vecs, rwkv_w0[0], rwkv_w2[0], rwkv_a0[0], rwkv_a2[0], n_ctx)

    x1 = mix_out(x, gt1, o_f, o_b, p_gdn, y_f, y_b, xs, gdn_norm_w[0].reshape(1, GDN_DK), vecs,
                 rwkv_a0[0], rwkv_a2[0], rwkv_g2[0], w_out[0].astype(BF16), n_ctx, ROW_TILE_TOKENS)

    h2, idx_t, gate_t = peer_route(x1, norm2_g[0], sh2, sc2, peer_w_query[0].astype(BF16),
                                   peer_sub_keys[0], ROUTE_TILE_TOKENS)
    table = fuse_expert_tables(peer_down[0], peer_up[0], ROW_TILE_TOKENS)
    out = peer_expert(idx_t.T, h2, gate_t.T, x1.reshape(bsz * seq, d), gt2,
                      final_norm_g, table, seq, GATHER_STEP_TOKENS)
    return out.reshape(bsz, seq, d)
```
